```python
import math
import jax, jax.numpy as jnp
from jax import lax
import numpy as np

D_MODEL = 4096
BATCH = 2
SEQ = 4096
DEPTH = 4

N_MIXERS = 2
EPS = 1e-5

SSM_EXPAND = 2
D_INNER = SSM_EXPAND * D_MODEL
SSM_HEADDIM = 64
SSM_HEADS = D_INNER // SSM_HEADDIM
SSM_GROUPS = 8
HEADS_PER_GROUP = SSM_HEADS // SSM_GROUPS
SSM_STATE = 128
SSM_CONV = 4
CONV_DIM = D_INNER + 2 * SSM_GROUPS * SSM_STATE
CHUNK = 128
DT_MIN = 1e-3
DT_MAX = 1e-1

DIFF_HEAD_DIM = 128
DIFF_HEADS = D_MODEL // (2 * DIFF_HEAD_DIM)
Q_BLOCK = 128
ATTN_SCALE = DIFF_HEAD_DIM ** -0.5
ROPE_THETA = 10000.0

D_FF = 256 * ((8 * D_MODEL // 3 + 255) // 256)
FFN_CONV = 3

kernel_name = "hybrid_ssd_diffattn_convffn_trunk"


def rmsnorm(x, w):
    xf = x.astype(jnp.float32)
    y = xf * lax.rsqrt(jnp.mean(xf * xf, axis=-1, keepdims=True) + EPS)
    return (y * w.astype(jnp.float32)).astype(x.dtype)


def causal_dwconv(x, w, b):
    k_width = w.shape[0]
    length = x.shape[1]
    xp = jnp.pad(x, ((0, 0), (k_width - 1, 0), (0, 0)))
    return b + sum(w[k] * xp[:, k:k + length] for k in range(k_width))


def gated_group_rmsnorm(y, z, w):
    b, l, _ = y.shape
    g = (y.astype(jnp.float32) * jax.nn.silu(z.astype(jnp.float32))).reshape(b, l, SSM_GROUPS, -1)
    g = g * lax.rsqrt(jnp.mean(g * g, axis=-1, keepdims=True) + EPS)
    return (g.reshape(b, l, D_INNER) * w.astype(jnp.float32)).astype(y.dtype)


def ssd_chunked(x, a, bm, cm):
    b, l, g, h, p = x.shape
    n = bm.shape[-1]
    c = l // CHUNK
    x = x.reshape(b, c, CHUNK, g, h, p)
    bm = bm.reshape(b, c, CHUNK, g, n)
    cm = cm.reshape(b, c, CHUNK, g, n)
    a = a.reshape(b, c, CHUNK, g, h).transpose(0, 3, 4, 1, 2)
    a_cs = jnp.cumsum(a, axis=-1)
    causal = jnp.tri(CHUNK, dtype=bool)
    seg = a_cs[..., :, None] - a_cs[..., None, :]
    decay_in = jnp.exp(jnp.where(causal, seg, -jnp.inf))
    cb = jnp.einsum("bclgn,bcsgn->bcgls", cm, bm)
    y_diag = jnp.einsum("bcgls,bghcls,bcsghp->bclghp", cb, decay_in, x)
    decay_states = jnp.exp(a_cs[..., -1:] - a_cs)
    states = jnp.einsum("bcsgn,bghcs,bcsghp->bcghpn", bm, decay_states, x)
    chunk_cs = jnp.cumsum(jnp.pad(a_cs[..., -1], ((0, 0), (0, 0), (0, 0), (1, 0))), axis=-1)
    decay_chunk = jnp.exp(jnp.where(jnp.tri(c + 1, dtype=bool),
                                    chunk_cs[..., :, None] - chunk_cs[..., None, :], -jnp.inf))
    states = jnp.concatenate([jnp.zeros_like(states[:, :1]), states], axis=1)
    states = jnp.einsum("bghzc,bcghpn->bzghpn", decay_chunk, states)[:, :-1]
    y_off = jnp.einsum("bclgn,bcghpn,bghcl->bclghp", cm, states, jnp.exp(a_cs))
    return (y_diag + y_off).reshape(b, l, g, h, p)


def mamba2_mixer(u, w_in, conv_w, conv_b, dt_bias, a_log, d_skip, norm_w, w_out):
    b, l, _ = u.shape
    zxbcdt = u @ w_in
    z = zxbcdt[..., :D_INNER]
    xbc = zxbcdt[..., D_INNER:D_INNER + CONV_DIM]
    dt = zxbcdt[..., D_INNER + CONV_DIM:]
    xbc = jax.nn.silu(causal_dwconv(xbc, conv_w, conv_b))
    xs = xbc[..., :D_INNER].reshape(b, l, SSM_GROUPS, HEADS_PER_GROUP, SSM_HEADDIM)
    bm = xbc[..., D_INNER:D_INNER + SSM_GROUPS * SSM_STATE].reshape(b, l, SSM_GROUPS, SSM_STATE)
    cm = xbc[..., D_INNER + SSM_GROUPS * SSM_STATE:].reshape(b, l, SSM_GROUPS, SSM_STATE)
    dt = jax.nn.softplus(dt.astype(jnp.float32) + dt_bias.astype(jnp.float32))
    dt = dt.reshape(b, l, SSM_GROUPS, HEADS_PER_GROUP)
    a = -jnp.exp(a_log.astype(jnp.float32)).reshape(SSM_GROUPS, HEADS_PER_GROUP)
    xf = xs.astype(jnp.float32)
    y = ssd_chunked(xf * dt[..., None], dt * a, bm.astype(jnp.float32), cm.astype(jnp.float32))
    y = y + d_skip.astype(jnp.float32).reshape(SSM_GROUPS, HEADS_PER_GROUP)[:, :, None] * xf
    y = y.reshape(b, l, D_INNER).astype(u.dtype)
    return gated_group_rmsnorm(y, z, norm_w) @ w_out


def rotary_tables(positions):
    inv_freq = 1.0 / (ROPE_THETA ** (jnp.arange(0, DIFF_HEAD_DIM, 2, dtype=jnp.float32) / DIFF_HEAD_DIM))
    freqs = positions.astype(jnp.float32)[..., None] * inv_freq
    emb = jnp.concatenate([freqs, freqs], axis=-1)
    return jnp.cos(emb), jnp.sin(emb)


def apply_rope(t, cos, sin):
    c = cos[:, :, None, None, :]
    s = sin[:, :, None, None, :]
    half = DIFF_HEAD_DIM // 2
    tf = t.astype(jnp.float32)
    rot = jnp.concatenate([-tf[..., half:], tf[..., :half]], axis=-1)
    return (tf * c + rot * s).astype(t.dtype)


def diff_attention(u, cos, sin, w_qkv, lq1, lk1, lq2, lk2, subln_w, w_o, lambda_init):
    b, l, _ = u.shape
    qkv = u @ w_qkv
    q = qkv[..., :D_MODEL].reshape(b, l, DIFF_HEADS, 2, DIFF_HEAD_DIM)
    k = qkv[..., D_MODEL:2 * D_MODEL].reshape(b, l, DIFF_HEADS, 2, DIFF_HEAD_DIM)
    v = qkv[..., 2 * D_MODEL:].reshape(b, l, DIFF_HEADS, 2 * DIFF_HEAD_DIM)
    q = apply_rope(q, cos, sin)
    k = apply_rope(k, cos, sin)
    lam = (jnp.exp(jnp.sum(lq1.astype(jnp.float32) * lk1.astype(jnp.float32)))
           - jnp.exp(jnp.sum(lq2.astype(jnp.float32) * lk2.astype(jnp.float32))) + lambda_init)
    k = k.transpose(0, 2, 3, 1, 4)
    v = v.transpose(0, 2, 1, 3)
    n_blk = l // Q_BLOCK
    qb = q.reshape(b, n_blk, Q_BLOCK, DIFF_HEADS, 2, DIFF_HEAD_DIM).transpose(1, 0, 3, 4, 2, 5)
    k_pos = jnp.arange(l)

    def one_block(args):
        q_blk, blk = args
        s = jnp.einsum("bhcqd,bhckd->bhcqk", q_blk, k).astype(jnp.float32) * ATTN_SCALE
        q_pos = blk * Q_BLOCK + jnp.arange(Q_BLOCK)
        s = jnp.where(k_pos[None, :] <= q_pos[:, None], s, -jnp.inf)
        p = jax.nn.softmax(s, axis=-1)
        attn = p[:, :, 0] - lam * p[:, :, 1]
        return jnp.einsum("bhqk,bhkv->bhqv", attn.astype(v.dtype), v)

    o = lax.map(one_block, (qb, jnp.arange(n_blk)))
    o = rmsnorm(o, subln_w) * (1.0 - lambda_init)
    o = o.transpose(1, 0, 3, 2, 4).reshape(b, l, DIFF_HEADS * 2 * DIFF_HEAD_DIM)
    return o @ w_o


def conv_ffn(u, w_up, conv_w, conv_b, w_down):
    h = causal_dwconv(u @ w_up, conv_w, conv_b)
    return (jax.nn.silu(h[..., :D_FF]) * h[..., D_FF:]) @ w_down


def setup_inputs(seed: int = 0) -> dict:
    key = jax.random.key(seed)
    keys = list(jax.random.split(key, 96))
    kit = iter(keys)

    def nrm(shape, scale):
        return jax.random.normal(next(kit), shape, jnp.float32) * scale

    def gain(n):
        return 1.0 + nrm((n,), 0.02)

    inputs = {}
    inputs["x"] = nrm((BATCH, SEQ, D_MODEL), 1.0)
    inputs["positions"] = (jnp.arange(SEQ, dtype=jnp.int32)[None, :]
                           + jax.random.randint(next(kit), (BATCH, 1), 0, 1024, dtype=jnp.int32))
    for i in range(DEPTH):
        p = f"l{i}_"
        inputs[p + "norm_mix"] = gain(D_MODEL)
        if i % N_MIXERS == 0:
            inputs[p + "m_w_in"] = nrm((D_MODEL, 2 * D_INNER + 2 * SSM_GROUPS * SSM_STATE + SSM_HEADS), D_MODEL ** -0.5)
            inputs[p + "m_conv_w"] = nrm((SSM_CONV, CONV_DIM), SSM_CONV ** -0.5)
            inputs[p + "m_conv_b"] = nrm((CONV_DIM,), 0.02)
            dt = jnp.exp(jax.random.uniform(next(kit), (SSM_HEADS,), jnp.float32)
                         * (math.log(DT_MAX) - math.log(DT_MIN)) + math.log(DT_MIN))
            inputs[p + "m_dt_bias"] = dt + jnp.log(-jnp.expm1(-dt))
            inputs[p + "m_a_log"] = jnp.log(jax.random.uniform(next(kit), (SSM_HEADS,), jnp.float32, 1.0, 16.0))
            inputs[p + "m_d"] = 1.0 + nrm((SSM_HEADS,), 0.1)
            inputs[p + "m_norm"] = gain(D_INNER)
            inputs[p + "m_w_out"] = nrm((D_INNER, D_MODEL), D_INNER ** -0.5)
        else:
            inputs[p + "a_w_qkv"] = nrm((D_MODEL, 3 * D_MODEL), D_MODEL ** -0.5)
            inputs[p + "a_lq1"] = nrm((DIFF_HEAD_DIM,), 0.1)
            inputs[p + "a_lk1"] = nrm((DIFF_HEAD_DIM,), 0.1)
            inputs[p + "a_lq2"] = nrm((DIFF_HEAD_DIM,), 0.1)
            inputs[p + "a_lk2"] = nrm((DIFF_HEAD_DIM,), 0.1)
            inputs[p + "a_subln"] = gain(2 * DIFF_HEAD_DIM)
            inputs[p + "a_w_o"] = nrm((D_MODEL, D_MODEL), D_MODEL ** -0.5)
        inputs[p + "norm_ffn"] = gain(D_MODEL)
        inputs[p + "f_w_up"] = nrm((D_MODEL, 2 * D_FF), D_MODEL ** -0.5)
        inputs[p + "f_conv_w"] = nrm((FFN_CONV, 2 * D_FF), FFN_CONV ** -0.5)
        inputs[p + "f_conv_b"] = nrm((2 * D_FF,), 0.02)
        inputs[p + "f_w_down"] = nrm((D_FF, D_MODEL), D_FF ** -0.5)
    inputs["final_norm"] = gain(D_MODEL)
    return inputs


def reference(x, positions,
              l0_norm_mix, l0_m_w_in, l0_m_conv_w, l0_m_conv_b, l0_m_dt_bias, l0_m_a_log, l0_m_d, l0_m_norm, l0_m_w_out,
              l0_norm_ffn, l0_f_w_up, l0_f_conv_w, l0_f_conv_b, l0_f_w_down,
              l1_norm_mix, l1_a_w_qkv, l1_a_lq1, l1_a_lk1, l1_a_lq2, l1_a_lk2, l1_a_subln, l1_a_w_o,
              l1_norm_ffn, l1_f_w_up, l1_f_conv_w, l1_f_conv_b, l1_f_w_down,
              l2_norm_mix, l2_m_w_in, l2_m_conv_w, l2_m_conv_b, l2_m_dt_bias, l2_m_a_log, l2_m_d, l2_m_norm, l2_m_w_out,
              l2_norm_ffn, l2_f_w_up, l2_f_conv_w, l2_f_conv_b, l2_f_w_down,
              l3_norm_mix, l3_a_w_qkv, l3_a_lq1, l3_a_lk1, l3_a_lq2, l3_a_lk2, l3_a_subln, l3_a_w_o,
              l3_norm_ffn, l3_f_w_up, l3_f_conv_w, l3_f_conv_b, l3_f_w_down,
              final_norm):
    norm_mix = [l0_norm_mix, l1_norm_mix, l2_norm_mix, l3_norm_mix]
    norm_ffn = [l0_norm_ffn, l1_norm_ffn, l2_norm_ffn, l3_norm_ffn]
    mixer_params = [
        (l0_m_w_in, l0_m_conv_w, l0_m_conv_b, l0_m_dt_bias, l0_m_a_log, l0_m_d, l0_m_norm, l0_m_w_out),
        (l1_a_w_qkv, l1_a_lq1, l1_a_lk1, l1_a_lq2, l1_a_lk2, l1_a_subln, l1_a_w_o),
        (l2_m_w_in, l2_m_conv_w, l2_m_conv_b, l2_m_dt_bias, l2_m_a_log, l2_m_d, l2_m_norm, l2_m_w_out),
        (l3_a_w_qkv, l3_a_lq1, l3_a_lk1, l3_a_lq2, l3_a_lk2, l3_a_subln, l3_a_w_o),
    ]
    ffn_params = [
        (l0_f_w_up, l0_f_conv_w, l0_f_conv_b, l0_f_w_down),
        (l1_f_w_up, l1_f_conv_w, l1_f_conv_b, l1_f_w_down),
        (l2_f_w_up, l2_f_conv_w, l2_f_conv_b, l2_f_w_down),
        (l3_f_w_up, l3_f_conv_w, l3_f_conv_b, l3_f_w_down),
    ]
    cos, sin = rotary_tables(positions)
    for i in range(DEPTH):
        u = rmsnorm(x, norm_mix[i])
        if i % N_MIXERS == 0:
            x = x + mamba2_mixer(u, *mixer_params[i])
        else:
            lambda_init = 0.8 - 0.6 * math.exp(-0.3 * i)
            x = x + diff_attention(u, cos, sin, *mixer_params[i], lambda_init)
        x = x + conv_ffn(rmsnorm(x, norm_ffn[i]), *ffn_params[i])
    return rmsnorm(x, final_norm)
```

```python
import functools
import math

import jax
import jax.numpy as jnp
from jax import lax
from jax.experimental import pallas as pl
from jax.experimental.pallas import tpu as pltpu

F32 = jnp.float32
BF16 = jnp.bfloat16

EPS = 1e-5
N_MIXERS = 2

SSM_HEADDIM = 64
SSM_GROUPS = 8
SSM_STATE = 128
SSM_CONV = 4
CHUNK = 128

DIFF_HEAD_DIM = 128
ATTN_SCALE = DIFF_HEAD_DIM ** -0.5
ROPE_THETA = 10000.0

FFN_CONV = 3

V7X_VMEM_BYTES = 64 * 1024 * 1024
VMEM_LIMIT_BYTES = V7X_VMEM_BYTES - 8 * 1024 * 1024
SUBLANES = 8
BF16_ROWS = 16
LANES = 128


def _params(*sem):
    return pltpu.CompilerParams(dimension_semantics=sem, vmem_limit_bytes=VMEM_LIMIT_BYTES)


def _pick(dim, pref):
    if dim <= pref:
        return dim
    t = pref
    while dim % t:
        t //= 2
    return t


def _silu(x):
    return x / (1.0 + jnp.exp(-x))


def _rmsnorm_kernel(x_ref, w_ref, o_ref):
    x = x_ref[...]
    y = x * lax.rsqrt(jnp.mean(x * x, axis=-1, keepdims=True) + EPS)
    o_ref[...] = (y * w_ref[...]).astype(o_ref.dtype)


def rmsnorm(x, w, out_dtype):
    m, d = x.shape
    tm = _pick(m, 256)
    return pl.pallas_call(
        _rmsnorm_kernel,
        grid=(m // tm,),
        in_specs=[pl.BlockSpec((tm, d), lambda i: (i, 0)),
                  pl.BlockSpec((1, d), lambda i: (0, 0))],
        out_specs=pl.BlockSpec((tm, d), lambda i: (i, 0)),
        out_shape=jax.ShapeDtypeStruct((m, d), out_dtype),
        compiler_params=_params("arbitrary"),
        name="rmsnorm",
    )(x, w.reshape(1, d))


def _mm_kernel(a_ref, w_ref, o_ref):
    o_ref[...] = jnp.dot(a_ref[...], w_ref[...], preferred_element_type=F32).astype(o_ref.dtype)


def _mm_res_kernel(a_ref, w_ref, r_ref, o_ref):
    o_ref[...] = r_ref[...] + jnp.dot(a_ref[...], w_ref[...], preferred_element_type=F32)


def matmul(a, w, *, col0=0, n=None, tm, tn, out_dtype=F32, residual=None, single_buffer_a=False):
    m, k = a.shape
    n = w.shape[1] - col0 if n is None else n
    tm = _pick(m, tm)
    tn = _pick(n, tn)
    assert col0 % tn == 0
    jb = col0 // tn
    a_kwargs = dict(pipeline_mode=pl.Buffered(1)) if single_buffer_a else {}
    in_specs = [pl.BlockSpec((tm, k), lambda i, j: (i, 0), **a_kwargs),
                pl.BlockSpec((k, tn), lambda i, j: (0, j + jb))]
    args = [a, w]
    body = _mm_kernel
    if residual is not None:
        in_specs.append(pl.BlockSpec((tm, tn), lambda i, j: (i, j)))
        args.append(residual)
        body = _mm_res_kernel
    return pl.pallas_call(
        body,
        grid=(m // tm, n // tn),
        in_specs=in_specs,
        out_specs=pl.BlockSpec((tm, tn), lambda i, j: (i, j)),
        out_shape=jax.ShapeDtypeStruct((m, n), out_dtype),
        compiler_params=_params("arbitrary", "arbitrary"),
        name="matmul",
    )(*args)


def _shift_rows(ext, s, halo_rows):
    if s == 0:
        return ext[halo_rows:]
    return pltpu.roll(ext, s, axis=0)[halo_rows:]


def _causal_conv_ext(ext, halo_rows, w, b):
    k_width = w.shape[0]
    y = b + w[0:1] * _shift_rows(ext, k_width - 1, halo_rows)
    for k in range(1, k_width):
        y = y + w[k:k + 1] * _shift_rows(ext, k_width - 1 - k, halo_rows)
    return y


def _ffn_up_kernel(a_ref, ah_ref, wg_ref, wu_ref, cwg_ref, cwu_ref, cbg_ref, cbu_ref,
                   o_ref, aext_ref, *, tiles_per_seq):
    i = pl.program_id(0)
    j = pl.program_id(1)

    @pl.when(j == 0)
    def _():
        seq_start = (i % tiles_per_seq) == 0
        halo = ah_ref[...]
        aext_ref[0:BF16_ROWS, :] = jnp.where(seq_start, jnp.zeros_like(halo), halo)
        aext_ref[BF16_ROWS:, :] = a_ref[...]

    a_ext = aext_ref[...]

    def branch(w_ref, cw_ref, cb_ref):
        h_ext = jnp.dot(a_ext, w_ref[...], preferred_element_type=F32)
        return _causal_conv_ext(h_ext, BF16_ROWS, cw_ref[...], cb_ref[...])

    gate = branch(wg_ref, cwg_ref, cbg_ref)
    up = branch(wu_ref, cwu_ref, cbu_ref)
    o_ref[...] = (_silu(gate) * up).astype(o_ref.dtype)


def ffn_up(u, w_up, conv_w, conv_b, seq_len, *, tm=1024, tn=256):
    m, k = u.shape
    d_ff = w_up.shape[1] // 2
    tm = _pick(seq_len, tm)
    tn = _pick(d_ff, tn)
    nb = d_ff // tn
    hb = tm // BF16_ROWS
    kern = functools.partial(_ffn_up_kernel, tiles_per_seq=seq_len // tm)
    return pl.pallas_call(
        kern,
        grid=(m // tm, nb),
        in_specs=[
            pl.BlockSpec((tm, k), lambda i, j: (i, 0), pipeline_mode=pl.Buffered(1)),
            pl.BlockSpec((BF16_ROWS, k), lambda i, j: (jnp.maximum(i * hb - 1, 0), 0)),
            pl.BlockSpec((k, tn), lambda i, j: (0, j)),
            pl.BlockSpec((k, tn), lambda i, j: (0, j + nb)),
            pl.BlockSpec((FFN_CONV, tn), lambda i, j: (0, j)),
            pl.BlockSpec((FFN_CONV, tn), lambda i, j: (0, j + nb)),
            pl.BlockSpec((1, tn), lambda i, j: (0, j)),
            pl.BlockSpec((1, tn), lambda i, j: (0, j + nb)),
        ],
        out_specs=pl.BlockSpec((tm, tn), lambda i, j: (i, j)),
        out_shape=jax.ShapeDtypeStruct((m, d_ff), BF16),
        scratch_shapes=[pltpu.VMEM((tm + BF16_ROWS, k), BF16)],
        compiler_params=_params("arbitrary", "arbitrary"),
        name="ffn_up",
    )(u, u, w_up, w_up, conv_w, conv_w, conv_b.reshape(1, -1), conv_b.reshape(1, -1))


def _rope_table_kernel(pos_ref, inv_ref, cos_ref, sin_ref):
    freqs = pos_ref[...].astype(F32) * inv_ref[...]
    lane = lax.broadcasted_iota(jnp.int32, freqs.shape, 1)
    sign = jnp.where(lane < DIFF_HEAD_DIM // 2, -1.0, 1.0)
    cos_ref[...] = jnp.cos(freqs)
    sin_ref[...] = jnp.sin(freqs) * sign


def rope_tables(positions):
    m = positions.size
    inv = 1.0 / (ROPE_THETA ** (jnp.arange(0, DIFF_HEAD_DIM, 2, dtype=F32) / DIFF_HEAD_DIM))
    inv_full = jnp.concatenate([inv, inv]).reshape(1, DIFF_HEAD_DIM)
    tm = _pick(m, 1024)
    shape = jax.ShapeDtypeStruct((m, DIFF_HEAD_DIM), F32)
    return pl.pallas_call(
        _rope_table_kernel,
        grid=(m // tm,),
        in_specs=[pl.BlockSpec((tm, 1), lambda i: (i, 0)),
                  pl.BlockSpec((1, DIFF_HEAD_DIM), lambda i: (0, 0))],
        out_specs=[pl.BlockSpec((tm, DIFF_HEAD_DIM), lambda i: (i, 0))] * 2,
        out_shape=[shape, shape],
        compiler_params=_params("arbitrary"),
        name="rope_tables",
    )(positions.reshape(m, 1), inv_full)


def _qkv_rope_kernel(a_ref, w_ref, cos_ref, sin_ref, o_ref, *, rope_tiles):
    j = pl.program_id(1)
    acc = jnp.dot(a_ref[...], w_ref[...], preferred_element_type=F32)

    @pl.when(j < rope_tiles)
    def _():
        cos = cos_ref[...]
        sin = sin_ref[...]
        for c in range(acc.shape[1] // DIFF_HEAD_DIM):
            sl = slice(c * DIFF_HEAD_DIM, (c + 1) * DIFF_HEAD_DIM)
            t = acc[:, sl]
            rot = pltpu.roll(t, DIFF_HEAD_DIM // 2, axis=1)
            o_ref[:, sl] = (t * cos + rot * sin).astype(o_ref.dtype)

    @pl.when(j >= rope_tiles)
    def _():
        o_ref[...] = acc.astype(o_ref.dtype)


def qkv_rope(u, w_qkv, cos, sin, *, tm=1024, tn=1024):
    m, k = u.shape
    n = w_qkv.shape[1]
    d_model = n // 3
    tm = _pick(m, tm)
    tn = _pick(d_model, tn)
    kern = functools.partial(_qkv_rope_kernel, rope_tiles=2 * d_model // tn)
    return pl.pallas_call(
        kern,
        grid=(m // tm, n // tn),
        in_specs=[pl.BlockSpec((tm, k), lambda i, j: (i, 0)),
                  pl.BlockSpec((k, tn), lambda i, j: (0, j)),
                  pl.BlockSpec((tm, DIFF_HEAD_DIM), lambda i, j: (i, 0)),
                  pl.BlockSpec((tm, DIFF_HEAD_DIM), lambda i, j: (i, 0))],
        out_specs=pl.BlockSpec((tm, tn), lambda i, j: (i, j)),
        out_shape=jax.ShapeDtypeStruct((m, n), BF16),
        compiler_params=_params("arbitrary", "arbitrary"),
        name="qkv_rope",
    )(u, w_qkv, cos, sin)


def _diff_attn_kernel(q_ref, k_ref, v_ref, lq1_ref, lk1_ref, lq2_ref, lk2_ref, sw_ref, o_ref,
                      m_ref, l_ref, acc_ref, *, lambda_init, tq, tk):
    qi = pl.program_id(2)
    ki = pl.program_id(3)
    nk = pl.num_programs(3)
    d = DIFF_HEAD_DIM

    @pl.when(ki == 0)
    def _():
        m_ref[...] = jnp.full(m_ref.shape, -jnp.inf, F32)
        l_ref[...] = jnp.zeros(l_ref.shape, F32)
        acc_ref[...] = jnp.zeros(acc_ref.shape, F32)

    @pl.when(ki * tk < (qi + 1) * tq)
    def _():
        q = q_ref[...]
        k = k_ref[...]
        v = v_ref[...]
        q_pos = qi * tq + lax.broadcasted_iota(jnp.int32, (tq, tk), 0)
        k_pos = ki * tk + lax.broadcasted_iota(jnp.int32, (tq, tk), 1)
        causal = k_pos <= q_pos
        for c in range(2):
            s = lax.dot_general(q[:, c * d:(c + 1) * d], k[:, c * d:(c + 1) * d],
                                (((1,), (1,)), ((), ())), preferred_element_type=F32)
            s = jnp.where(causal, s * ATTN_SCALE, -jnp.inf)
            m_old = m_ref[c]
            m_new = jnp.maximum(m_old, jnp.max(s, axis=-1, keepdims=True))
            alpha = jnp.exp(m_old - m_new)
            p = jnp.exp(s - m_new)
            l_ref[c] = alpha * l_ref[c] + jnp.sum(p, axis=-1, keepdims=True)
            acc_ref[c] = alpha * acc_ref[c] + jnp.dot(p.astype(v.dtype), v, preferred_element_type=F32)
            m_ref[c] = m_new

    @pl.when(ki == nk - 1)
    def _():
        lam = (jnp.exp(jnp.sum(lq1_ref[...] * lk1_ref[...], axis=-1, keepdims=True))
               - jnp.exp(jnp.sum(lq2_ref[...] * lk2_ref[...], axis=-1, keepdims=True)) + lambda_init)
        o = acc_ref[0] / l_ref[0] - lam * (acc_ref[1] / l_ref[1])
        o = o * lax.rsqrt(jnp.mean(o * o, axis=-1, keepdims=True) + EPS)
        o_ref[...] = ((o * sw_ref[...]) * (1.0 - lambda_init)).astype(o_ref.dtype)


def diff_attention(qkv, lq1, lk1, lq2, lk2, subln_w, lambda_init, batch, seq_len, *, tq=512, tk=512):
    m, n = qkv.shape
    d_model = n // 3
    hw = 2 * DIFF_HEAD_DIM
    heads = d_model // hw
    tq = _pick(seq_len, tq)
    tk = _pick(seq_len, tk)
    nq = seq_len // tq
    nk = seq_len // tk

    def kv_block(b, qi, ki):
        last = ((qi + 1) * tq - 1) // tk
        return b * nk + jnp.minimum(ki, last)

    kern = functools.partial(_diff_attn_kernel, lambda_init=lambda_init, tq=tq, tk=tk)
    vec = lambda a: a.reshape(1, -1)
    lam_spec = pl.BlockSpec((1, DIFF_HEAD_DIM), lambda b, h, qi, ki: (0, 0))
    return pl.pallas_call(
        kern,
        grid=(batch, heads, nq, nk),
        in_specs=[
            pl.BlockSpec((tq, hw), lambda b, h, qi, ki: (b * nq + qi, h)),
            pl.BlockSpec((tk, hw), lambda b, h, qi, ki: (kv_block(b, qi, ki), heads + h)),
            pl.BlockSpec((tk, hw), lambda b, h, qi, ki: (kv_block(b, qi, ki), 2 * heads + h)),
            lam_spec, lam_spec, lam_spec, lam_spec,
            pl.BlockSpec((1, hw), lambda b, h, qi, ki: (0, 0)),
        ],
        out_specs=pl.BlockSpec((tq, hw), lambda b, h, qi, ki: (b * nq + qi, h)),
        out_shape=jax.ShapeDtypeStruct((m, d_model), BF16),
        scratch_shapes=[pltpu.VMEM((2, tq, 1), F32), pltpu.VMEM((2, tq, 1), F32),
                        pltpu.VMEM((2, tq, hw), F32)],
        compiler_params=_params("arbitrary", "arbitrary", "arbitrary", "arbitrary"),
        name="diff_attention",
    )(qkv, qkv, qkv, vec(lq1), vec(lk1), vec(lq2), vec(lk2), vec(subln_w))


def _dot_exact(a, b):
    return jnp.dot(a, b, preferred_element_type=F32, precision=lax.Precision.HIGHEST)


def _ssd_kernel(z_ref, x_ref, xh_ref, b_ref, bh_ref, c_ref, ch_ref, dt_ref,
                cwx_ref, cwb_ref, cwc_ref, cbx_ref, cbb_ref, cbc_ref,
                dtb_ref, alog_ref, dexp_ref, nw_ref, o_ref, state_ref, *, heads_per_group):
    g = pl.program_id(1)
    c = pl.program_id(2)
    t = CHUNK
    hpg = heads_per_group
    p = SSM_HEADDIM
    gw = hpg * p
    nh = dt_ref.shape[1]

    @pl.when(c == 0)
    def _():
        state_ref[...] = jnp.zeros(state_ref.shape, F32)

    seq_start = c == 0

    def conv_silu(cur_ref, halo_ref, w_ref, bias_ref):
        halo = halo_ref[...]
        halo = jnp.where(seq_start, jnp.zeros_like(halo), halo)
        ext = jnp.concatenate([halo, cur_ref[...]], axis=0)
        return _silu(_causal_conv_ext(ext, SUBLANES, w_ref[...], bias_ref[...]))

    xc = conv_silu(x_ref, xh_ref, cwx_ref, cbx_ref)
    bc = conv_silu(b_ref, bh_ref, cwb_ref, cbb_ref)
    cc = conv_silu(c_ref, ch_ref, cwc_ref, cbc_ref)

    dtr = dt_ref[...] + dtb_ref[...]
    dtv = jnp.maximum(dtr, 0.0) + jnp.log1p(jnp.exp(-jnp.abs(dtr)))
    a_all = dtv * (-jnp.exp(alog_ref[...]))

    row_h = lax.broadcasted_iota(jnp.int32, (nh, nh), 0)
    col_h = lax.broadcasted_iota(jnp.int32, (nh, nh), 1)
    sel = jnp.where((row_h == g * hpg + col_h) & (col_h < hpg), 1.0, 0.0)
    a_g = _dot_exact(a_all, sel)
    dt_g = _dot_exact(dtv, sel)

    row_t = lax.broadcasted_iota(jnp.int32, (t, t), 0)
    col_t = lax.broadcasted_iota(jnp.int32, (t, t), 1)
    causal = col_t <= row_t
    a_cs = _dot_exact(jnp.where(causal, 1.0, 0.0), a_g)
    a_cs_t = a_cs.T
    a_last = a_cs[t - 1:t, :]

    row_e = lax.broadcasted_iota(jnp.int32, (nh, gw), 0)
    col_e = lax.broadcasted_iota(jnp.int32, (nh, gw), 1)
    expand = jnp.where((col_e >= row_e * p) & (col_e < (row_e + 1) * p), 1.0, 0.0)
    dt_e = _dot_exact(dt_g, expand)
    exp_acs_e = _dot_exact(jnp.exp(a_cs), expand)
    dec_e = _dot_exact(jnp.exp(a_last - a_cs), expand)

    xdt = xc * dt_e
    xdt_b = xdt.astype(BF16)
    bb = bc.astype(BF16)
    cb = cc.astype(BF16)
    cbm = lax.dot_general(cb, bb, (((1,), (1,)), ((), ())), preferred_element_type=F32)

    state = state_ref[...]
    y = jnp.dot(cb, state.astype(BF16), preferred_element_type=F32) * exp_acs_e

    lane = lax.broadcasted_iota(jnp.int32, (t, 2 * p), 1)
    pieces = []
    for pair in range(hpg // 2):
        x_pair = xdt_b[:, pair * 2 * p:(pair + 1) * 2 * p]
        outs = []
        for hh in (2 * pair, 2 * pair + 1):
            seg = a_cs[:, hh:hh + 1] - a_cs_t[hh:hh + 1, :]
            decay = jnp.exp(jnp.where(causal, seg, -jnp.inf))
            outs.append(jnp.dot((cbm * decay).astype(BF16), x_pair, preferred_element_type=F32))
        pieces.append(jnp.where(lane < p, outs[0], outs[1]))
    y = y + jnp.concatenate(pieces, axis=1) + dexp_ref[...] * xc

    new_contrib = jnp.dot(bc.T.astype(BF16), (xdt * dec_e).astype(BF16), preferred_element_type=F32)
    state_ref[...] = state * exp_acs_e[t - 1:t, :] + new_contrib

    gn = y * _silu(z_ref[...])
    gn = gn * lax.rsqrt(jnp.mean(gn * gn, axis=-1, keepdims=True) + EPS)
    o_ref[...] = (gn * nw_ref[...]).astype(o_ref.dtype)


def ssd_mixer(zx, bcm, dt, conv_w, conv_b, dt_bias, a_log, d_skip, norm_w, batch, seq_len):
    m = zx.shape[0]
    d_inner = zx.shape[1] // 2
    nh = dt.shape[1]
    hpg = nh // SSM_GROUPS
    gw = d_inner // SSM_GROUPS
    n = SSM_STATE
    t = CHUNK
    nc = seq_len // t
    hb = t // SUBLANES
    ng = SSM_GROUPS
    xb0 = d_inner // gw
    cwb0 = d_inner // n

    def row(b, g, c):
        return b * nc + c

    def halo(b, g, c):
        return jnp.maximum((b * nc + c) * hb - 1, 0)

    d_exp = jnp.repeat(d_skip.astype(F32), SSM_HEADDIM).reshape(1, d_inner)
    conv_b2 = conv_b.reshape(1, -1)
    kern = functools.partial(_ssd_kernel, heads_per_group=hpg)
    return pl.pallas_call(
        kern,
        grid=(batch, ng, nc),
        in_specs=[
            pl.BlockSpec((t, gw), lambda b, g, c: (row(b, g, c), g)),
            pl.BlockSpec((t, gw), lambda b, g, c: (row(b, g, c), xb0 + g)),
            pl.BlockSpec((SUBLANES, gw), lambda b, g, c: (halo(b, g, c), xb0 + g)),
            pl.BlockSpec((t, n), lambda b, g, c: (row(b, g, c), g)),
            pl.BlockSpec((SUBLANES, n), lambda b, g, c: (halo(b, g, c), g)),
            pl.BlockSpec((t, n), lambda b, g, c: (row(b, g, c), ng + g)),
            pl.BlockSpec((SUBLANES, n), lambda b, g, c: (halo(b, g, c), ng + g)),
            pl.BlockSpec((t, nh), lambda b, g, c: (row(b, g, c), 0)),
            pl.BlockSpec((SSM_CONV, gw), lambda b, g, c: (0, g)),
            pl.BlockSpec((SSM_CONV, n), lambda b, g, c: (0, cwb0 + g)),
            pl.BlockSpec((SSM_CONV, n), lambda b, g, c: (0, cwb0 + ng + g)),
            pl.BlockSpec((1, gw), lambda b, g, c: (0, g)),
            pl.BlockSpec((1, n), lambda b, g, c: (0, cwb0 + g)),
            pl.BlockSpec((1, n), lambda b, g, c: (0, cwb0 + ng + g)),
            pl.BlockSpec((1, nh), lambda b, g, c: (0, 0)),
            pl.BlockSpec((1, nh), lambda b, g, c: (0, 0)),
            pl.BlockSpec((1, gw), lambda b, g, c: (0, g)),
            pl.BlockSpec((1, gw), lambda b, g, c: (0, g)),
        ],
        out_specs=pl.BlockSpec((t, gw), lambda b, g, c: (row(b, g, c), g)),
        out_shape=jax.ShapeDtypeStruct((m, d_inner), BF16),
        scratch_shapes=[pltpu.VMEM((n, gw), F32)],
        compiler_params=_params("arbitrary", "arbitrary", "arbitrary"),
        name="ssd_mixer",
    )(zx, zx, zx, bcm, bcm, bcm, bcm, dt, conv_w, conv_w, conv_w, conv_b2, conv_b2, conv_b2,
      dt_bias.reshape(1, nh), a_log.reshape(1, nh), d_exp, norm_w.reshape(1, d_inner))


def mamba_layer(xf, u, w_in, conv_w, conv_b, dt_bias, a_log, d_skip, norm_w, w_out, batch, seq_len):
    nh = dt_bias.shape[0]
    d_inner = norm_w.shape[0]
    bc_cols = 2 * SSM_GROUPS * SSM_STATE
    w_in = w_in.astype(BF16)
    zx = matmul(u, w_in, col0=0, n=2 * d_inner, tm=1024, tn=1024)
    bcm = matmul(u, w_in, col0=2 * d_inner, n=bc_cols, tm=1024, tn=1024)
    dt = matmul(u, w_in, col0=2 * d_inner + bc_cols, n=nh, tm=1024, tn=nh)
    yn = ssd_mixer(zx, bcm, dt, conv_w, conv_b, dt_bias, a_log, d_skip, norm_w, batch, seq_len)
    return matmul(yn, w_out.astype(BF16), tm=1024, tn=512, residual=xf, single_buffer_a=True)


def attention_layer(xf, u, cos, sin, w_qkv, lq1, lk1, lq2, lk2, subln_w, w_o, lambda_init, batch, seq_len):
    qkv = qkv_rope(u, w_qkv.astype(BF16), cos, sin)
    o = diff_attention(qkv, lq1, lk1, lq2, lk2, subln_w, lambda_init, batch, seq_len)
    return matmul(o, w_o.astype(BF16), tm=1024, tn=1024, residual=xf)


def ffn_layer(xf, u, w_up, conv_w, conv_b, w_down, seq_len):
    g = ffn_up(u, w_up.astype(BF16), conv_w, conv_b, seq_len)
    return matmul(g, w_down.astype(BF16), tm=1024, tn=256, residual=xf, single_buffer_a=True)


def kernel(x, positions, l0_norm_mix, l0_m_w_in, l0_m_conv_w, l0_m_conv_b, l0_m_dt_bias, l0_m_a_log, l0_m_d, l0_m_norm, l0_m_w_out, l0_norm_ffn, l0_f_w_up, l0_f_conv_w, l0_f_conv_b, l0_f_w_down, l1_norm_mix, l1_a_w_qkv, l1_a_lq1, l1_a_lk1, l1_a_lq2, l1_a_lk2, l1_a_subln, l1_a_w_o, l1_norm_ffn, l1_f_w_up, l1_f_conv_w, l1_f_conv_b, l1_f_w_down, l2_norm_mix, l2_m_w_in, l2_m_conv_w, l2_m_conv_b, l2_m_dt_bias, l2_m_a_log, l2_m_d, l2_m_norm, l2_m_w_out, l2_norm_ffn, l2_f_w_up, l2_f_conv_w, l2_f_conv_b, l2_f_w_down, l3_norm_mix, l3_a_w_qkv, l3_a_lq1, l3_a_lk1, l3_a_lq2, l3_a_lk2, l3_a_subln, l3_a_w_o, l3_norm_ffn, l3_f_w_up, l3_f_conv_w, l3_f_conv_b, l3_f_w_down, final_norm):
    norm_mix = [l0_norm_mix, l1_norm_mix, l2_norm_mix, l3_norm_mix]
    norm_ffn = [l0_norm_ffn, l1_norm_ffn, l2_norm_ffn, l3_norm_ffn]
    mixer_params = [
        (l0_m_w_in, l0_m_conv_w, l0_m_conv_b, l0_m_dt_bias, l0_m_a_log, l0_m_d, l0_m_norm, l0_m_w_out),
        (l1_a_w_qkv, l1_a_lq1, l1_a_lk1, l1_a_lq2, l1_a_lk2, l1_a_subln, l1_a_w_o),
        (l2_m_w_in, l2_m_conv_w, l2_m_conv_b, l2_m_dt_bias, l2_m_a_log, l2_m_d, l2_m_norm, l2_m_w_out),
        (l3_a_w_qkv, l3_a_lq1, l3_a_lk1, l3_a_lq2, l3_a_lk2, l3_a_subln, l3_a_w_o),
    ]
    ffn_params = [
        (l0_f_w_up, l0_f_conv_w, l0_f_conv_b, l0_f_w_down),
        (l1_f_w_up, l1_f_conv_w, l1_f_conv_b, l1_f_w_down),
        (l2_f_w_up, l2_f_conv_w, l2_f_conv_b, l2_f_w_down),
        (l3_f_w_up, l3_f_conv_w, l3_f_conv_b, l3_f_w_down),
    ]
    batch, seq_len, d_model = x.shape
    xf = x.reshape(batch * seq_len, d_model)
    cos, sin = rope_tables(positions)
    for i in range(len(norm_mix)):
        u = rmsnorm(xf, norm_mix[i], BF16)
        if i % N_MIXERS == 0:
            xf = mamba_layer(xf, u, *mixer_params[i], batch, seq_len)
        else:
            lambda_init = 0.8 - 0.6 * math.exp(-0.3 * i)
            xf = attention_layer(xf, u, cos, sin, *mixer_params[i], lambda_init, batch, seq_len)
        u = rmsnorm(xf, norm_ffn[i], BF16)
        xf = ffn_layer(xf, u, *ffn_params[i], seq_len)
    return rmsnorm(xf, final_norm, F32).reshape(batch, seq_len, d_model)
```

```python
import functools
import math

import jax
import jax.numpy as jnp
from jax import lax
from jax.experimental import pallas as pl
from jax.experimental.pallas import tpu as pltpu

F32 = jnp.float32
BF16 = jnp.bfloat16

EPS = 1e-5
N_MIXERS = 2

SSM_HEADDIM = 64
SSM_GROUPS = 8
SSM_STATE = 128
SSM_CONV = 4
CHUNK = 128

DIFF_HEAD_DIM = 128
ATTN_SCALE = DIFF_HEAD_DIM ** -0.5
Q_PRESCALE = ATTN_SCALE * math.log2(math.e)
ROPE_THETA = 10000.0

FFN_CONV = 3

V7X_VMEM_BYTES = 64 * 1024 * 1024
VMEM_LIMIT_BYTES = V7X_VMEM_BYTES - 8 * 1024 * 1024
SUBLANES = 8
BF16_ROWS = 16
LANES = 128
EPILOGUE_ROWS = 64


def _params(*sem, flags=None):
    return pltpu.CompilerParams(dimension_semantics=sem, vmem_limit_bytes=VMEM_LIMIT_BYTES, flags=flags)


def _pick(dim, pref):
    if dim <= pref:
        return dim
    t = pref
    while dim % t:
        t //= 2
    return t


def _silu(x):
    return x / (1.0 + jnp.exp(-x))


def _rmsnorm_kernel(x_ref, w_ref, o_ref):
    x = x_ref[...]
    y = x * lax.rsqrt(jnp.mean(x * x, axis=-1, keepdims=True) + EPS)
    o_ref[...] = (y * w_ref[...]).astype(o_ref.dtype)


def rmsnorm(x, w, out_dtype):
    m, d = x.shape
    tm = _pick(m, 256)
    return pl.pallas_call(
        _rmsnorm_kernel,
        grid=(m // tm,),
        in_specs=[pl.BlockSpec((tm, d), lambda i: (i, 0)),
                  pl.BlockSpec((1, d), lambda i: (0, 0))],
        out_specs=pl.BlockSpec((tm, d), lambda i: (i, 0)),
        out_shape=jax.ShapeDtypeStruct((m, d), out_dtype),
        compiler_params=_params("arbitrary"),
        name="rmsnorm",
    )(x, w.reshape(1, d))


def _mm_kernel(a_ref, w_ref, o_ref):
    o_ref[...] = jnp.dot(a_ref[...], w_ref[...], preferred_element_type=F32).astype(o_ref.dtype)


def _mm_res_kernel(a_ref, w_ref, r_ref, o_ref):
    o_ref[...] = r_ref[...] + jnp.dot(a_ref[...], w_ref[...], preferred_element_type=F32)


def matmul(a, w, *, col0=0, n=None, tm, tn, out_dtype=F32, residual=None, single_buffer_a=False):
    m, k = a.shape
    n = w.shape[1] - col0 if n is None else n
    tm = _pick(m, tm)
    tn = _pick(n, tn)
    assert col0 % tn == 0
    jb = col0 // tn
    a_kwargs = dict(pipeline_mode=pl.Buffered(1)) if single_buffer_a else {}
    in_specs = [pl.BlockSpec((tm, k), lambda i, j: (i, 0), **a_kwargs),
                pl.BlockSpec((k, tn), lambda i, j: (0, j + jb))]
    args = [a, w]
    body = _mm_kernel
    if residual is not None:
        in_specs.append(pl.BlockSpec((tm, tn), lambda i, j: (i, j)))
        args.append(residual)
        body = _mm_res_kernel
    return pl.pallas_call(
        body,
        grid=(m // tm, n // tn),
        in_specs=in_specs,
        out_specs=pl.BlockSpec((tm, tn), lambda i, j: (i, j)),
        out_shape=jax.ShapeDtypeStruct((m, n), out_dtype),
        compiler_params=_params("arbitrary", "arbitrary"),
        name="matmul",
    )(*args)


def _shift_rows(ext, s, halo_rows):
    if s == 0:
        return ext[halo_rows:]
    return pltpu.roll(ext, s, axis=0)[halo_rows:]


def _causal_conv_ext(ext, halo_rows, w, b):
    k_width = w.shape[0]
    y = b + w[0:1] * _shift_rows(ext, k_width - 1, halo_rows)
    for k in range(1, k_width):
        y = y + w[k:k + 1] * _shift_rows(ext, k_width - 1 - k, halo_rows)
    return y


def _causal_conv_ref(ext_ref, halo_rows, row0, rows, w, b):
    k_width = w.shape[0]
    base = halo_rows + row0 - (k_width - 1)
    y = b + w[0:1] * ext_ref[pl.ds(base, rows), :]
    for k in range(1, k_width):
        y = y + w[k:k + 1] * ext_ref[pl.ds(base + k, rows), :]
    return y


def _ffn_up_kernel(a_ref, ah_ref, wg_ref, wu_ref, cwg_ref, cwu_ref, cbg_ref, cbu_ref,
                   o_ref, aext_ref, hg0_ref, hu0_ref, hg1_ref, hu1_ref, *, tiles_per_seq, nb, n_tiles):
    s = pl.program_id(0)
    t = jnp.minimum(s, n_tiles - 1)
    i = t // nb
    j = t % nb
    slot = s % 2
    hg = (hg0_ref, hg1_ref)
    hu = (hu0_ref, hu1_ref)

    @pl.when(s == 0)
    def _():
        hg1_ref[...] = jnp.zeros(hg1_ref.shape, F32)
        hu1_ref[...] = jnp.zeros(hu1_ref.shape, F32)

    @pl.when((j == 0) & (s < n_tiles))
    def _():
        seq_start = (i % tiles_per_seq) == 0
        halo = ah_ref[...]
        aext_ref[0:BF16_ROWS, :] = jnp.where(seq_start, jnp.zeros_like(halo), halo)
        aext_ref[BF16_ROWS:, :] = a_ref[...]

    def step(cur, prev):
        cwg, cwu, cbg, cbu = cwg_ref[...], cwu_ref[...], cbg_ref[...], cbu_ref[...]
        rc = min(EPILOGUE_ROWS, o_ref.shape[0])
        for r0 in range(0, o_ref.shape[0], rc):
            gate = _causal_conv_ref(hg[prev], BF16_ROWS, r0, rc, cwg, cbg)
            up = _causal_conv_ref(hu[prev], BF16_ROWS, r0, rc, cwu, cbu)
            o_ref[r0:r0 + rc, :] = (_silu(gate) * up).astype(o_ref.dtype)
        a_ext = aext_ref[...]
        hg[cur][...] = jnp.dot(a_ext, wg_ref[...], preferred_element_type=F32)
        hu[cur][...] = jnp.dot(a_ext, wu_ref[...], preferred_element_type=F32)

    @pl.when(slot == 0)
    def _():
        step(0, 1)

    @pl.when(slot == 1)
    def _():
        step(1, 0)


def ffn_up(u, w_up, conv_w, conv_b, seq_len, *, tm=1024, tn=256):
    m, k = u.shape
    d_ff = w_up.shape[1] // 2
    tm = _pick(seq_len, tm)
    tn = _pick(d_ff, tn)
    nb = d_ff // tn
    hb = tm // BF16_ROWS
    n_tiles = (m // tm) * nb

    def cur(s):
        t = jnp.minimum(s, n_tiles - 1)
        return t // nb, t % nb

    def prev(s):
        t = jnp.maximum(s - 1, 0)
        return t // nb, t % nb

    kern = functools.partial(_ffn_up_kernel, tiles_per_seq=seq_len // tm, nb=nb, n_tiles=n_tiles)
    return pl.pallas_call(
        kern,
        grid=(n_tiles + 1,),
        in_specs=[
            pl.BlockSpec((tm, k), lambda s: (cur(s)[0], 0), pipeline_mode=pl.Buffered(1)),
            pl.BlockSpec((BF16_ROWS, k), lambda s: (jnp.maximum(cur(s)[0] * hb - 1, 0), 0)),
            pl.BlockSpec((k, tn), lambda s: (0, cur(s)[1])),
            pl.BlockSpec((k, tn), lambda s: (0, cur(s)[1] + nb)),
            pl.BlockSpec((FFN_CONV, tn), lambda s: (0, prev(s)[1])),
            pl.BlockSpec((FFN_CONV, tn), lambda s: (0, prev(s)[1] + nb)),
            pl.BlockSpec((1, tn), lambda s: (0, prev(s)[1])),
            pl.BlockSpec((1, tn), lambda s: (0, prev(s)[1] + nb)),
        ],
        out_specs=pl.BlockSpec((tm, tn), lambda s: prev(s)),
        out_shape=jax.ShapeDtypeStruct((m, d_ff), BF16),
        scratch_shapes=[pltpu.VMEM((tm + BF16_ROWS, k), BF16)]
        + [pltpu.VMEM((tm + BF16_ROWS, tn), F32)] * 4,
        compiler_params=_params("arbitrary"),
        name="ffn_up",
    )(u, u, w_up, w_up, conv_w, conv_w, conv_b.reshape(1, -1), conv_b.reshape(1, -1))


def _rope_table_kernel(pos_ref, inv_ref, cos_ref, sin_ref):
    freqs = pos_ref[...].astype(F32) * inv_ref[...]
    lane = lax.broadcasted_iota(jnp.int32, freqs.shape, 1)
    sign = jnp.where(lane < DIFF_HEAD_DIM // 2, -1.0, 1.0)
    cos_ref[...] = jnp.cos(freqs)
    sin_ref[...] = jnp.sin(freqs) * sign


def rope_tables(positions):
    m = positions.size
    inv = 1.0 / (ROPE_THETA ** (jnp.arange(0, DIFF_HEAD_DIM, 2, dtype=F32) / DIFF_HEAD_DIM))
    inv_full = jnp.concatenate([inv, inv]).reshape(1, DIFF_HEAD_DIM)
    tm = _pick(m, 1024)
    shape = jax.ShapeDtypeStruct((m, DIFF_HEAD_DIM), F32)
    return pl.pallas_call(
        _rope_table_kernel,
        grid=(m // tm,),
        in_specs=[pl.BlockSpec((tm, 1), lambda i: (i, 0)),
                  pl.BlockSpec((1, DIFF_HEAD_DIM), lambda i: (0, 0))],
        out_specs=[pl.BlockSpec((tm, DIFF_HEAD_DIM), lambda i: (i, 0))] * 2,
        out_shape=[shape, shape],
        compiler_params=_params("arbitrary"),
        name="rope_tables",
    )(positions.reshape(m, 1), inv_full)


def _qkv_rope_kernel(a_ref, w_ref, cos_ref, sin_ref, o_ref, *, rope_tiles):
    j = pl.program_id(1)
    acc = jnp.dot(a_ref[...], w_ref[...], preferred_element_type=F32)

    @pl.when(j < rope_tiles)
    def _():
        scale = jnp.where(j < rope_tiles // 2, Q_PRESCALE, 1.0)
        cos = cos_ref[...] * scale
        sin = sin_ref[...] * scale
        for c in range(acc.shape[1] // DIFF_HEAD_DIM):
            sl = slice(c * DIFF_HEAD_DIM, (c + 1) * DIFF_HEAD_DIM)
            t = acc[:, sl]
            rot = pltpu.roll(t, DIFF_HEAD_DIM // 2, axis=1)
            o_ref[:, sl] = (t * cos + rot * sin).astype(o_ref.dtype)

    @pl.when(j >= rope_tiles)
    def _():
        o_ref[...] = acc.astype(o_ref.dtype)


def qkv_rope(u, w_qkv, cos, sin, *, tm=1024, tn=1024):
    m, k = u.shape
    n = w_qkv.shape[1]
    d_model = n // 3
    tm = _pick(m, tm)
    tn = _pick(d_model, tn)
    kern = functools.partial(_qkv_rope_kernel, rope_tiles=2 * d_model // tn)
    return pl.pallas_call(
        kern,
        grid=(m // tm, n // tn),
        in_specs=[pl.BlockSpec((tm, k), lambda i, j: (i, 0)),
                  pl.BlockSpec((k, tn), lambda i, j: (0, j)),
                  pl.BlockSpec((tm, DIFF_HEAD_DIM), lambda i, j: (i, 0)),
                  pl.BlockSpec((tm, DIFF_HEAD_DIM), lambda i, j: (i, 0))],
        out_specs=pl.BlockSpec((tm, tn), lambda i, j: (i, j)),
        out_shape=jax.ShapeDtypeStruct((m, n), BF16),
        compiler_params=_params("arbitrary", "arbitrary"),
        name="qkv_rope",
    )(u, w_qkv, cos, sin)


def _diff_attn_kernel(q_ref, k_ref, v_ref, lq1_ref, lk1_ref, lq2_ref, lk2_ref, sw_ref, o_ref,
                      m_ref, l_ref, acc_ref, *, lambda_init, tq):
    qi = pl.program_id(2)
    d = DIFF_HEAD_DIM
    q = q_ref[...]

    m_ref[...] = jnp.full(m_ref.shape, -jnp.inf, F32)
    l_ref[...] = jnp.zeros(l_ref.shape, F32)
    acc_ref[...] = jnp.zeros(acc_ref.shape, F32)

    def kv_block(kb, masked):
        start = pl.multiple_of(kb * tq, tq)
        k = k_ref[pl.ds(start, tq), :]
        v = v_ref[pl.ds(start, tq), :]
        probs = []
        alphas = []
        for c in range(2):
            s = lax.dot_general(q[:, c * d:(c + 1) * d], k[:, c * d:(c + 1) * d],
                                (((1,), (1,)), ((), ())), preferred_element_type=F32)
            if masked:
                row = lax.broadcasted_iota(jnp.int32, s.shape, 0)
                col = lax.broadcasted_iota(jnp.int32, s.shape, 1)
                s = jnp.where(col <= row, s, -jnp.inf)
            chunks = [s[:, t * LANES:(t + 1) * LANES] for t in range(tq // LANES)]
            cmax = functools.reduce(jnp.maximum, chunks)
            m_old = m_ref[c]
            m_new = jnp.maximum(m_old, jnp.max(cmax, axis=-1, keepdims=True))
            alpha = jnp.exp2(m_old - m_new)
            ps = [jnp.exp2(ch - m_new) for ch in chunks]
            l_ref[c] = alpha * l_ref[c] + functools.reduce(jnp.add, ps)
            m_ref[c] = m_new
            probs.append(jnp.concatenate(ps, axis=1).astype(v.dtype))
            alphas.append(alpha)
        pv = jnp.dot(jnp.concatenate(probs, axis=0), v, preferred_element_type=F32)
        for c in range(2):
            a2 = jnp.concatenate([alphas[c], alphas[c]], axis=1)
            acc_ref[c] = a2 * acc_ref[c] + pv[c * tq:(c + 1) * tq]

    def body(kb, carry):
        kv_block(kb, False)
        return carry

    lax.fori_loop(0, qi, body, 0)
    kv_block(qi, True)

    lam = (jnp.exp(jnp.sum(lq1_ref[...] * lk1_ref[...], axis=-1, keepdims=True))
           - jnp.exp(jnp.sum(lq2_ref[...] * lk2_ref[...], axis=-1, keepdims=True)) + lambda_init)
    l0 = jnp.sum(l_ref[0], axis=-1, keepdims=True)
    l1 = jnp.sum(l_ref[1], axis=-1, keepdims=True)
    o = acc_ref[0] / l0 - lam * (acc_ref[1] / l1)
    o = o * lax.rsqrt(jnp.mean(o * o, axis=-1, keepdims=True) + EPS)
    o_ref[...] = ((o * sw_ref[...]) * (1.0 - lambda_init)).astype(o_ref.dtype)


def diff_attention(qkv, lq1, lk1, lq2, lk2, subln_w, lambda_init, batch, seq_len, *, tq=512):
    m, n = qkv.shape
    d_model = n // 3
    hw = 2 * DIFF_HEAD_DIM
    heads = d_model // hw
    tq = _pick(seq_len, tq)
    nq = seq_len // tq
    kern = functools.partial(_diff_attn_kernel, lambda_init=lambda_init, tq=tq)
    vec = lambda a: a.reshape(1, -1)
    lam_spec = pl.BlockSpec((1, DIFF_HEAD_DIM), lambda b, h, qi: (0, 0))
    return pl.pallas_call(
        kern,
        grid=(batch, heads, nq),
        in_specs=[
            pl.BlockSpec((tq, hw), lambda b, h, qi: (b * nq + qi, h)),
            pl.BlockSpec((seq_len, hw), lambda b, h, qi: (b, heads + h)),
            pl.BlockSpec((seq_len, hw), lambda b, h, qi: (b, 2 * heads + h)),
            lam_spec, lam_spec, lam_spec, lam_spec,
            pl.BlockSpec((1, hw), lambda b, h, qi: (0, 0)),
        ],
        out_specs=pl.BlockSpec((tq, hw), lambda b, h, qi: (b * nq + qi, h)),
        out_shape=jax.ShapeDtypeStruct((m, d_model), BF16),
        scratch_shapes=[pltpu.VMEM((2, tq, LANES), F32), pltpu.VMEM((2, tq, LANES), F32),
                        pltpu.VMEM((2, tq, hw), F32)],
        compiler_params=_params("arbitrary", "arbitrary", "arbitrary"),
        name="diff_attention",
    )(qkv, qkv, qkv, vec(lq1), vec(lk1), vec(lq2), vec(lk2), vec(subln_w))


def _dot_exact(a, b):
    return jnp.dot(a, b, preferred_element_type=F32, precision=lax.Precision.HIGHEST)


def _ssd_kernel(z_ref, x_ref, xh_ref, b_ref, bh_ref, c_ref, ch_ref, dt_ref,
                cwx_ref, cwb_ref, cwc_ref, cbx_ref, cbb_ref, cbc_ref,
                dtb_ref, alog_ref, dexp_ref, nw_ref, o_ref, state_ref, *, heads_per_group):
    g = pl.program_id(1)
    c = pl.program_id(2)
    t = CHUNK
    hpg = heads_per_group
    p = SSM_HEADDIM
    gw = hpg * p
    nh = dt_ref.shape[1]

    @pl.when(c == 0)
    def _():
        state_ref[...] = jnp.zeros(state_ref.shape, F32)

    seq_start = c == 0

    def conv_silu(cur_ref, halo_ref, w_ref, bias_ref):
        halo = halo_ref[...]
        halo = jnp.where(seq_start, jnp.zeros_like(halo), halo)
        ext = jnp.concatenate([halo, cur_ref[...]], axis=0)
        return _silu(_causal_conv_ext(ext, SUBLANES, w_ref[...], bias_ref[...]))

    xc = conv_silu(x_ref, xh_ref, cwx_ref, cbx_ref)
    bc = conv_silu(b_ref, bh_ref, cwb_ref, cbb_ref)
    cc = conv_silu(c_ref, ch_ref, cwc_ref, cbc_ref)

    dtr = dt_ref[...] + dtb_ref[...]
    dtv = jnp.maximum(dtr, 0.0) + jnp.log1p(jnp.exp(-jnp.abs(dtr)))
    a_all = dtv * (-jnp.exp(alog_ref[...]))

    row_t = lax.broadcasted_iota(jnp.int32, (t, t), 0)
    col_t = lax.broadcasted_iota(jnp.int32, (t, t), 1)
    causal = col_t <= row_t
    a_cs_all = _dot_exact(jnp.where(causal, 1.0, 0.0), a_all)
    a_last = a_cs_all[t - 1:t, :]

    row_h = lax.broadcasted_iota(jnp.int32, (nh, nh), 0)
    col_h = lax.broadcasted_iota(jnp.int32, (nh, nh), 1)
    sel = jnp.where((row_h == g * hpg + col_h) & (col_h < hpg), 1.0, 0.0)
    a_cs = _dot_exact(a_cs_all, sel)
    a_cs_t = a_cs.T

    row_e = lax.broadcasted_iota(jnp.int32, (nh, gw), 0) - g * hpg
    col_e = lax.broadcasted_iota(jnp.int32, (nh, gw), 1)
    expand = jnp.where((col_e >= row_e * p) & (col_e < (row_e + 1) * p), 1.0, 0.0).astype(BF16)
    stacked = jnp.concatenate([dtv, jnp.exp(a_cs_all), jnp.exp(a_last - a_cs_all)], axis=0)
    hi = stacked.astype(BF16)
    lo = (stacked - hi.astype(F32)).astype(BF16)
    expanded = jnp.dot(jnp.concatenate([hi, lo], axis=1), jnp.concatenate([expand, expand], axis=0),
                       preferred_element_type=F32)
    dt_e = expanded[0:t]
    exp_acs_e = expanded[t:2 * t]
    dec_e = expanded[2 * t:3 * t]

    xdt = xc * dt_e
    xdt_b = xdt.astype(BF16)
    bb = bc.astype(BF16)
    cb = cc.astype(BF16)
    cbm = lax.dot_general(cb, bb, (((1,), (1,)), ((), ())), preferred_element_type=F32)

    state = state_ref[...]
    y = jnp.dot(cb, state.astype(BF16), preferred_element_type=F32) * exp_acs_e

    lane = lax.broadcasted_iota(jnp.int32, (t, 2 * p), 1)
    pieces = []
    for pair in range(hpg // 2):
        x_pair = xdt_b[:, pair * 2 * p:(pair + 1) * 2 * p]
        outs = []
        for hh in (2 * pair, 2 * pair + 1):
            seg = a_cs[:, hh:hh + 1] - a_cs_t[hh:hh + 1, :]
            decay = jnp.exp(jnp.where(causal, seg, -jnp.inf))
            outs.append(jnp.dot((cbm * decay).astype(BF16), x_pair, preferred_element_type=F32))
        pieces.append(jnp.where(lane < p, outs[0], outs[1]))
    y = y + jnp.concatenate(pieces, axis=1) + dexp_ref[...] * xc

    new_contrib = jnp.dot(bc.T.astype(BF16), (xdt * dec_e).astype(BF16), preferred_element_type=F32)
    state_ref[...] = state * exp_acs_e[t - 1:t, :] + new_contrib

    gn = y * _silu(z_ref[...])
    gn = gn * lax.rsqrt(jnp.mean(gn * gn, axis=-1, keepdims=True) + EPS)
    o_ref[...] = (gn * nw_ref[...]).astype(o_ref.dtype)


def ssd_mixer(zx, bcm, dt, conv_w, conv_b, dt_bias, a_log, d_skip, norm_w, batch, seq_len):
    m = zx.shape[0]
    d_inner = zx.shape[1] // 2
    nh = dt.shape[1]
    hpg = nh // SSM_GROUPS
    gw = d_inner // SSM_GROUPS
    n = SSM_STATE
    t = CHUNK
    nc = seq_len // t
    hb = t // SUBLANES
    ng = SSM_GROUPS
    xb0 = d_inner // gw
    cwb0 = d_inner // n

    def row(b, g, c):
        return b * nc + c

    def halo(b, g, c):
        return jnp.maximum((b * nc + c) * hb - 1, 0)

    d_exp = jnp.repeat(d_skip.astype(F32), SSM_HEADDIM).reshape(1, d_inner)
    conv_b2 = conv_b.reshape(1, -1)
    kern = functools.partial(_ssd_kernel, heads_per_group=hpg)
    return pl.pallas_call(
        kern,
        grid=(batch, ng, nc),
        in_specs=[
            pl.BlockSpec((t, gw), lambda b, g, c: (row(b, g, c), g)),
            pl.BlockSpec((t, gw), lambda b, g, c: (row(b, g, c), xb0 + g)),
            pl.BlockSpec((SUBLANES, gw), lambda b, g, c: (halo(b, g, c), xb0 + g)),
            pl.BlockSpec((t, n), lambda b, g, c: (row(b, g, c), g)),
            pl.BlockSpec((SUBLANES, n), lambda b, g, c: (halo(b, g, c), g)),
            pl.BlockSpec((t, n), lambda b, g, c: (row(b, g, c), ng + g)),
            pl.BlockSpec((SUBLANES, n), lambda b, g, c: (halo(b, g, c), ng + g)),
            pl.BlockSpec((t, nh), lambda b, g, c: (row(b, g, c), 0)),
            pl.BlockSpec((SSM_CONV, gw), lambda b, g, c: (0, g)),
            pl.BlockSpec((SSM_CONV, n), lambda b, g, c: (0, cwb0 + g)),
            pl.BlockSpec((SSM_CONV, n), lambda b, g, c: (0, cwb0 + ng + g)),
            pl.BlockSpec((1, gw), lambda b, g, c: (0, g)),
            pl.BlockSpec((1, n), lambda b, g, c: (0, cwb0 + g)),
            pl.BlockSpec((1, n), lambda b, g, c: (0, cwb0 + ng + g)),
            pl.BlockSpec((1, nh), lambda b, g, c: (0, 0)),
            pl.BlockSpec((1, nh), lambda b, g, c: (0, 0)),
            pl.BlockSpec((1, gw), lambda b, g, c: (0, g)),
            pl.BlockSpec((1, gw), lambda b, g, c: (0, g)),
        ],
        out_specs=pl.BlockSpec((t, gw), lambda b, g, c: (row(b, g, c), g)),
        out_shape=jax.ShapeDtypeStruct((m, d_inner), BF16),
        scratch_shapes=[pltpu.VMEM((n, gw), F32)],
        compiler_params=_params("arbitrary", "arbitrary", "arbitrary"),
        name="ssd_mixer",
    )(zx, zx, zx, bcm, bcm, bcm, bcm, dt, conv_w, conv_w, conv_w, conv_b2, conv_b2, conv_b2,
      dt_bias.reshape(1, nh), a_log.reshape(1, nh), d_exp, norm_w.reshape(1, d_inner))


def mamba_layer(xf, u, w_in, conv_w, conv_b, dt_bias, a_log, d_skip, norm_w, w_out, batch, seq_len):
    nh = dt_bias.shape[0]
    d_inner = norm_w.shape[0]
    bc_cols = 2 * SSM_GROUPS * SSM_STATE
    w_in = w_in.astype(BF16)
    zx = matmul(u, w_in, col0=0, n=2 * d_inner, tm=1024, tn=1024)
    bcm = matmul(u, w_in, col0=2 * d_inner, n=bc_cols, tm=1024, tn=1024)
    dt = matmul(u, w_in, col0=2 * d_inner + bc_cols, n=nh, tm=1024, tn=nh)
    yn = ssd_mixer(zx, bcm, dt, conv_w, conv_b, dt_bias, a_log, d_skip, norm_w, batch, seq_len)
    return matmul(yn, w_out.astype(BF16), tm=1024, tn=512, residual=xf, single_buffer_a=True)


def attention_layer(xf, u, cos, sin, w_qkv, lq1, lk1, lq2, lk2, subln_w, w_o, lambda_init, batch, seq_len):
    qkv = qkv_rope(u, w_qkv.astype(BF16), cos, sin)
    o = diff_attention(qkv, lq1, lk1, lq2, lk2, subln_w, lambda_init, batch, seq_len)
    return matmul(o, w_o.astype(BF16), tm=1024, tn=1024, residual=xf)


def ffn_layer(xf, u, w_up, conv_w, conv_b, w_down, seq_len):
    g = ffn_up(u, w_up.astype(BF16), conv_w, conv_b, seq_len)
    return matmul(g, w_down.astype(BF16), tm=1024, tn=256, residual=xf, single_buffer_a=True)


def kernel(x, positions, l0_norm_mix, l0_m_w_in, l0_m_conv_w, l0_m_conv_b, l0_m_dt_bias, l0_m_a_log, l0_m_d, l0_m_norm, l0_m_w_out, l0_norm_ffn, l0_f_w_up, l0_f_conv_w, l0_f_conv_b, l0_f_w_down, l1_norm_mix, l1_a_w_qkv, l1_a_lq1, l1_a_lk1, l1_a_lq2, l1_a_lk2, l1_a_subln, l1_a_w_o, l1_norm_ffn, l1_f_w_up, l1_f_conv_w, l1_f_conv_b, l1_f_w_down, l2_norm_mix, l2_m_w_in, l2_m_conv_w, l2_m_conv_b, l2_m_dt_bias, l2_m_a_log, l2_m_d, l2_m_norm, l2_m_w_out, l2_norm_ffn, l2_f_w_up, l2_f_conv_w, l2_f_conv_b, l2_f_w_down, l3_norm_mix, l3_a_w_qkv, l3_a_lq1, l3_a_lk1, l3_a_lq2, l3_a_lk2, l3_a_subln, l3_a_w_o, l3_norm_ffn, l3_f_w_up, l3_f_conv_w, l3_f_conv_b, l3_f_w_down, final_norm):
    norm_mix = [l0_norm_mix, l1_norm_mix, l2_norm_mix, l3_norm_mix]
    norm_ffn = [l0_norm_ffn, l1_norm_ffn, l2_norm_ffn, l3_norm_ffn]
    mixer_params = [
        (l0_m_w_in, l0_m_conv_w, l0_m_conv_b, l0_m_dt_bias, l0_m_a_log, l0_m_d, l0_m_norm, l0_m_w_out),
        (l1_a_w_qkv, l1_a_lq1, l1_a_lk1, l1_a_lq2, l1_a_lk2, l1_a_subln, l1_a_w_o),
        (l2_m_w_in, l2_m_conv_w, l2_m_conv_b, l2_m_dt_bias, l2_m_a_log, l2_m_d, l2_m_norm, l2_m_w_out),
        (l3_a_w_qkv, l3_a_lq1, l3_a_lk1, l3_a_lq2, l3_a_lk2, l3_a_subln, l3_a_w_o),
    ]
    ffn_params = [
        (l0_f_w_up, l0_f_conv_w, l0_f_conv_b, l0_f_w_down),
        (l1_f_w_up, l1_f_conv_w, l1_f_conv_b, l1_f_w_down),
        (l2_f_w_up, l2_f_conv_w, l2_f_conv_b, l2_f_w_down),
        (l3_f_w_up, l3_f_conv_w, l3_f_conv_b, l3_f_w_down),
    ]
    batch, seq_len, d_model = x.shape
    xf = x.reshape(batch * seq_len, d_model)
    cos, sin = rope_tables(positions)
    for i in range(len(norm_mix)):
        u = rmsnorm(xf, norm_mix[i], BF16)
        if i % N_MIXERS == 0:
            xf = mamba_layer(xf, u, *mixer_params[i], batch, seq_len)
        else:
            lambda_init = 0.8 - 0.6 * math.exp(-0.3 * i)
            xf = attention_layer(xf, u, cos, sin, *mixer_params[i], lambda_init, batch, seq_len)
        u = rmsnorm(xf, norm_ffn[i], BF16)
        xf = ffn_layer(xf, u, *ffn_params[i], seq_len)
    return rmsnorm(xf, final_norm, F32).reshape(batch, seq_len, d_model)
```

```python
import functools
import math

import jax
import jax.numpy as jnp
from jax import lax
from jax.experimental import pallas as pl
from jax.experimental.pallas import tpu as pltpu

F32 = jnp.float32
BF16 = jnp.bfloat16

EPS = 1e-5
N_MIXERS = 2

SSM_HEADDIM = 64
SSM_GROUPS = 8
SSM_STATE = 128
SSM_CONV = 4
CHUNK = 128

DIFF_HEAD_DIM = 128
ATTN_SCALE = DIFF_HEAD_DIM ** -0.5
Q_PRESCALE = ATTN_SCALE * math.log2(math.e)
ROPE_THETA = 10000.0

FFN_CONV = 3

V7X_VMEM_BYTES = 64 * 1024 * 1024
VMEM_LIMIT_BYTES = V7X_VMEM_BYTES - 8 * 1024 * 1024
SUBLANES = 8
BF16_ROWS = 16
LANES = 128
MXU_COLS = 256
EPILOGUE_VREGS = 32


def _params(*sem):
    return pltpu.CompilerParams(dimension_semantics=sem, vmem_limit_bytes=VMEM_LIMIT_BYTES)


def _pick(dim, pref):
    if dim <= pref:
        return dim
    t = pref
    while dim % t:
        t //= 2
    return t


def _silu(x):
    return x / (1.0 + jnp.exp(-x))


def _col_tiles(w, tn, dtype=None):
    k, n = w.shape
    return w.reshape(k, n // tn, tn).transpose(1, 0, 2).astype(BF16 if dtype is None else dtype)


def _rmsnorm_kernel(x_ref, w_ref, o_ref):
    x = x_ref[...]
    y = x * lax.rsqrt(jnp.mean(x * x, axis=-1, keepdims=True) + EPS)
    o_ref[...] = (y * w_ref[...]).astype(o_ref.dtype)


def rmsnorm(x, w, out_dtype):
    m, d = x.shape
    tm = _pick(m, 256)
    return pl.pallas_call(
        _rmsnorm_kernel,
        grid=(m // tm,),
        in_specs=[pl.BlockSpec((tm, d), lambda i: (i, 0)),
                  pl.BlockSpec((1, d), lambda i: (0, 0))],
        out_specs=pl.BlockSpec((tm, d), lambda i: (i, 0)),
        out_shape=jax.ShapeDtypeStruct((m, d), out_dtype),
        compiler_params=_params("arbitrary"),
        name="rmsnorm",
    )(x, w.reshape(1, d))


def _mm_kernel(a_ref, w_ref, o_ref):
    o_ref[...] = jnp.dot(a_ref[...], w_ref[...], preferred_element_type=F32).astype(o_ref.dtype)


def _mm_res_kernel(a_ref, w_ref, r_ref, o_ref):
    o_ref[...] = r_ref[...] + jnp.dot(a_ref[...], w_ref[...], preferred_element_type=F32)


def matmul(a, w_tiles, *, tm, out_dtype=F32, residual=None, single_buffer_a=False):
    m, k = a.shape
    nb, _, tn = w_tiles.shape
    tm = _pick(m, tm)
    a_kwargs = dict(pipeline_mode=pl.Buffered(1)) if single_buffer_a else {}
    in_specs = [pl.BlockSpec((tm, k), lambda i, j: (i, 0), **a_kwargs),
                pl.BlockSpec((None, k, tn), lambda i, j: (j, 0, 0))]
    args = [a, w_tiles]
    body = _mm_kernel
    if residual is not None:
        in_specs.append(pl.BlockSpec((tm, tn), lambda i, j: (i, j)))
        args.append(residual)
        body = _mm_res_kernel
    return pl.pallas_call(
        body,
        grid=(m // tm, nb),
        in_specs=in_specs,
        out_specs=pl.BlockSpec((tm, tn), lambda i, j: (i, j)),
        out_shape=jax.ShapeDtypeStruct((m, nb * tn), out_dtype),
        compiler_params=_params("arbitrary", "arbitrary"),
        name="matmul",
    )(*args)


def _shift_rows(ext, s, halo_rows):
    if s == 0:
        return ext[halo_rows:]
    return pltpu.roll(ext, s, axis=0)[halo_rows:]


def _causal_conv_ext(ext, halo_rows, w, b):
    k_width = w.shape[0]
    y = b + w[0:1] * _shift_rows(ext, k_width - 1, halo_rows)
    for k in range(1, k_width):
        y = y + w[k:k + 1] * _shift_rows(ext, k_width - 1 - k, halo_rows)
    return y


def _causal_conv_ref(ext_ref, halo_rows, row0, rows, w, b):
    k_width = w.shape[0]
    base = halo_rows + row0 - (k_width - 1)
    y = b + w[0:1] * ext_ref[pl.ds(base, rows), :]
    for k in range(1, k_width):
        y = y + w[k:k + 1] * ext_ref[pl.ds(base + k, rows), :]
    return y


def _row_chunks(rows, cols):
    rc = max(BF16_ROWS, EPILOGUE_VREGS * SUBLANES * LANES // cols)
    rc = min(rc, rows)
    return [(r0, rc) for r0 in range(0, rows, rc)]


def _epilogue_conv_gate(h_ref, halo_rows, aux, o_ref, jp):
    cw, cb = aux[0][...], aux[1][...]
    tn = o_ref.shape[1]

    def chunk(r0, rc):
        y = _causal_conv_ref(h_ref, halo_rows, r0, rc, cw, cb)
        o_ref[r0:r0 + rc, :] = (_silu(y[:, :tn]) * y[:, tn:]).astype(o_ref.dtype)

    return [functools.partial(chunk, r0, rc) for r0, rc in _row_chunks(o_ref.shape[0], 2 * tn)]


def _epilogue_rope(h_ref, halo_rows, aux, o_ref, jp, *, rope_tiles):
    cos_ref, sin_ref = aux
    d = DIFF_HEAD_DIM
    is_rope = jp < rope_tiles
    scale = jnp.where(jp < rope_tiles // 2, Q_PRESCALE, 1.0)

    def chunk(r0, rc):
        cos = jnp.where(is_rope, cos_ref[r0:r0 + rc, :] * scale, 1.0)
        sin = jnp.where(is_rope, sin_ref[r0:r0 + rc, :] * scale, 0.0)
        for c in range(o_ref.shape[1] // d):
            t = h_ref[r0:r0 + rc, c * d:(c + 1) * d]
            rot = pltpu.roll(t, d // 2, axis=1)
            o_ref[r0:r0 + rc, c * d:(c + 1) * d] = (t * cos + rot * sin).astype(o_ref.dtype)

    return [functools.partial(chunk, r0, rc) for r0, rc in _row_chunks(o_ref.shape[0], 4 * d)]


def _pipelined_kernel(*refs, epilogue, n_aux, use_halo, nb, n_tiles, tiles_per_seq):
    it = iter(refs)
    a_ref = next(it)
    ah_ref = next(it) if use_halo else None
    w_ref = next(it)
    aux = [next(it) for _ in range(n_aux)]
    o_ref = next(it)
    aext_ref = next(it) if use_halo else None
    h = (next(it), next(it))
    halo_rows = BF16_ROWS if use_halo else 0

    s = pl.program_id(0)
    t = jnp.minimum(s, n_tiles - 1)
    i = t // nb
    j = t % nb
    jp = jnp.maximum(s - 1, 0) % nb
    slot = s % 2

    @pl.when(s == 0)
    def _():
        h[1][...] = jnp.zeros(h[1].shape, F32)

    if use_halo:
        @pl.when((j == 0) & (s < n_tiles))
        def _():
            seq_start = (i % tiles_per_seq) == 0
            halo = ah_ref[...]
            aext_ref[0:BF16_ROWS, :] = jnp.where(seq_start, jnp.zeros_like(halo), halo)
            aext_ref[BF16_ROWS:, :] = a_ref[...]

    def step(cur, prev):
        chunks = epilogue(h[prev], halo_rows, aux, o_ref, jp)
        lhs = aext_ref[...] if use_halo else a_ref[...]
        wn = w_ref.shape[1]
        nt = max(1, wn // MXU_COLS)
        per = -(-len(chunks) // nt)
        for q in range(nt):
            for chunk in chunks[q * per:(q + 1) * per]:
                chunk()
            cols = slice(q * (wn // nt), (q + 1) * (wn // nt))
            h[cur][:, cols] = jnp.dot(lhs, w_ref[:, cols], preferred_element_type=F32)

    @pl.when(slot == 0)
    def _():
        step(0, 1)

    @pl.when(slot == 1)
    def _():
        step(1, 0)


def pipelined_matmul(a, w_tiles, *, epilogue, col_aux=(), row_aux=(), out_cols, out_dtype,
                     halo_seq_len=None, tm=1024, single_buffer_a=True, name):
    m, k = a.shape
    nb, _, wn = w_tiles.shape
    use_halo = halo_seq_len is not None
    tm = _pick(halo_seq_len if use_halo else m, tm)
    hb = tm // BF16_ROWS
    n_tiles = (m // tm) * nb
    hrows = tm + (BF16_ROWS if use_halo else 0)

    def cur(s):
        t = jnp.minimum(s, n_tiles - 1)
        return t // nb, t % nb

    def prev(s):
        t = jnp.maximum(s - 1, 0)
        return t // nb, t % nb

    a_kwargs = dict(pipeline_mode=pl.Buffered(1)) if single_buffer_a else {}
    in_specs = [pl.BlockSpec((tm, k), lambda s: (cur(s)[0], 0), **a_kwargs)]
    args = [a]
    if use_halo:
        in_specs.append(pl.BlockSpec((BF16_ROWS, k), lambda s: (jnp.maximum(cur(s)[0] * hb - 1, 0), 0)))
        args.append(a)
    in_specs.append(pl.BlockSpec((None, k, wn), lambda s: (cur(s)[1], 0, 0)))
    args.append(w_tiles)
    for arr in col_aux:
        in_specs.append(pl.BlockSpec((None,) + arr.shape[1:], lambda s: (prev(s)[1], 0, 0)))
        args.append(arr)
    for arr in row_aux:
        in_specs.append(pl.BlockSpec((tm, arr.shape[1]), lambda s: (prev(s)[0], 0)))
        args.append(arr)
    scratch = ([pltpu.VMEM((hrows, k), BF16)] if use_halo else []) + [pltpu.VMEM((hrows, wn), F32)] * 2
    kern = functools.partial(
        _pipelined_kernel, epilogue=epilogue, n_aux=len(col_aux) + len(row_aux), use_halo=use_halo,
        nb=nb, n_tiles=n_tiles, tiles_per_seq=(halo_seq_len // tm) if use_halo else 1)
    return pl.pallas_call(
        kern,
        grid=(n_tiles + 1,),
        in_specs=in_specs,
        out_specs=pl.BlockSpec((tm, out_cols), lambda s: prev(s)),
        out_shape=jax.ShapeDtypeStruct((m, nb * out_cols), out_dtype),
        scratch_shapes=scratch,
        compiler_params=_params("arbitrary"),
        name=name,
    )(*args)


def _rope_table_kernel(pos_ref, inv_ref, cos_ref, sin_ref):
    freqs = pos_ref[...].astype(F32) * inv_ref[...]
    lane = lax.broadcasted_iota(jnp.int32, freqs.shape, 1)
    sign = jnp.where(lane < DIFF_HEAD_DIM // 2, -1.0, 1.0)
    cos_ref[...] = jnp.cos(freqs)
    sin_ref[...] = jnp.sin(freqs) * sign


def rope_tables(positions):
    m = positions.size
    inv = 1.0 / (ROPE_THETA ** (jnp.arange(0, DIFF_HEAD_DIM, 2, dtype=F32) / DIFF_HEAD_DIM))
    inv_full = jnp.concatenate([inv, inv]).reshape(1, DIFF_HEAD_DIM)
    tm = _pick(m, 1024)
    shape = jax.ShapeDtypeStruct((m, DIFF_HEAD_DIM), F32)
    return pl.pallas_call(
        _rope_table_kernel,
        grid=(m // tm,),
        in_specs=[pl.BlockSpec((tm, 1), lambda i: (i, 0)),
                  pl.BlockSpec((1, DIFF_HEAD_DIM), lambda i: (0, 0))],
        out_specs=[pl.BlockSpec((tm, DIFF_HEAD_DIM), lambda i: (i, 0))] * 2,
        out_shape=[shape, shape],
        compiler_params=_params("arbitrary"),
        name="rope_tables",
    )(positions.reshape(m, 1), inv_full)


def _diff_attn_kernel(q_ref, k_ref, v_ref, lq1_ref, lk1_ref, lq2_ref, lk2_ref, sw_ref, o_ref,
                      m_ref, l_ref, acc_ref, *, lambda_init, tq):
    qi = pl.program_id(2)
    d = DIFF_HEAD_DIM
    q = q_ref[...]

    m_ref[...] = jnp.full(m_ref.shape, -jnp.inf, F32)
    l_ref[...] = jnp.zeros(l_ref.shape, F32)
    acc_ref[...] = jnp.zeros(acc_ref.shape, F32)

    def kv_block(kb, masked):
        start = pl.multiple_of(kb * tq, tq)
        k = k_ref[pl.ds(start, tq), :]
        v = v_ref[pl.ds(start, tq), :]
        probs = []
        alphas = []
        for c in range(2):
            s = lax.dot_general(q[:, c * d:(c + 1) * d], k[:, c * d:(c + 1) * d],
                                (((1,), (1,)), ((), ())), preferred_element_type=F32)
            if masked:
                row = lax.broadcasted_iota(jnp.int32, s.shape, 0)
                col = lax.broadcasted_iota(jnp.int32, s.shape, 1)
                s = jnp.where(col <= row, s, -jnp.inf)
            chunks = [s[:, t * LANES:(t + 1) * LANES] for t in range(tq // LANES)]
            cmax = functools.reduce(jnp.maximum, chunks)
            m_old = m_ref[c]
            m_new = jnp.maximum(m_old, jnp.max(cmax, axis=-1, keepdims=True))
            alpha = jnp.exp2(m_old - m_new)
            ps = [jnp.exp2(ch - m_new) for ch in chunks]
            l_ref[c] = alpha * l_ref[c] + functools.reduce(jnp.add, ps)
            m_ref[c] = m_new
            probs.append(jnp.concatenate(ps, axis=1).astype(v.dtype))
            alphas.append(alpha)
        pv = jnp.dot(jnp.concatenate(probs, axis=0), v, preferred_element_type=F32)
        for c in range(2):
            a2 = jnp.concatenate([alphas[c], alphas[c]], axis=1)
            acc_ref[c] = a2 * acc_ref[c] + pv[c * tq:(c + 1) * tq]

    def body(kb, carry):
        kv_block(kb, False)
        return carry

    lax.fori_loop(0, qi, body, 0)
    kv_block(qi, True)

    lam = (jnp.exp(jnp.sum(lq1_ref[...] * lk1_ref[...], axis=-1, keepdims=True))
           - jnp.exp(jnp.sum(lq2_ref[...] * lk2_ref[...], axis=-1, keepdims=True)) + lambda_init)
    l0 = jnp.sum(l_ref[0], axis=-1, keepdims=True)
    l1 = jnp.sum(l_ref[1], axis=-1, keepdims=True)
    o = acc_ref[0] / l0 - lam * (acc_ref[1] / l1)
    o = o * lax.rsqrt(jnp.mean(o * o, axis=-1, keepdims=True) + EPS)
    o_ref[...] = ((o * sw_ref[...]) * (1.0 - lambda_init)).astype(o_ref.dtype)


def diff_attention(qkv, lq1, lk1, lq2, lk2, subln_w, lambda_init, batch, seq_len, *, tq=512):
    m, n = qkv.shape
    d_model = n // 3
    hw = 2 * DIFF_HEAD_DIM
    heads = d_model // hw
    tq = _pick(seq_len, tq)
    nq = seq_len // tq
    kern = functools.partial(_diff_attn_kernel, lambda_init=lambda_init, tq=tq)
    vec = lambda a: a.reshape(1, -1)
    lam_spec = pl.BlockSpec((1, DIFF_HEAD_DIM), lambda b, h, qi: (0, 0))
    return pl.pallas_call(
        kern,
        grid=(batch, heads, nq),
        in_specs=[
            pl.BlockSpec((tq, hw), lambda b, h, qi: (b * nq + qi, h)),
            pl.BlockSpec((seq_len, hw), lambda b, h, qi: (b, heads + h)),
            pl.BlockSpec((seq_len, hw), lambda b, h, qi: (b, 2 * heads + h)),
            lam_spec, lam_spec, lam_spec, lam_spec,
            pl.BlockSpec((1, hw), lambda b, h, qi: (0, 0)),
        ],
        out_specs=pl.BlockSpec((tq, hw), lambda b, h, qi: (b * nq + qi, h)),
        out_shape=jax.ShapeDtypeStruct((m, d_model), BF16),
        scratch_shapes=[pltpu.VMEM((2, tq, LANES), F32), pltpu.VMEM((2, tq, LANES), F32),
                        pltpu.VMEM((2, tq, hw), F32)],
        compiler_params=_params("arbitrary", "arbitrary", "arbitrary"),
        name="diff_attention",
    )(qkv, qkv, qkv, vec(lq1), vec(lk1), vec(lq2), vec(lk2), vec(subln_w))


def _dot_exact(a, b):
    return jnp.dot(a, b, preferred_element_type=F32, precision=lax.Precision.HIGHEST)


def _ssd_kernel(z_ref, x_ref, xh_ref, b_ref, bh_ref, c_ref, ch_ref, dt_ref,
                cwx_ref, cwb_ref, cwc_ref, cbx_ref, cbb_ref, cbc_ref,
                dtb_ref, alog_ref, dexp_ref, nw_ref, o_ref, state_ref, *, heads_per_group):
    g = pl.program_id(1)
    c = pl.program_id(2)
    t = CHUNK
    hpg = heads_per_group
    p = SSM_HEADDIM
    gw = hpg * p
    nh = dt_ref.shape[1]

    @pl.when(c == 0)
    def _():
        state_ref[...] = jnp.zeros(state_ref.shape, F32)

    seq_start = c == 0

    def conv_silu(cur_ref, halo_ref, w_ref, bias_ref):
        halo = halo_ref[...]
        halo = jnp.where(seq_start, jnp.zeros_like(halo), halo)
        ext = jnp.concatenate([halo, cur_ref[...]], axis=0)
        return _silu(_causal_conv_ext(ext, SUBLANES, w_ref[...], bias_ref[...]))

    xc = conv_silu(x_ref, xh_ref, cwx_ref, cbx_ref)
    bc = conv_silu(b_ref, bh_ref, cwb_ref, cbb_ref)
    cc = conv_silu(c_ref, ch_ref, cwc_ref, cbc_ref)

    dtr = dt_ref[...] + dtb_ref[...]
    dtv = jnp.maximum(dtr, 0.0) + jnp.log1p(jnp.exp(-jnp.abs(dtr)))
    a_all = dtv * (-jnp.exp(alog_ref[...]))

    row_t = lax.broadcasted_iota(jnp.int32, (t, t), 0)
    col_t = lax.broadcasted_iota(jnp.int32, (t, t), 1)
    causal = col_t <= row_t
    a_cs_all = _dot_exact(jnp.where(causal, 1.0, 0.0), a_all)
    a_last = a_cs_all[t - 1:t, :]

    row_h = lax.broadcasted_iota(jnp.int32, (nh, nh), 0)
    col_h = lax.broadcasted_iota(jnp.int32, (nh, nh), 1)
    sel = jnp.where((row_h == g * hpg + col_h) & (col_h < hpg), 1.0, 0.0)
    a_cs = _dot_exact(a_cs_all, sel)
    a_cs_t = a_cs.T

    row_e = lax.broadcasted_iota(jnp.int32, (nh, gw), 0) - g * hpg
    col_e = lax.broadcasted_iota(jnp.int32, (nh, gw), 1)
    expand = jnp.where((col_e >= row_e * p) & (col_e < (row_e + 1) * p), 1.0, 0.0).astype(BF16)
    stacked = jnp.concatenate([dtv, jnp.exp(a_cs_all), jnp.exp(a_last - a_cs_all)], axis=0)
    hi = stacked.astype(BF16)
    lo = (stacked - hi.astype(F32)).astype(BF16)
    expanded = jnp.dot(jnp.concatenate([hi, lo], axis=1), jnp.concatenate([expand, expand], axis=0),
                       preferred_element_type=F32)
    dt_e = expanded[0:t]
    exp_acs_e = expanded[t:2 * t]
    dec_e = expanded[2 * t:3 * t]

    xdt = xc * dt_e
    xdt_b = xdt.astype(BF16)
    bb = bc.astype(BF16)
    cb = cc.astype(BF16)
    cbm = lax.dot_general(cb, bb, (((1,), (1,)), ((), ())), preferred_element_type=F32)

    state = state_ref[...]
    y = jnp.dot(cb, state.astype(BF16), preferred_element_type=F32) * exp_acs_e

    lane = lax.broadcasted_iota(jnp.int32, (t, 2 * p), 1)
    pieces = []
    for pair in range(hpg // 2):
        x_pair = xdt_b[:, pair * 2 * p:(pair + 1) * 2 * p]
        outs = []
        for hh in (2 * pair, 2 * pair + 1):
            seg = a_cs[:, hh:hh + 1] - a_cs_t[hh:hh + 1, :]
            decay = jnp.exp(jnp.where(causal, seg, -jnp.inf))
            outs.append(jnp.dot((cbm * decay).astype(BF16), x_pair, preferred_element_type=F32))
        pieces.append(jnp.where(lane < p, outs[0], outs[1]))
    y = y + jnp.concatenate(pieces, axis=1) + dexp_ref[...] * xc

    new_contrib = jnp.dot(bc.T.astype(BF16), (xdt * dec_e).astype(BF16), preferred_element_type=F32)
    state_ref[...] = state * exp_acs_e[t - 1:t, :] + new_contrib

    gn = y * _silu(z_ref[...])
    gn = gn * lax.rsqrt(jnp.mean(gn * gn, axis=-1, keepdims=True) + EPS)
    o_ref[...] = (gn * nw_ref[...]).astype(o_ref.dtype)


def ssd_mixer(zx, bcm, dt, conv_w, conv_b, dt_bias, a_log, d_skip, norm_w, batch, seq_len):
    m = zx.shape[0]
    d_inner = zx.shape[1] // 2
    nh = dt.shape[1]
    hpg = nh // SSM_GROUPS
    gw = d_inner // SSM_GROUPS
    n = SSM_STATE
    t = CHUNK
    nc = seq_len // t
    hb = t // SUBLANES
    ng = SSM_GROUPS
    xb0 = d_inner // gw
    cwb0 = d_inner // n

    def row(b, g, c):
        return b * nc + c

    def halo(b, g, c):
        return jnp.maximum((b * nc + c) * hb - 1, 0)

    d_exp = jnp.repeat(d_skip.astype(F32), SSM_HEADDIM).reshape(1, d_inner)
    conv_b2 = conv_b.reshape(1, -1)
    kern = functools.partial(_ssd_kernel, heads_per_group=hpg)
    return pl.pallas_call(
        kern,
        grid=(batch, ng, nc),
        in_specs=[
            pl.BlockSpec((t, gw), lambda b, g, c: (row(b, g, c), g)),
            pl.BlockSpec((t, gw), lambda b, g, c: (row(b, g, c), xb0 + g)),
            pl.BlockSpec((SUBLANES, gw), lambda b, g, c: (halo(b, g, c), xb0 + g)),
            pl.BlockSpec((t, n), lambda b, g, c: (row(b, g, c), g)),
            pl.BlockSpec((SUBLANES, n), lambda b, g, c: (halo(b, g, c), g)),
            pl.BlockSpec((t, n), lambda b, g, c: (row(b, g, c), ng + g)),
            pl.BlockSpec((SUBLANES, n), lambda b, g, c: (halo(b, g, c), ng + g)),
            pl.BlockSpec((t, nh), lambda b, g, c: (row(b, g, c), 0)),
            pl.BlockSpec((SSM_CONV, gw), lambda b, g, c: (0, g)),
            pl.BlockSpec((SSM_CONV, n), lambda b, g, c: (0, cwb0 + g)),
            pl.BlockSpec((SSM_CONV, n), lambda b, g, c: (0, cwb0 + ng + g)),
            pl.BlockSpec((1, gw), lambda b, g, c: (0, g)),
            pl.BlockSpec((1, n), lambda b, g, c: (0, cwb0 + g)),
            pl.BlockSpec((1, n), lambda b, g, c: (0, cwb0 + ng + g)),
            pl.BlockSpec((1, nh), lambda b, g, c: (0, 0)),
            pl.BlockSpec((1, nh), lambda b, g, c: (0, 0)),
            pl.BlockSpec((1, gw), lambda b, g, c: (0, g)),
            pl.BlockSpec((1, gw), lambda b, g, c: (0, g)),
        ],
        out_specs=pl.BlockSpec((t, gw), lambda b, g, c: (row(b, g, c), g)),
        out_shape=jax.ShapeDtypeStruct((m, d_inner), BF16),
        scratch_shapes=[pltpu.VMEM((n, gw), F32)],
        compiler_params=_params("arbitrary", "arbitrary", "arbitrary"),
        name="ssd_mixer",
    )(zx, zx, zx, bcm, bcm, bcm, bcm, dt, conv_w, conv_w, conv_w, conv_b2, conv_b2, conv_b2,
      dt_bias.reshape(1, nh), a_log.reshape(1, nh), d_exp, norm_w.reshape(1, d_inner))


def mamba_layer(xf, u, w_in, conv_w, conv_b, dt_bias, a_log, d_skip, norm_w, w_out, batch, seq_len):
    nh = dt_bias.shape[0]
    d_inner = norm_w.shape[0]
    bc_cols = 2 * SSM_GROUPS * SSM_STATE
    tn = _pick(bc_cols, 1024)
    zx = matmul(u, _col_tiles(w_in[:, :2 * d_inner], tn), tm=1024)
    bcm = matmul(u, _col_tiles(w_in[:, 2 * d_inner:2 * d_inner + bc_cols], tn), tm=1024)
    dt = matmul(u, _col_tiles(w_in[:, 2 * d_inner + bc_cols:], nh), tm=1024)
    yn = ssd_mixer(zx, bcm, dt, conv_w, conv_b, dt_bias, a_log, d_skip, norm_w, batch, seq_len)
    return matmul(yn, _col_tiles(w_out, _pick(w_out.shape[1], 512)), tm=1024, residual=xf,
                  single_buffer_a=True)


def attention_layer(xf, u, cos, sin, w_qkv, lq1, lk1, lq2, lk2, subln_w, w_o, lambda_init, batch, seq_len):
    d_model = w_o.shape[0]
    tn = _pick(d_model, 1024)
    qkv = pipelined_matmul(u, _col_tiles(w_qkv, tn),
                           epilogue=functools.partial(_epilogue_rope, rope_tiles=2 * d_model // tn),
                           row_aux=(cos, sin), out_cols=tn, out_dtype=BF16, single_buffer_a=False,
                           name="qkv_rope")
    o = diff_attention(qkv, lq1, lk1, lq2, lk2, subln_w, lambda_init, batch, seq_len)
    return matmul(o, _col_tiles(w_o, tn), tm=1024, residual=xf)


def ffn_layer(xf, u, w_up, conv_w, conv_b, w_down, seq_len):
    k, two_dff = w_up.shape
    d_ff = two_dff // 2
    tn = _pick(d_ff, 256)
    nb = d_ff // tn
    w_tiles = w_up.reshape(k, 2, nb, tn).transpose(2, 0, 1, 3).reshape(nb, k, 2 * tn).astype(BF16)
    cw = conv_w.astype(F32).reshape(FFN_CONV, 2, nb, tn).transpose(2, 0, 1, 3).reshape(nb, FFN_CONV, 2 * tn)
    cb = conv_b.astype(F32).reshape(2, nb, tn).transpose(1, 0, 2).reshape(nb, 1, 2 * tn)
    g = pipelined_matmul(u, w_tiles, epilogue=_epilogue_conv_gate, col_aux=(cw, cb), out_cols=tn,
                         out_dtype=BF16, halo_seq_len=seq_len, name="ffn_up")
    return matmul(g, _col_tiles(w_down, tn), tm=1024, residual=xf, single_buffer_a=True)


def kernel(x, positions, l0_norm_mix, l0_m_w_in, l0_m_conv_w, l0_m_conv_b, l0_m_dt_bias, l0_m_a_log, l0_m_d, l0_m_norm, l0_m_w_out, l0_norm_ffn, l0_f_w_up, l0_f_conv_w, l0_f_conv_b, l0_f_w_down, l1_norm_mix, l1_a_w_qkv, l1_a_lq1, l1_a_lk1, l1_a_lq2, l1_a_lk2, l1_a_subln, l1_a_w_o, l1_norm_ffn, l1_f_w_up, l1_f_conv_w, l1_f_conv_b, l1_f_w_down, l2_norm_mix, l2_m_w_in, l2_m_conv_w, l2_m_conv_b, l2_m_dt_bias, l2_m_a_log, l2_m_d, l2_m_norm, l2_m_w_out, l2_norm_ffn, l2_f_w_up, l2_f_conv_w, l2_f_conv_b, l2_f_w_down, l3_norm_mix, l3_a_w_qkv, l3_a_lq1, l3_a_lk1, l3_a_lq2, l3_a_lk2, l3_a_subln, l3_a_w_o, l3_norm_ffn, l3_f_w_up, l3_f_conv_w, l3_f_conv_b, l3_f_w_down, final_norm):
    norm_mix = [l0_norm_mix, l1_norm_mix, l2_norm_mix, l3_norm_mix]
    norm_ffn = [l0_norm_ffn, l1_norm_ffn, l2_norm_ffn, l3_norm_ffn]
    mixer_params = [
        (l0_m_w_in, l0_m_conv_w, l0_m_conv_b, l0_m_dt_bias, l0_m_a_log, l0_m_d, l0_m_norm, l0_m_w_out),
        (l1_a_w_qkv, l1_a_lq1, l1_a_lk1, l1_a_lq2, l1_a_lk2, l1_a_subln, l1_a_w_o),
        (l2_m_w_in, l2_m_conv_w, l2_m_conv_b, l2_m_dt_bias, l2_m_a_log, l2_m_d, l2_m_norm, l2_m_w_out),
        (l3_a_w_qkv, l3_a_lq1, l3_a_lk1, l3_a_lq2, l3_a_lk2, l3_a_subln, l3_a_w_o),
    ]
    ffn_params = [
        (l0_f_w_up, l0_f_conv_w, l0_f_conv_b, l0_f_w_down),
        (l1_f_w_up, l1_f_conv_w, l1_f_conv_b, l1_f_w_down),
        (l2_f_w_up, l2_f_conv_w, l2_f_conv_b, l2_f_w_down),
        (l3_f_w_up, l3_f_conv_w, l3_f_conv_b, l3_f_w_down),
    ]
    batch, seq_len, d_model = x.shape
    xf = x.reshape(batch * seq_len, d_model)
    cos, sin = rope_tables(positions)
    for i in range(len(norm_mix)):
        u = rmsnorm(xf, norm_mix[i], BF16)
        if i % N_MIXERS == 0:
            xf = mamba_layer(xf, u, *mixer_params[i], batch, seq_len)
        else:
            lambda_init = 0.8 - 0.6 * math.exp(-0.3 * i)
            xf = attention_layer(xf, u, cos, sin, *mixer_params[i], lambda_init, batch, seq_len)
        u = rmsnorm(xf, norm_ffn[i], BF16)
        xf = ffn_layer(xf, u, *ffn_params[i], seq_len)
    return rmsnorm(xf, final_norm, F32).reshape(batch, seq_len, d_model)
```

```python
import functools
import math

import jax
import jax.numpy as jnp
from jax import lax
from jax.experimental import pallas as pl
from jax.experimental.pallas import tpu as pltpu

F32 = jnp.float32
BF16 = jnp.bfloat16

EPS = 1e-5
N_MIXERS = 2

SSM_HEADDIM = 64
SSM_GROUPS = 8
SSM_STATE = 128
SSM_CONV = 4
CHUNK = 128

DIFF_HEAD_DIM = 128
ATTN_SCALE = DIFF_HEAD_DIM ** -0.5
Q_PRESCALE = ATTN_SCALE * math.log2(math.e)
ROPE_THETA = 10000.0

FFN_CONV = 3

V7X_VMEM_BYTES = 64 * 1024 * 1024
VMEM_LIMIT_BYTES = V7X_VMEM_BYTES - 8 * 1024 * 1024
SUBLANES = 8
BF16_ROWS = 16
LANES = 128
MXU_COLS = 256
EPILOGUE_VREGS = 32


def _params(*sem):
    return pltpu.CompilerParams(dimension_semantics=sem, vmem_limit_bytes=VMEM_LIMIT_BYTES)


def _pick(dim, pref):
    if dim <= pref:
        return dim
    t = pref
    while dim % t:
        t //= 2
    return t


def _silu(x):
    return x / (1.0 + jnp.exp(-x))


def _rmsnorm_kernel(x_ref, w_ref, o_ref):
    x = x_ref[...]
    y = x * lax.rsqrt(jnp.mean(x * x, axis=-1, keepdims=True) + EPS)
    o_ref[...] = (y * w_ref[...]).astype(o_ref.dtype)


def rmsnorm(x, w, out_dtype):
    m, d = x.shape
    tm = _pick(m, 256)
    return pl.pallas_call(
        _rmsnorm_kernel,
        grid=(m // tm,),
        in_specs=[pl.BlockSpec((tm, d), lambda i: (i, 0)),
                  pl.BlockSpec((1, d), lambda i: (0, 0))],
        out_specs=pl.BlockSpec((tm, d), lambda i: (i, 0)),
        out_shape=jax.ShapeDtypeStruct((m, d), out_dtype),
        compiler_params=_params("arbitrary"),
        name="rmsnorm",
    )(x, w.reshape(1, d))


def _mm_kernel(a_ref, w_ref, o_ref):
    o_ref[...] = jnp.dot(a_ref[...], w_ref[...], preferred_element_type=F32).astype(o_ref.dtype)


def _mm_res_kernel(a_ref, w_ref, r_ref, o_ref):
    o_ref[...] = r_ref[...] + jnp.dot(a_ref[...], w_ref[...], preferred_element_type=F32)


def matmul(a, w, *, col0=0, n=None, tm, tn, out_dtype=F32, residual=None, single_buffer_a=False):
    m, k = a.shape
    n = w.shape[1] - col0 if n is None else n
    tm = _pick(m, tm)
    tn = _pick(n, tn)
    assert col0 % tn == 0
    jb = col0 // tn
    a_kwargs = dict(pipeline_mode=pl.Buffered(1)) if single_buffer_a else {}
    in_specs = [pl.BlockSpec((tm, k), lambda i, j: (i, 0), **a_kwargs),
                pl.BlockSpec((k, tn), lambda i, j: (0, j + jb))]
    args = [a, w]
    body = _mm_kernel
    if residual is not None:
        in_specs.append(pl.BlockSpec((tm, tn), lambda i, j: (i, j)))
        args.append(residual)
        body = _mm_res_kernel
    return pl.pallas_call(
        body,
        grid=(m // tm, n // tn),
        in_specs=in_specs,
        out_specs=pl.BlockSpec((tm, tn), lambda i, j: (i, j)),
        out_shape=jax.ShapeDtypeStruct((m, n), out_dtype),
        compiler_params=_params("arbitrary", "arbitrary"),
        name="matmul",
    )(*args)


def _shift_rows(ext, s, halo_rows):
    if s == 0:
        return ext[halo_rows:]
    return pltpu.roll(ext, s, axis=0)[halo_rows:]


def _causal_conv_ext(ext, halo_rows, w, b):
    k_width = w.shape[0]
    y = b + w[0:1] * _shift_rows(ext, k_width - 1, halo_rows)
    for k in range(1, k_width):
        y = y + w[k:k + 1] * _shift_rows(ext, k_width - 1 - k, halo_rows)
    return y


def _causal_conv_ref(ext_ref, halo_rows, row0, rows, w, b):
    k_width = w.shape[0]
    base = halo_rows + row0 - (k_width - 1)
    y = b + w[0:1] * ext_ref[pl.ds(base, rows), :]
    for k in range(1, k_width):
        y = y + w[k:k + 1] * ext_ref[pl.ds(base + k, rows), :]
    return y


def _row_chunks(rows, cols):
    rc = max(BF16_ROWS, EPILOGUE_VREGS * SUBLANES * LANES // cols)
    rc = min(rc, rows)
    return [(r0, rc) for r0 in range(0, rows, rc)]


def _epilogue_conv_gate(h_ref, halo_rows, aux, o_ref, jp, *, aligned):
    cw, cb = aux[0][...], aux[1][...]
    tn = o_ref.shape[1]

    def chunk(r0, rc):
        if aligned:
            ext = h_ref[pl.ds(halo_rows + r0 - SUBLANES, rc + SUBLANES), :]
            y = _causal_conv_ext(ext, SUBLANES, cw, cb)
        else:
            y = _causal_conv_ref(h_ref, halo_rows, r0, rc, cw, cb)
        o_ref[r0:r0 + rc, :] = (_silu(y[:, :tn]) * y[:, tn:]).astype(o_ref.dtype)

    return [functools.partial(chunk, r0, rc) for r0, rc in _row_chunks(o_ref.shape[0], 2 * tn)]


def _epilogue_rope(h_ref, halo_rows, aux, o_ref, jp, *, rope_tiles):
    cos_ref, sin_ref = aux
    d = DIFF_HEAD_DIM
    is_rope = jp < rope_tiles
    scale = jnp.where(jp < rope_tiles // 2, Q_PRESCALE, 1.0)

    def chunk(r0, rc):
        cos = jnp.where(is_rope, cos_ref[r0:r0 + rc, :] * scale, 1.0)
        sin = jnp.where(is_rope, sin_ref[r0:r0 + rc, :] * scale, 0.0)
        for c in range(o_ref.shape[1] // d):
            t = h_ref[r0:r0 + rc, c * d:(c + 1) * d]
            rot = pltpu.roll(t, d // 2, axis=1)
            o_ref[r0:r0 + rc, c * d:(c + 1) * d] = (t * cos + rot * sin).astype(o_ref.dtype)

    return [functools.partial(chunk, r0, rc) for r0, rc in _row_chunks(o_ref.shape[0], 4 * d)]


def _pipelined_kernel(*refs, epilogue, n_w, n_aux, use_halo, lagged, nb, n_tiles, tiles_per_seq):
    it = iter(refs)
    a_ref = next(it)
    ah_ref = next(it) if use_halo else None
    w_refs = [next(it) for _ in range(n_w)]
    aux = [next(it) for _ in range(n_aux)]
    o_ref = next(it)
    aext_ref = next(it) if use_halo else None
    h = [next(it) for _ in range(2 if lagged else 1)]
    halo_rows = BF16_ROWS if use_halo else 0

    s = pl.program_id(0)
    t = jnp.minimum(s, n_tiles - 1)
    i = t // nb
    j = t % nb

    if use_halo:
        @pl.when((j == 0) & (s < n_tiles))
        def _():
            seq_start = (i % tiles_per_seq) == 0
            halo = ah_ref[...]
            aext_ref[0:BF16_ROWS, :] = jnp.where(seq_start, jnp.zeros_like(halo), halo)
            aext_ref[BF16_ROWS:, :] = a_ref[...]

    def matmul_into(h_ref):
        lhs = aext_ref[...] if use_halo else a_ref[...]
        col = 0
        for w_ref in w_refs:
            wn = w_ref.shape[1]
            step_cols = min(wn, MXU_COLS)
            for c0 in range(0, wn, step_cols):
                h_ref[:, col + c0:col + c0 + step_cols] = jnp.dot(
                    lhs, w_ref[:, c0:c0 + step_cols], preferred_element_type=F32)
            col += wn

    if not lagged:
        matmul_into(h[0])
        for chunk in epilogue(h[0], halo_rows, aux, o_ref, j):
            chunk()
        return

    jp = jnp.maximum(s - 1, 0) % nb
    slot = s % 2

    @pl.when(s == 0)
    def _():
        h[1][...] = jnp.zeros(h[1].shape, F32)

    def step(cur, prev):
        for chunk in epilogue(h[prev], halo_rows, aux, o_ref, jp):
            chunk()
        matmul_into(h[cur])

    @pl.when(slot == 0)
    def _():
        step(0, 1)

    @pl.when(slot == 1)
    def _():
        step(1, 0)


def pipelined_matmul(a, w, w_col_blocks, *, epilogue, col_aux=(), row_aux=(), out_cols, out_dtype,
                     nb, halo_seq_len=None, tm=1024, single_buffer_a=True, lagged=True, name):
    m, k = a.shape
    use_halo = halo_seq_len is not None
    tm = _pick(halo_seq_len if use_halo else m, tm)
    hb = tm // BF16_ROWS
    n_tiles = (m // tm) * nb
    hrows = tm + (BF16_ROWS if use_halo else 0)
    wn = sum(width for width, _ in w_col_blocks)

    def cur(s):
        t = jnp.minimum(s, n_tiles - 1)
        return t // nb, t % nb

    def prev(s):
        t = jnp.maximum(s - 1, 0) if lagged else s
        return t // nb, t % nb

    a_kwargs = dict(pipeline_mode=pl.Buffered(1)) if single_buffer_a else {}
    in_specs = [pl.BlockSpec((tm, k), lambda s: (cur(s)[0], 0), **a_kwargs)]
    args = [a]
    if use_halo:
        in_specs.append(pl.BlockSpec((BF16_ROWS, k), lambda s: (jnp.maximum(cur(s)[0] * hb - 1, 0), 0)))
        args.append(a)
    for width, first in w_col_blocks:
        in_specs.append(pl.BlockSpec((k, width), lambda s, first=first: (0, first + cur(s)[1])))
        args.append(w)
    for arr in col_aux:
        in_specs.append(pl.BlockSpec((None,) + arr.shape[1:], lambda s: (prev(s)[1], 0, 0)))
        args.append(arr)
    for arr in row_aux:
        in_specs.append(pl.BlockSpec((tm, arr.shape[1]), lambda s: (prev(s)[0], 0)))
        args.append(arr)
    scratch = ([pltpu.VMEM((hrows, k), BF16)] if use_halo else [])
    scratch += [pltpu.VMEM((hrows, wn), F32)] * (2 if lagged else 1)
    kern = functools.partial(
        _pipelined_kernel, epilogue=epilogue, n_w=len(w_col_blocks), n_aux=len(col_aux) + len(row_aux),
        use_halo=use_halo, lagged=lagged, nb=nb, n_tiles=n_tiles,
        tiles_per_seq=(halo_seq_len // tm) if use_halo else 1)
    return pl.pallas_call(
        kern,
        grid=(n_tiles + (1 if lagged else 0),),
        in_specs=in_specs,
        out_specs=pl.BlockSpec((tm, out_cols), lambda s: prev(s)),
        out_shape=jax.ShapeDtypeStruct((m, nb * out_cols), out_dtype),
        scratch_shapes=scratch,
        compiler_params=_params("arbitrary"),
        name=name,
    )(*args)


def _rope_table_kernel(pos_ref, inv_ref, cos_ref, sin_ref):
    freqs = pos_ref[...].astype(F32) * inv_ref[...]
    lane = lax.broadcasted_iota(jnp.int32, freqs.shape, 1)
    sign = jnp.where(lane < DIFF_HEAD_DIM // 2, -1.0, 1.0)
    cos_ref[...] = jnp.cos(freqs)
    sin_ref[...] = jnp.sin(freqs) * sign


def rope_tables(positions):
    m = positions.size
    inv = 1.0 / (ROPE_THETA ** (jnp.arange(0, DIFF_HEAD_DIM, 2, dtype=F32) / DIFF_HEAD_DIM))
    inv_full = jnp.concatenate([inv, inv]).reshape(1, DIFF_HEAD_DIM)
    tm = _pick(m, 1024)
    shape = jax.ShapeDtypeStruct((m, DIFF_HEAD_DIM), F32)
    return pl.pallas_call(
        _rope_table_kernel,
        grid=(m // tm,),
        in_specs=[pl.BlockSpec((tm, 1), lambda i: (i, 0)),
                  pl.BlockSpec((1, DIFF_HEAD_DIM), lambda i: (0, 0))],
        out_specs=[pl.BlockSpec((tm, DIFF_HEAD_DIM), lambda i: (i, 0))] * 2,
        out_shape=[shape, shape],
        compiler_params=_params("arbitrary"),
        name="rope_tables",
    )(positions.reshape(m, 1), inv_full)


def _diff_attn_kernel(q_ref, k_ref, v_ref, lq1_ref, lk1_ref, lq2_ref, lk2_ref, sw_ref, o_ref,
                      m_ref, l_ref, acc_ref, *, lambda_init, tq):
    qi = pl.program_id(2)
    d = DIFF_HEAD_DIM
    q = q_ref[...]

    m_ref[...] = jnp.full(m_ref.shape, -jnp.inf, F32)
    l_ref[...] = jnp.zeros(l_ref.shape, F32)
    acc_ref[...] = jnp.zeros(acc_ref.shape, F32)

    def kv_block(kb, masked):
        start = pl.multiple_of(kb * tq, tq)
        k = k_ref[pl.ds(start, tq), :]
        v = v_ref[pl.ds(start, tq), :]
        probs = []
        alphas = []
        for c in range(2):
            s = lax.dot_general(q[:, c * d:(c + 1) * d], k[:, c * d:(c + 1) * d],
                                (((1,), (1,)), ((), ())), preferred_element_type=F32)
            if masked:
                row = lax.broadcasted_iota(jnp.int32, s.shape, 0)
                col = lax.broadcasted_iota(jnp.int32, s.shape, 1)
                s = jnp.where(col <= row, s, -jnp.inf)
            chunks = [s[:, t * LANES:(t + 1) * LANES] for t in range(tq // LANES)]
            cmax = functools.reduce(jnp.maximum, chunks)
            m_old = m_ref[c]
            m_new = jnp.maximum(m_old, jnp.max(cmax, axis=-1, keepdims=True))
            alpha = jnp.exp2(m_old - m_new)
            ps = [jnp.exp2(ch - m_new) for ch in chunks]
            l_ref[c] = alpha * l_ref[c] + functools.reduce(jnp.add, ps)
            m_ref[c] = m_new
            probs.append(jnp.concatenate(ps, axis=1).astype(v.dtype))
            alphas.append(alpha)
        pv = jnp.dot(jnp.concatenate(probs, axis=0), v, preferred_element_type=F32)
        for c in range(2):
            a2 = jnp.concatenate([alphas[c], alphas[c]], axis=1)
            acc_ref[c] = a2 * acc_ref[c] + pv[c * tq:(c + 1) * tq]

    def body(kb, carry):
        kv_block(kb, False)
        return carry

    lax.fori_loop(0, qi, body, 0)
    kv_block(qi, True)

    lam = (jnp.exp(jnp.sum(lq1_ref[...] * lk1_ref[...], axis=-1, keepdims=True))
           - jnp.exp(jnp.sum(lq2_ref[...] * lk2_ref[...], axis=-1, keepdims=True)) + lambda_init)
    l0 = jnp.sum(l_ref[0], axis=-1, keepdims=True)
    l1 = jnp.sum(l_ref[1], axis=-1, keepdims=True)
    o = acc_ref[0] / l0 - lam * (acc_ref[1] / l1)
    o = o * lax.rsqrt(jnp.mean(o * o, axis=-1, keepdims=True) + EPS)
    o_ref[...] = ((o * sw_ref[...]) * (1.0 - lambda_init)).astype(o_ref.dtype)


def _cast_slabs(w, steps):
    per_step, rem = divmod(w.size, steps)
    c, rem2 = divmod(per_step, BF16_ROWS)
    if rem or rem2 or c % LANES:
        return None
    return w.reshape(steps, BF16_ROWS, c)


def diff_attention(qkv, lq1, lk1, lq2, lk2, subln_w, lambda_init, batch, seq_len, *, tq=512,
                   cast_weights=()):
    m, n = qkv.shape
    d_model = n // 3
    hw = 2 * DIFF_HEAD_DIM
    heads = d_model // hw
    tq = _pick(seq_len, tq)
    nq = seq_len // tq
    steps = batch * heads * nq
    slabs = [_cast_slabs(w, steps) for w in cast_weights]
    in_kernel = [s for s in slabs if s is not None]
    n_cast = len(in_kernel)

    def kern(*refs):
        main_in, cast_in = refs[:8], refs[8:8 + n_cast]
        o_ref = refs[8 + n_cast]
        cast_out = refs[9 + n_cast:9 + 2 * n_cast]
        scratch = refs[9 + 2 * n_cast:]
        for src, dst in zip(cast_in, cast_out):
            dst[...] = src[...].astype(dst.dtype)
        _diff_attn_kernel(*main_in, o_ref, *scratch, lambda_init=lambda_init, tq=tq)

    vec = lambda a: a.reshape(1, -1)
    lam_spec = pl.BlockSpec((1, DIFF_HEAD_DIM), lambda b, h, qi: (0, 0))
    slab_specs = [pl.BlockSpec((None,) + s.shape[1:], lambda b, h, qi: ((b * heads + h) * nq + qi, 0, 0))
                  for s in in_kernel]
    outs = pl.pallas_call(
        kern,
        grid=(batch, heads, nq),
        in_specs=[
            pl.BlockSpec((tq, hw), lambda b, h, qi: (b * nq + qi, h)),
            pl.BlockSpec((seq_len, hw), lambda b, h, qi: (b, heads + h)),
            pl.BlockSpec((seq_len, hw), lambda b, h, qi: (b, 2 * heads + h)),
            lam_spec, lam_spec, lam_spec, lam_spec,
            pl.BlockSpec((1, hw), lambda b, h, qi: (0, 0)),
        ] + slab_specs,
        out_specs=[pl.BlockSpec((tq, hw), lambda b, h, qi: (b * nq + qi, h))] + slab_specs,
        out_shape=[jax.ShapeDtypeStruct((m, d_model), BF16)]
        + [jax.ShapeDtypeStruct(s.shape, BF16) for s in in_kernel],
        scratch_shapes=[pltpu.VMEM((2, tq, LANES), F32), pltpu.VMEM((2, tq, LANES), F32),
                        pltpu.VMEM((2, tq, hw), F32)],
        compiler_params=_params("arbitrary", "arbitrary", "arbitrary"),
        name="diff_attention",
    )(qkv, qkv, qkv, vec(lq1), vec(lk1), vec(lq2), vec(lk2), vec(subln_w), *in_kernel)
    done = iter(outs[1:])
    casted = [next(done).reshape(w.shape) if s is not None else w.astype(BF16)
              for w, s in zip(cast_weights, slabs)]
    return outs[0], casted


def _dot_exact(a, b):
    return jnp.dot(a, b, preferred_element_type=F32, precision=lax.Precision.HIGHEST)


def _ssd_kernel(z_ref, x_ref, xh_ref, b_ref, bh_ref, c_ref, ch_ref, dt_ref,
                cwx_ref, cwb_ref, cwc_ref, cbx_ref, cbb_ref, cbc_ref,
                dtb_ref, alog_ref, dexp_ref, nw_ref, o_ref, state_ref, *, heads_per_group):
    g = pl.program_id(1)
    c = pl.program_id(2)
    t = CHUNK
    hpg = heads_per_group
    p = SSM_HEADDIM
    gw = hpg * p
    nh = dt_ref.shape[1]

    @pl.when(c == 0)
    def _():
        state_ref[...] = jnp.zeros(state_ref.shape, F32)

    seq_start = c == 0

    def conv_silu(cur_ref, halo_ref, w_ref, bias_ref):
        halo = halo_ref[...]
        halo = jnp.where(seq_start, jnp.zeros_like(halo), halo)
        ext = jnp.concatenate([halo, cur_ref[...]], axis=0)
        return _silu(_causal_conv_ext(ext, SUBLANES, w_ref[...], bias_ref[...]))

    xc = conv_silu(x_ref, xh_ref, cwx_ref, cbx_ref)
    bc = conv_silu(b_ref, bh_ref, cwb_ref, cbb_ref)
    cc = conv_silu(c_ref, ch_ref, cwc_ref, cbc_ref)

    dtr = dt_ref[...] + dtb_ref[...]
    dtv = jnp.maximum(dtr, 0.0) + jnp.log1p(jnp.exp(-jnp.abs(dtr)))
    a_all = dtv * (-jnp.exp(alog_ref[...]))

    row_t = lax.broadcasted_iota(jnp.int32, (t, t), 0)
    col_t = lax.broadcasted_iota(jnp.int32, (t, t), 1)
    causal = col_t <= row_t
    a_cs_all = _dot_exact(jnp.where(causal, 1.0, 0.0), a_all)
    a_last = a_cs_all[t - 1:t, :]

    row_h = lax.broadcasted_iota(jnp.int32, (nh, nh), 0)
    col_h = lax.broadcasted_iota(jnp.int32, (nh, nh), 1)
    sel = jnp.where((row_h == g * hpg + col_h) & (col_h < hpg), 1.0, 0.0)
    a_cs = _dot_exact(a_cs_all, sel)
    a_cs_t = a_cs.T

    row_e = lax.broadcasted_iota(jnp.int32, (nh, gw), 0) - g * hpg
    col_e = lax.broadcasted_iota(jnp.int32, (nh, gw), 1)
    expand = jnp.where((col_e >= row_e * p) & (col_e < (row_e + 1) * p), 1.0, 0.0).astype(BF16)
    stacked = jnp.concatenate([dtv, jnp.exp(a_cs_all), jnp.exp(a_last - a_cs_all)], axis=0)
    hi = stacked.astype(BF16)
    lo = (stacked - hi.astype(F32)).astype(BF16)
    expanded = jnp.dot(jnp.concatenate([hi, lo], axis=1), jnp.concatenate([expand, expand], axis=0),
                       preferred_element_type=F32)
    dt_e = expanded[0:t]
    exp_acs_e = expanded[t:2 * t]
    dec_e = expanded[2 * t:3 * t]

    xdt = xc * dt_e
    xdt_b = xdt.astype(BF16)
    bb = bc.astype(BF16)
    cb = cc.astype(BF16)
    cbm = lax.dot_general(cb, bb, (((1,), (1,)), ((), ())), preferred_element_type=F32)

    state = state_ref[...]
    y = jnp.dot(cb, state.astype(BF16), preferred_element_type=F32) * exp_acs_e

    lane = lax.broadcasted_iota(jnp.int32, (t, 2 * p), 1)
    pieces = []
    for pair in range(hpg // 2):
        x_pair = xdt_b[:, pair * 2 * p:(pair + 1) * 2 * p]
        outs = []
        for hh in (2 * pair, 2 * pair + 1):
            seg = a_cs[:, hh:hh + 1] - a_cs_t[hh:hh + 1, :]
            decay = jnp.exp(jnp.where(causal, seg, -jnp.inf))
            outs.append(jnp.dot((cbm * decay).astype(BF16), x_pair, preferred_element_type=F32))
        pieces.append(jnp.where(lane < p, outs[0], outs[1]))
    y = y + jnp.concatenate(pieces, axis=1) + dexp_ref[...] * xc

    new_contrib = jnp.dot(bc.T.astype(BF16), (xdt * dec_e).astype(BF16), preferred_element_type=F32)
    state_ref[...] = state * exp_acs_e[t - 1:t, :] + new_contrib

    gn = y * _silu(z_ref[...])
    gn = gn * lax.rsqrt(jnp.mean(gn * gn, axis=-1, keepdims=True) + EPS)
    o_ref[...] = (gn * nw_ref[...]).astype(o_ref.dtype)


def ssd_mixer(zx, bcm, dt, conv_w, conv_b, dt_bias, a_log, d_skip, norm_w, batch, seq_len):
    m = zx.shape[0]
    d_inner = zx.shape[1] // 2
    nh = dt.shape[1]
    hpg = nh // SSM_GROUPS
    gw = d_inner // SSM_GROUPS
    n = SSM_STATE
    t = CHUNK
    nc = seq_len // t
    hb = t // SUBLANES
    ng = SSM_GROUPS
    xb0 = d_inner // gw
    cwb0 = d_inner // n

    def row(b, g, c):
        return b * nc + c

    def halo(b, g, c):
        return jnp.maximum((b * nc + c) * hb - 1, 0)

    d_exp = jnp.repeat(d_skip.astype(F32), SSM_HEADDIM).reshape(1, d_inner)
    conv_b2 = conv_b.reshape(1, -1)
    kern = functools.partial(_ssd_kernel, heads_per_group=hpg)
    return pl.pallas_call(
        kern,
        grid=(batch, ng, nc),
        in_specs=[
            pl.BlockSpec((t, gw), lambda b, g, c: (row(b, g, c), g)),
            pl.BlockSpec((t, gw), lambda b, g, c: (row(b, g, c), xb0 + g)),
            pl.BlockSpec((SUBLANES, gw), lambda b, g, c: (halo(b, g, c), xb0 + g)),
            pl.BlockSpec((t, n), lambda b, g, c: (row(b, g, c), g)),
            pl.BlockSpec((SUBLANES, n), lambda b, g, c: (halo(b, g, c), g)),
            pl.BlockSpec((t, n), lambda b, g, c: (row(b, g, c), ng + g)),
            pl.BlockSpec((SUBLANES, n), lambda b, g, c: (halo(b, g, c), ng + g)),
            pl.BlockSpec((t, nh), lambda b, g, c: (row(b, g, c), 0)),
            pl.BlockSpec((SSM_CONV, gw), lambda b, g, c: (0, g)),
            pl.BlockSpec((SSM_CONV, n), lambda b, g, c: (0, cwb0 + g)),
            pl.BlockSpec((SSM_CONV, n), lambda b, g, c: (0, cwb0 + ng + g)),
            pl.BlockSpec((1, gw), lambda b, g, c: (0, g)),
            pl.BlockSpec((1, n), lambda b, g, c: (0, cwb0 + g)),
            pl.BlockSpec((1, n), lambda b, g, c: (0, cwb0 + ng + g)),
            pl.BlockSpec((1, nh), lambda b, g, c: (0, 0)),
            pl.BlockSpec((1, nh), lambda b, g, c: (0, 0)),
            pl.BlockSpec((1, gw), lambda b, g, c: (0, g)),
            pl.BlockSpec((1, gw), lambda b, g, c: (0, g)),
        ],
        out_specs=pl.BlockSpec((t, gw), lambda b, g, c: (row(b, g, c), g)),
        out_shape=jax.ShapeDtypeStruct((m, d_inner), BF16),
        scratch_shapes=[pltpu.VMEM((n, gw), F32)],
        compiler_params=_params("arbitrary", "arbitrary", "arbitrary"),
        name="ssd_mixer",
    )(zx, zx, zx, bcm, bcm, bcm, bcm, dt, conv_w, conv_w, conv_w, conv_b2, conv_b2, conv_b2,
      dt_bias.reshape(1, nh), a_log.reshape(1, nh), d_exp, norm_w.reshape(1, d_inner))


def mamba_layer(xf, u, w_in, conv_w, conv_b, dt_bias, a_log, d_skip, norm_w, w_out, batch, seq_len):
    nh = dt_bias.shape[0]
    d_inner = norm_w.shape[0]
    bc_cols = 2 * SSM_GROUPS * SSM_STATE
    w_in = w_in.astype(BF16)
    zx = matmul(u, w_in, col0=0, n=2 * d_inner, tm=1024, tn=1024)
    bcm = matmul(u, w_in, col0=2 * d_inner, n=bc_cols, tm=1024, tn=1024)
    dt = matmul(u, w_in, col0=2 * d_inner + bc_cols, n=nh, tm=1024, tn=nh)
    yn = ssd_mixer(zx, bcm, dt, conv_w, conv_b, dt_bias, a_log, d_skip, norm_w, batch, seq_len)
    return matmul(yn, w_out.astype(BF16), tm=1024, tn=512, residual=xf, single_buffer_a=True)


def attention_layer(xf, u, cos, sin, w_qkv, lq1, lk1, lq2, lk2, subln_w, w_o, lambda_init, batch, seq_len,
                    cast_weights=()):
    d_model = w_o.shape[0]
    tn = _pick(d_model, 1024)
    nb = w_qkv.shape[1] // tn
    qkv = pipelined_matmul(u, w_qkv.astype(BF16), [(tn, 0)],
                           epilogue=functools.partial(_epilogue_rope, rope_tiles=2 * d_model // tn),
                           row_aux=(cos, sin), out_cols=tn, out_dtype=BF16, nb=nb,
                           single_buffer_a=False, name="qkv_rope")
    o, casted = diff_attention(qkv, lq1, lk1, lq2, lk2, subln_w, lambda_init, batch, seq_len,
                               cast_weights=cast_weights)
    return matmul(o, w_o.astype(BF16), tm=1024, tn=1024, residual=xf), casted


def ffn_layer(xf, u, w_up, conv_w, conv_b, w_down, seq_len, variant):
    d_ff = w_up.shape[1] // 2
    tn = _pick(d_ff, 256)
    nb = d_ff // tn
    cw = conv_w.astype(F32).reshape(FFN_CONV, 2, nb, tn).transpose(2, 0, 1, 3).reshape(nb, FFN_CONV, 2 * tn)
    cb = conv_b.astype(F32).reshape(2, nb, tn).transpose(1, 0, 2).reshape(nb, 1, 2 * tn)
    g = pipelined_matmul(u, w_up.astype(BF16), [(tn, 0), (tn, nb)],
                         epilogue=functools.partial(_epilogue_conv_gate, aligned=variant["aligned"]),
                         col_aux=(cw, cb), out_cols=tn, out_dtype=BF16, nb=nb, halo_seq_len=seq_len,
                         lagged=variant["lagged"], single_buffer_a=variant["single_buffer_a"],
                         name=variant["name"])
    return matmul(g, w_down.astype(BF16), tm=1024, tn=256, residual=xf, single_buffer_a=True)


FFN_VARIANTS = (
    dict(name="ffn_up_lag_aligned", lagged=True, aligned=True, single_buffer_a=True),
    dict(name="ffn_up_nolag_aligned", lagged=False, aligned=True, single_buffer_a=True),
    dict(name="ffn_up_lag_aligned_2buf", lagged=True, aligned=True, single_buffer_a=False),
    dict(name="ffn_up_lag_shifted", lagged=True, aligned=False, single_buffer_a=True),
)


def kernel(x, positions, l0_norm_mix, l0_m_w_in, l0_m_conv_w, l0_m_conv_b, l0_m_dt_bias, l0_m_a_log, l0_m_d, l0_m_norm, l0_m_w_out, l0_norm_ffn, l0_f_w_up, l0_f_conv_w, l0_f_conv_b, l0_f_w_down, l1_norm_mix, l1_a_w_qkv, l1_a_lq1, l1_a_lk1, l1_a_lq2, l1_a_lk2, l1_a_subln, l1_a_w_o, l1_norm_ffn, l1_f_w_up, l1_f_conv_w, l1_f_conv_b, l1_f_w_down, l2_norm_mix, l2_m_w_in, l2_m_conv_w, l2_m_conv_b, l2_m_dt_bias, l2_m_a_log, l2_m_d, l2_m_norm, l2_m_w_out, l2_norm_ffn, l2_f_w_up, l2_f_conv_w, l2_f_conv_b, l2_f_w_down, l3_norm_mix, l3_a_w_qkv, l3_a_lq1, l3_a_lk1, l3_a_lq2, l3_a_lk2, l3_a_subln, l3_a_w_o, l3_norm_ffn, l3_f_w_up, l3_f_conv_w, l3_f_conv_b, l3_f_w_down, final_norm):
    norm_mix = [l0_norm_mix, l1_norm_mix, l2_norm_mix, l3_norm_mix]
    norm_ffn = [l0_norm_ffn, l1_norm_ffn, l2_norm_ffn, l3_norm_ffn]
    mixer_params = [
        (l0_m_w_in, l0_m_conv_w, l0_m_conv_b, l0_m_dt_bias, l0_m_a_log, l0_m_d, l0_m_norm, l0_m_w_out),
        (l1_a_w_qkv, l1_a_lq1, l1_a_lk1, l1_a_lq2, l1_a_lk2, l1_a_subln, l1_a_w_o),
        (l2_m_w_in, l2_m_conv_w, l2_m_conv_b, l2_m_dt_bias, l2_m_a_log, l2_m_d, l2_m_norm, l2_m_w_out),
        (l3_a_w_qkv, l3_a_lq1, l3_a_lk1, l3_a_lq2, l3_a_lk2, l3_a_subln, l3_a_w_o),
    ]
    ffn_params = [
        (l0_f_w_up, l0_f_conv_w, l0_f_conv_b, l0_f_w_down),
        (l1_f_w_up, l1_f_conv_w, l1_f_conv_b, l1_f_w_down),
        (l2_f_w_up, l2_f_conv_w, l2_f_conv_b, l2_f_w_down),
        (l3_f_w_up, l3_f_conv_w, l3_f_conv_b, l3_f_w_down),
    ]
    batch, seq_len, d_model = x.shape
    xf = x.reshape(batch * seq_len, d_model)
    cos, sin = rope_tables(positions)
    for i in range(len(norm_mix)):
        u = rmsnorm(xf, norm_mix[i], BF16)
        if i % N_MIXERS == 0:
            xf = mamba_layer(xf, u, *mixer_params[i], batch, seq_len)
        else:
            lambda_init = 0.8 - 0.6 * math.exp(-0.3 * i)
            w_up, cw, cb, w_down = ffn_params[i]
            cast_weights = (w_up, w_down) if i == 1 else ()
            xf, casted = attention_layer(xf, u, cos, sin, *mixer_params[i], lambda_init, batch, seq_len,
                                         cast_weights=cast_weights)
            if casted:
                ffn_params[i] = (casted[0], cw, cb, casted[1])
        u = rmsnorm(xf, norm_ffn[i], BF16)
        xf = ffn_layer(xf, u, *ffn_params[i], seq_len, FFN_VARIANTS[i])
    return rmsnorm(xf, final_norm, F32).reshape(batch, seq_len, d_model)
```

```python
import functools
import math

import jax
import jax.numpy as jnp
from jax import lax
from jax.experimental import pallas as pl
from jax.experimental.pallas import tpu as pltpu

F32 = jnp.float32
BF16 = jnp.bfloat16

EPS = 1e-5
N_MIXERS = 2

SSM_HEADDIM = 64
SSM_GROUPS = 8
SSM_STATE = 128
SSM_CONV = 4
CHUNK = 128

DIFF_HEAD_DIM = 128
ATTN_SCALE = DIFF_HEAD_DIM ** -0.5
Q_PRESCALE = ATTN_SCALE * math.log2(math.e)
ROPE_THETA = 10000.0

FFN_CONV = 3

V7X_VMEM_BYTES = 64 * 1024 * 1024
VMEM_LIMIT_BYTES = V7X_VMEM_BYTES - 8 * 1024 * 1024
SUBLANES = 8
BF16_ROWS = 16
LANES = 128
MXU_COLS = 256
EPILOGUE_VREGS = 32


def _params(*sem):
    return pltpu.CompilerParams(dimension_semantics=sem, vmem_limit_bytes=VMEM_LIMIT_BYTES)


def _pick(dim, pref):
    if dim <= pref:
        return dim
    t = pref
    while dim % t:
        t //= 2
    return t


def _silu(x):
    return x / (1.0 + jnp.exp(-x))


def _rmsnorm_kernel(x_ref, w_ref, o_ref):
    x = x_ref[...]
    y = x * lax.rsqrt(jnp.mean(x * x, axis=-1, keepdims=True) + EPS)
    o_ref[...] = (y * w_ref[...]).astype(o_ref.dtype)


def rmsnorm(x, w, out_dtype):
    m, d = x.shape
    tm = _pick(m, 256)
    return pl.pallas_call(
        _rmsnorm_kernel,
        grid=(m // tm,),
        in_specs=[pl.BlockSpec((tm, d), lambda i: (i, 0)),
                  pl.BlockSpec((1, d), lambda i: (0, 0))],
        out_specs=pl.BlockSpec((tm, d), lambda i: (i, 0)),
        out_shape=jax.ShapeDtypeStruct((m, d), out_dtype),
        compiler_params=_params("arbitrary"),
        name="rmsnorm",
    )(x, w.reshape(1, d))


def _mm_kernel(a_ref, w_ref, o_ref):
    o_ref[...] = jnp.dot(a_ref[...], w_ref[...], preferred_element_type=F32).astype(o_ref.dtype)


def _mm_res_kernel(a_ref, w_ref, r_ref, o_ref):
    o_ref[...] = r_ref[...] + jnp.dot(a_ref[...], w_ref[...], preferred_element_type=F32)


def matmul(a, w, *, col0=0, n=None, tm, tn, out_dtype=F32, residual=None, single_buffer_a=False):
    m, k = a.shape
    n = w.shape[1] - col0 if n is None else n
    tm = _pick(m, tm)
    tn = _pick(n, tn)
    assert col0 % tn == 0
    jb = col0 // tn
    a_kwargs = dict(pipeline_mode=pl.Buffered(1)) if single_buffer_a else {}
    in_specs = [pl.BlockSpec((tm, k), lambda i, j: (i, 0), **a_kwargs),
                pl.BlockSpec((k, tn), lambda i, j: (0, j + jb))]
    args = [a, w]
    body = _mm_kernel
    if residual is not None:
        in_specs.append(pl.BlockSpec((tm, tn), lambda i, j: (i, j)))
        args.append(residual)
        body = _mm_res_kernel
    return pl.pallas_call(
        body,
        grid=(m // tm, n // tn),
        in_specs=in_specs,
        out_specs=pl.BlockSpec((tm, tn), lambda i, j: (i, j)),
        out_shape=jax.ShapeDtypeStruct((m, n), out_dtype),
        compiler_params=_params("arbitrary", "arbitrary"),
        name="matmul",
    )(*args)


def _shift_rows(ext, s, halo_rows):
    if s == 0:
        return ext[halo_rows:]
    return pltpu.roll(ext, s, axis=0)[halo_rows:]


def _causal_conv_ext(ext, halo_rows, w, b):
    k_width = w.shape[0]
    y = b + w[0:1] * _shift_rows(ext, k_width - 1, halo_rows)
    for k in range(1, k_width):
        y = y + w[k:k + 1] * _shift_rows(ext, k_width - 1 - k, halo_rows)
    return y


def _causal_conv_ref(ext_ref, halo_rows, row0, rows, w, b):
    k_width = w.shape[0]
    base = halo_rows + row0 - (k_width - 1)
    y = b + w[0:1] * ext_ref[pl.ds(base, rows), :]
    for k in range(1, k_width):
        y = y + w[k:k + 1] * ext_ref[pl.ds(base + k, rows), :]
    return y


def _row_chunks(rows, cols):
    rc = max(BF16_ROWS, EPILOGUE_VREGS * SUBLANES * LANES // cols)
    rc = min(rc, rows)
    return [(r0, rc) for r0 in range(0, rows, rc)]


def _epilogue_conv_gate(h_ref, halo_rows, aux, o_ref, jp, *, aligned):
    cw, cb = aux[0][...], aux[1][...]
    tn = o_ref.shape[1]

    def chunk(r0, rc):
        if aligned:
            ext = h_ref[pl.ds(halo_rows + r0 - SUBLANES, rc + SUBLANES), :]
            y = _causal_conv_ext(ext, SUBLANES, cw, cb)
        else:
            y = _causal_conv_ref(h_ref, halo_rows, r0, rc, cw, cb)
        o_ref[r0:r0 + rc, :] = (_silu(y[:, :tn]) * y[:, tn:]).astype(o_ref.dtype)

    return [functools.partial(chunk, r0, rc) for r0, rc in _row_chunks(o_ref.shape[0], 2 * tn)]


def _epilogue_rope(h_ref, halo_rows, aux, o_ref, jp, *, rope_tiles):
    cos_ref, sin_ref = aux
    d = DIFF_HEAD_DIM
    is_rope = jp < rope_tiles
    scale = jnp.where(jp < rope_tiles // 2, Q_PRESCALE, 1.0)

    def chunk(r0, rc):
        cos = jnp.where(is_rope, cos_ref[r0:r0 + rc, :] * scale, 1.0)
        sin = jnp.where(is_rope, sin_ref[r0:r0 + rc, :] * scale, 0.0)
        for c in range(o_ref.shape[1] // d):
            t = h_ref[r0:r0 + rc, c * d:(c + 1) * d]
            rot = pltpu.roll(t, d // 2, axis=1)
            o_ref[r0:r0 + rc, c * d:(c + 1) * d] = (t * cos + rot * sin).astype(o_ref.dtype)

    return [functools.partial(chunk, r0, rc) for r0, rc in _row_chunks(o_ref.shape[0], 4 * d)]


def _pipelined_kernel(*refs, epilogue, n_w, n_aux, use_halo, lagged, nb, n_tiles, tiles_per_seq):
    it = iter(refs)
    a_ref = next(it)
    ah_ref = next(it) if use_halo else None
    w_refs = [next(it) for _ in range(n_w)]
    aux = [next(it) for _ in range(n_aux)]
    o_ref = next(it)
    aext_ref = next(it) if use_halo else None
    h = [next(it) for _ in range(2 if lagged else 1)]
    halo_rows = BF16_ROWS if use_halo else 0

    s = pl.program_id(0)
    t = jnp.minimum(s, n_tiles - 1)
    i = t // nb
    j = t % nb

    if use_halo:
        @pl.when((j == 0) & (s < n_tiles))
        def _():
            seq_start = (i % tiles_per_seq) == 0
            halo = ah_ref[...]
            aext_ref[0:BF16_ROWS, :] = jnp.where(seq_start, jnp.zeros_like(halo), halo)
            aext_ref[BF16_ROWS:, :] = a_ref[...]

    def matmul_into(h_ref):
        lhs = aext_ref[...] if use_halo else a_ref[...]
        col = 0
        for w_ref in w_refs:
            wn = w_ref.shape[1]
            step_cols = min(wn, MXU_COLS)
            for c0 in range(0, wn, step_cols):
                h_ref[:, col + c0:col + c0 + step_cols] = jnp.dot(
                    lhs, w_ref[:, c0:c0 + step_cols], preferred_element_type=F32)
            col += wn

    if not lagged:
        matmul_into(h[0])
        for chunk in epilogue(h[0], halo_rows, aux, o_ref, j):
            chunk()
        return

    jp = jnp.maximum(s - 1, 0) % nb
    slot = s % 2

    @pl.when(s == 0)
    def _():
        h[1][...] = jnp.zeros(h[1].shape, F32)

    def step(cur, prev):
        for chunk in epilogue(h[prev], halo_rows, aux, o_ref, jp):
            chunk()
        matmul_into(h[cur])

    @pl.when(slot == 0)
    def _():
        step(0, 1)

    @pl.when(slot == 1)
    def _():
        step(1, 0)


def pipelined_matmul(a, w, w_col_blocks, *, epilogue, col_aux=(), row_aux=(), out_cols, out_dtype,
                     nb, halo_seq_len=None, tm=1024, single_buffer_a=True, lagged=True, name):
    m, k = a.shape
    use_halo = halo_seq_len is not None
    tm = _pick(halo_seq_len if use_halo else m, tm)
    hb = tm // BF16_ROWS
    n_tiles = (m // tm) * nb
    hrows = tm + (BF16_ROWS if use_halo else 0)
    wn = sum(width for width, _ in w_col_blocks)

    def cur(s):
        t = jnp.minimum(s, n_tiles - 1)
        return t // nb, t % nb

    def prev(s):
        t = jnp.maximum(s - 1, 0) if lagged else s
        return t // nb, t % nb

    a_kwargs = dict(pipeline_mode=pl.Buffered(1)) if single_buffer_a else {}
    in_specs = [pl.BlockSpec((tm, k), lambda s: (cur(s)[0], 0), **a_kwargs)]
    args = [a]
    if use_halo:
        in_specs.append(pl.BlockSpec((BF16_ROWS, k), lambda s: (jnp.maximum(cur(s)[0] * hb - 1, 0), 0)))
        args.append(a)
    for width, first in w_col_blocks:
        in_specs.append(pl.BlockSpec((k, width), lambda s, first=first: (0, first + cur(s)[1])))
        args.append(w)
    for arr in col_aux:
        in_specs.append(pl.BlockSpec((None,) + arr.shape[1:], lambda s: (prev(s)[1], 0, 0)))
        args.append(arr)
    for arr in row_aux:
        in_specs.append(pl.BlockSpec((tm, arr.shape[1]), lambda s: (prev(s)[0], 0)))
        args.append(arr)
    scratch = ([pltpu.VMEM((hrows, k), BF16)] if use_halo else [])
    scratch += [pltpu.VMEM((hrows, wn), F32)] * (2 if lagged else 1)
    kern = functools.partial(
        _pipelined_kernel, epilogue=epilogue, n_w=len(w_col_blocks), n_aux=len(col_aux) + len(row_aux),
        use_halo=use_halo, lagged=lagged, nb=nb, n_tiles=n_tiles,
        tiles_per_seq=(halo_seq_len // tm) if use_halo else 1)
    return pl.pallas_call(
        kern,
        grid=(n_tiles + (1 if lagged else 0),),
        in_specs=in_specs,
        out_specs=pl.BlockSpec((tm, out_cols), lambda s: prev(s)),
        out_shape=jax.ShapeDtypeStruct((m, nb * out_cols), out_dtype),
        scratch_shapes=scratch,
        compiler_params=_params("arbitrary"),
        name=name,
    )(*args)


def _rope_table_kernel(pos_ref, inv_ref, cos_ref, sin_ref):
    freqs = pos_ref[...].astype(F32) * inv_ref[...]
    lane = lax.broadcasted_iota(jnp.int32, freqs.shape, 1)
    sign = jnp.where(lane < DIFF_HEAD_DIM // 2, -1.0, 1.0)
    cos_ref[...] = jnp.cos(freqs)
    sin_ref[...] = jnp.sin(freqs) * sign


def rope_tables(positions):
    m = positions.size
    inv = 1.0 / (ROPE_THETA ** (jnp.arange(0, DIFF_HEAD_DIM, 2, dtype=F32) / DIFF_HEAD_DIM))
    inv_full = jnp.concatenate([inv, inv]).reshape(1, DIFF_HEAD_DIM)
    tm = _pick(m, 1024)
    shape = jax.ShapeDtypeStruct((m, DIFF_HEAD_DIM), F32)
    return pl.pallas_call(
        _rope_table_kernel,
        grid=(m // tm,),
        in_specs=[pl.BlockSpec((tm, 1), lambda i: (i, 0)),
                  pl.BlockSpec((1, DIFF_HEAD_DIM), lambda i: (0, 0))],
        out_specs=[pl.BlockSpec((tm, DIFF_HEAD_DIM), lambda i: (i, 0))] * 2,
        out_shape=[shape, shape],
        compiler_params=_params("arbitrary"),
        name="rope_tables",
    )(positions.reshape(m, 1), inv_full)


def _diff_attn_kernel(q_ref, k_ref, v_ref, lq1_ref, lk1_ref, lq2_ref, lk2_ref, sw_ref, o_ref,
                      m_ref, l_ref, acc_ref, *, lambda_init, tq):
    qi = pl.program_id(2)
    d = DIFF_HEAD_DIM
    q = q_ref[...]

    m_ref[...] = jnp.full(m_ref.shape, -jnp.inf, F32)
    l_ref[...] = jnp.zeros(l_ref.shape, F32)
    acc_ref[...] = jnp.zeros(acc_ref.shape, F32)

    def kv_block(kb, masked):
        start = pl.multiple_of(kb * tq, tq)
        k = k_ref[pl.ds(start, tq), :]
        v = v_ref[pl.ds(start, tq), :]
        probs = []
        alphas = []
        for c in range(2):
            s = lax.dot_general(q[:, c * d:(c + 1) * d], k[:, c * d:(c + 1) * d],
                                (((1,), (1,)), ((), ())), preferred_element_type=F32)
            if masked:
                row = lax.broadcasted_iota(jnp.int32, s.shape, 0)
                col = lax.broadcasted_iota(jnp.int32, s.shape, 1)
                s = jnp.where(col <= row, s, -jnp.inf)
            chunks = [s[:, t * LANES:(t + 1) * LANES] for t in range(tq // LANES)]
            cmax = functools.reduce(jnp.maximum, chunks)
            m_old = m_ref[c]
            m_new = jnp.maximum(m_old, jnp.max(cmax, axis=-1, keepdims=True))
            alpha = jnp.exp2(m_old - m_new)
            ps = [jnp.exp2(ch - m_new) for ch in chunks]
            l_ref[c] = alpha * l_ref[c] + functools.reduce(jnp.add, ps)
            m_ref[c] = m_new
            probs.append(jnp.concatenate(ps, axis=1).astype(v.dtype))
            alphas.append(alpha)
        pv = jnp.dot(jnp.concatenate(probs, axis=0), v, preferred_element_type=F32)
        for c in range(2):
            a2 = jnp.concatenate([alphas[c], alphas[c]], axis=1)
            acc_ref[c] = a2 * acc_ref[c] + pv[c * tq:(c + 1) * tq]

    def body(kb, carry):
        kv_block(kb, False)
        return carry

    lax.fori_loop(0, qi, body, 0)
    kv_block(qi, True)

    lam = (jnp.exp(jnp.sum(lq1_ref[...] * lk1_ref[...], axis=-1, keepdims=True))
           - jnp.exp(jnp.sum(lq2_ref[...] * lk2_ref[...], axis=-1, keepdims=True)) + lambda_init)
    l0 = jnp.sum(l_ref[0], axis=-1, keepdims=True)
    l1 = jnp.sum(l_ref[1], axis=-1, keepdims=True)
    o = acc_ref[0] / l0 - lam * (acc_ref[1] / l1)
    o = o * lax.rsqrt(jnp.mean(o * o, axis=-1, keepdims=True) + EPS)
    o_ref[...] = ((o * sw_ref[...]) * (1.0 - lambda_init)).astype(o_ref.dtype)


CAST_BLOCK_BYTES = 4 * 1024 * 1024


def _cast_rows(w, steps):
    k, n = w.shape
    for rows in range(BF16_ROWS, k + 1, BF16_ROWS):
        if k % rows == 0 and k // rows <= steps:
            return rows if rows * n * 4 <= CAST_BLOCK_BYTES else None
    return None


def _plan_casts(weights, steps, lin):
    rows = [_cast_rows(w, steps) for w in weights]
    arrays, specs, spans = [], [], []
    at_end = False
    for w, r in zip(weights, rows):
        if not r:
            continue
        nb = w.shape[0] // r
        first = steps - nb if at_end else 0
        at_end = not at_end if nb < steps else at_end
        arrays.append(w)
        spans.append((first, nb))
        specs.append(pl.BlockSpec(
            (r, w.shape[1]),
            lambda *ids, first=first, nb=nb: (jnp.clip(lin(*ids) - first, 0, nb - 1), 0)))

    def finish(outputs):
        done = iter(outputs)
        return [next(done) if r else w.astype(BF16) for w, r in zip(weights, rows)]

    return arrays, specs, spans, finish


def _run_casts(step, spans, cast_in, cast_out):
    for (first, nb), src, dst in zip(spans, cast_in, cast_out):
        @pl.when((step >= first) & (step < first + nb))
        def _():
            dst[...] = src[...].astype(dst.dtype)


def diff_attention(qkv, lq1, lk1, lq2, lk2, subln_w, lambda_init, batch, seq_len, *, tq=512,
                   cast_weights=()):
    m, n = qkv.shape
    d_model = n // 3
    hw = 2 * DIFF_HEAD_DIM
    heads = d_model // hw
    tq = _pick(seq_len, tq)
    nq = seq_len // tq

    def lin(b, h, qi):
        return (b * heads + h) * nq + qi

    cast_arrays, cast_specs, spans, finish = _plan_casts(cast_weights, batch * heads * nq, lin)
    n_cast = len(cast_arrays)

    def kern(*refs):
        main_in, cast_in = refs[:8], refs[8:8 + n_cast]
        o_ref = refs[8 + n_cast]
        cast_out = refs[9 + n_cast:9 + 2 * n_cast]
        scratch = refs[9 + 2 * n_cast:]
        _run_casts(lin(pl.program_id(0), pl.program_id(1), pl.program_id(2)), spans, cast_in, cast_out)
        _diff_attn_kernel(*main_in, o_ref, *scratch, lambda_init=lambda_init, tq=tq)

    vec = lambda a: a.reshape(1, -1)
    lam_spec = pl.BlockSpec((1, DIFF_HEAD_DIM), lambda b, h, qi: (0, 0))
    outs = pl.pallas_call(
        kern,
        grid=(batch, heads, nq),
        in_specs=[
            pl.BlockSpec((tq, hw), lambda b, h, qi: (b * nq + qi, h)),
            pl.BlockSpec((seq_len, hw), lambda b, h, qi: (b, heads + h)),
            pl.BlockSpec((seq_len, hw), lambda b, h, qi: (b, 2 * heads + h)),
            lam_spec, lam_spec, lam_spec, lam_spec,
            pl.BlockSpec((1, hw), lambda b, h, qi: (0, 0)),
        ] + cast_specs,
        out_specs=[pl.BlockSpec((tq, hw), lambda b, h, qi: (b * nq + qi, h))] + cast_specs,
        out_shape=[jax.ShapeDtypeStruct((m, d_model), BF16)]
        + [jax.ShapeDtypeStruct(w.shape, BF16) for w in cast_arrays],
        scratch_shapes=[pltpu.VMEM((2, tq, LANES), F32), pltpu.VMEM((2, tq, LANES), F32),
                        pltpu.VMEM((2, tq, hw), F32)],
        compiler_params=_params("arbitrary", "arbitrary", "arbitrary"),
        name="diff_attention",
    )(qkv, qkv, qkv, vec(lq1), vec(lk1), vec(lq2), vec(lk2), vec(subln_w), *cast_arrays)
    return outs[0], finish(outs[1:])


def _dot_exact(a, b):
    return jnp.dot(a, b, preferred_element_type=F32, precision=lax.Precision.HIGHEST)


def _ssd_kernel(z_ref, x_ref, xh_ref, b_ref, bh_ref, c_ref, ch_ref, dt_ref,
                cwx_ref, cwb_ref, cwc_ref, cbx_ref, cbb_ref, cbc_ref,
                dtb_ref, alog_ref, dexp_ref, nw_ref, o_ref, state_ref, *, heads_per_group):
    g = pl.program_id(1)
    c = pl.program_id(2)
    t = CHUNK
    hpg = heads_per_group
    p = SSM_HEADDIM
    gw = hpg * p
    nh = dt_ref.shape[1]

    @pl.when(c == 0)
    def _():
        state_ref[...] = jnp.zeros(state_ref.shape, F32)

    seq_start = c == 0

    def conv_silu(cur_ref, halo_ref, w_ref, bias_ref):
        halo = halo_ref[...]
        halo = jnp.where(seq_start, jnp.zeros_like(halo), halo)
        ext = jnp.concatenate([halo, cur_ref[...]], axis=0)
        return _silu(_causal_conv_ext(ext, SUBLANES, w_ref[...], bias_ref[...]))

    xc = conv_silu(x_ref, xh_ref, cwx_ref, cbx_ref)
    bc = conv_silu(b_ref, bh_ref, cwb_ref, cbb_ref)
    cc = conv_silu(c_ref, ch_ref, cwc_ref, cbc_ref)

    dtr = dt_ref[...] + dtb_ref[...]
    dtv = jnp.maximum(dtr, 0.0) + jnp.log1p(jnp.exp(-jnp.abs(dtr)))
    a_all = dtv * (-jnp.exp(alog_ref[...]))

    row_t = lax.broadcasted_iota(jnp.int32, (t, t), 0)
    col_t = lax.broadcasted_iota(jnp.int32, (t, t), 1)
    causal = col_t <= row_t
    a_cs_all = _dot_exact(jnp.where(causal, 1.0, 0.0), a_all)
    a_last = a_cs_all[t - 1:t, :]

    row_h = lax.broadcasted_iota(jnp.int32, (nh, nh), 0)
    col_h = lax.broadcasted_iota(jnp.int32, (nh, nh), 1)
    sel = jnp.where((row_h == g * hpg + col_h) & (col_h < hpg), 1.0, 0.0)
    a_cs = _dot_exact(a_cs_all, sel)
    a_cs_t = a_cs.T

    row_e = lax.broadcasted_iota(jnp.int32, (nh, gw), 0) - g * hpg
    col_e = lax.broadcasted_iota(jnp.int32, (nh, gw), 1)
    expand = jnp.where((col_e >= row_e * p) & (col_e < (row_e + 1) * p), 1.0, 0.0).astype(BF16)
    stacked = jnp.concatenate([dtv, jnp.exp(a_cs_all), jnp.exp(a_last - a_cs_all)], axis=0)
    hi = stacked.astype(BF16)
    lo = (stacked - hi.astype(F32)).astype(BF16)
    expanded = jnp.dot(jnp.concatenate([hi, lo], axis=1), jnp.concatenate([expand, expand], axis=0),
                       preferred_element_type=F32)
    dt_e = expanded[0:t]
    exp_acs_e = expanded[t:2 * t]
    dec_e = expanded[2 * t:3 * t]

    xdt = xc * dt_e
    xdt_b = xdt.astype(BF16)
    bb = bc.astype(BF16)
    cb = cc.astype(BF16)
    cbm = lax.dot_general(cb, bb, (((1,), (1,)), ((), ())), preferred_element_type=F32)

    state = state_ref[...]
    y = jnp.dot(cb, state.astype(BF16), preferred_element_type=F32) * exp_acs_e

    lane = lax.broadcasted_iota(jnp.int32, (t, 2 * p), 1)
    pieces = []
    for pair in range(hpg // 2):
        x_pair = xdt_b[:, pair * 2 * p:(pair + 1) * 2 * p]
        outs = []
        for hh in (2 * pair, 2 * pair + 1):
            seg = a_cs[:, hh:hh + 1] - a_cs_t[hh:hh + 1, :]
            decay = jnp.exp(jnp.where(causal, seg, -jnp.inf))
            outs.append(jnp.dot((cbm * decay).astype(BF16), x_pair, preferred_element_type=F32))
        pieces.append(jnp.where(lane < p, outs[0], outs[1]))
    y = y + jnp.concatenate(pieces, axis=1) + dexp_ref[...] * xc

    new_contrib = jnp.dot(bc.T.astype(BF16), (xdt * dec_e).astype(BF16), preferred_element_type=F32)
    state_ref[...] = state * exp_acs_e[t - 1:t, :] + new_contrib

    gn = y * _silu(z_ref[...])
    gn = gn * lax.rsqrt(jnp.mean(gn * gn, axis=-1, keepdims=True) + EPS)
    o_ref[...] = (gn * nw_ref[...]).astype(o_ref.dtype)


def ssd_mixer(zx, bcm, dt, conv_w, conv_b, dt_bias, a_log, d_skip, norm_w, batch, seq_len, cast_weights=()):
    m = zx.shape[0]
    d_inner = zx.shape[1] // 2
    nh = dt.shape[1]
    hpg = nh // SSM_GROUPS
    gw = d_inner // SSM_GROUPS
    n = SSM_STATE
    t = CHUNK
    nc = seq_len // t
    hb = t // SUBLANES
    ng = SSM_GROUPS
    xb0 = d_inner // gw
    cwb0 = d_inner // n

    def row(b, g, c):
        return b * nc + c

    def halo(b, g, c):
        return jnp.maximum((b * nc + c) * hb - 1, 0)

    d_exp = jnp.repeat(d_skip.astype(F32), SSM_HEADDIM).reshape(1, d_inner)
    conv_b2 = conv_b.reshape(1, -1)
    def lin(b, g, c):
        return (b * ng + g) * nc + c

    cast_arrays, cast_specs, spans, finish = _plan_casts(cast_weights, batch * ng * nc, lin)
    n_cast = len(cast_arrays)
    n_main = 18

    def kern(*refs):
        main_in, cast_in = refs[:n_main], refs[n_main:n_main + n_cast]
        o_ref = refs[n_main + n_cast]
        cast_out = refs[n_main + n_cast + 1:n_main + 2 * n_cast + 1]
        scratch = refs[n_main + 2 * n_cast + 1:]
        _run_casts(lin(pl.program_id(0), pl.program_id(1), pl.program_id(2)), spans, cast_in, cast_out)
        _ssd_kernel(*main_in, o_ref, *scratch, heads_per_group=hpg)

    outs = pl.pallas_call(
        kern,
        grid=(batch, ng, nc),
        in_specs=[
            pl.BlockSpec((t, gw), lambda b, g, c: (row(b, g, c), g)),
            pl.BlockSpec((t, gw), lambda b, g, c: (row(b, g, c), xb0 + g)),
            pl.BlockSpec((SUBLANES, gw), lambda b, g, c: (halo(b, g, c), xb0 + g)),
            pl.BlockSpec((t, n), lambda b, g, c: (row(b, g, c), g)),
            pl.BlockSpec((SUBLANES, n), lambda b, g, c: (halo(b, g, c), g)),
            pl.BlockSpec((t, n), lambda b, g, c: (row(b, g, c), ng + g)),
            pl.BlockSpec((SUBLANES, n), lambda b, g, c: (halo(b, g, c), ng + g)),
            pl.BlockSpec((t, nh), lambda b, g, c: (row(b, g, c), 0)),
            pl.BlockSpec((SSM_CONV, gw), lambda b, g, c: (0, g)),
            pl.BlockSpec((SSM_CONV, n), lambda b, g, c: (0, cwb0 + g)),
            pl.BlockSpec((SSM_CONV, n), lambda b, g, c: (0, cwb0 + ng + g)),
            pl.BlockSpec((1, gw), lambda b, g, c: (0, g)),
            pl.BlockSpec((1, n), lambda b, g, c: (0, cwb0 + g)),
            pl.BlockSpec((1, n), lambda b, g, c: (0, cwb0 + ng + g)),
            pl.BlockSpec((1, nh), lambda b, g, c: (0, 0)),
            pl.BlockSpec((1, nh), lambda b, g, c: (0, 0)),
            pl.BlockSpec((1, gw), lambda b, g, c: (0, g)),
            pl.BlockSpec((1, gw), lambda b, g, c: (0, g)),
        ] + cast_specs,
        out_specs=[pl.BlockSpec((t, gw), lambda b, g, c: (row(b, g, c), g))] + cast_specs,
        out_shape=[jax.ShapeDtypeStruct((m, d_inner), BF16)]
        + [jax.ShapeDtypeStruct(w.shape, BF16) for w in cast_arrays],
        scratch_shapes=[pltpu.VMEM((n, gw), F32)],
        compiler_params=_params("arbitrary", "arbitrary", "arbitrary"),
        name="ssd_mixer",
    )(zx, zx, zx, bcm, bcm, bcm, bcm, dt, conv_w, conv_w, conv_w, conv_b2, conv_b2, conv_b2,
      dt_bias.reshape(1, nh), a_log.reshape(1, nh), d_exp, norm_w.reshape(1, d_inner), *cast_arrays)
    return outs[0], finish(outs[1:])


def mamba_layer(xf, u, w_in, conv_w, conv_b, dt_bias, a_log, d_skip, norm_w, w_out, batch, seq_len,
                cast_weights=()):
    nh = dt_bias.shape[0]
    d_inner = norm_w.shape[0]
    bc_cols = 2 * SSM_GROUPS * SSM_STATE
    w_in = w_in.astype(BF16)
    zx = matmul(u, w_in, col0=0, n=2 * d_inner, tm=1024, tn=1024)
    bcm = matmul(u, w_in, col0=2 * d_inner, n=bc_cols, tm=1024, tn=1024)
    dt = matmul(u, w_in, col0=2 * d_inner + bc_cols, n=nh, tm=1024, tn=nh)
    yn, casted = ssd_mixer(zx, bcm, dt, conv_w, conv_b, dt_bias, a_log, d_skip, norm_w, batch, seq_len,
                           cast_weights=(w_out,) + tuple(cast_weights))
    return matmul(yn, casted[0], tm=1024, tn=512, residual=xf, single_buffer_a=True), casted[1:]


def attention_layer(xf, u, cos, sin, w_qkv, lq1, lk1, lq2, lk2, subln_w, w_o, lambda_init, batch, seq_len,
                    cast_weights=()):
    d_model = w_o.shape[0]
    tn = _pick(d_model, 1024)
    nb = w_qkv.shape[1] // tn
    qkv = pipelined_matmul(u, w_qkv.astype(BF16), [(tn, 0)],
                           epilogue=functools.partial(_epilogue_rope, rope_tiles=2 * d_model // tn),
                           row_aux=(cos, sin), out_cols=tn, out_dtype=BF16, nb=nb,
                           single_buffer_a=False, name="qkv_rope")
    o, casted = diff_attention(qkv, lq1, lk1, lq2, lk2, subln_w, lambda_init, batch, seq_len,
                               cast_weights=cast_weights)
    return matmul(o, w_o.astype(BF16), tm=1024, tn=1024, residual=xf), casted


def ffn_layer(xf, u, w_up, conv_w, conv_b, w_down, seq_len, variant):
    d_ff = w_up.shape[1] // 2
    tn = _pick(d_ff, 256)
    nb = d_ff // tn
    cw = conv_w.astype(F32).reshape(FFN_CONV, 2, nb, tn).transpose(2, 0, 1, 3).reshape(nb, FFN_CONV, 2 * tn)
    cb = conv_b.astype(F32).reshape(2, nb, tn).transpose(1, 0, 2).reshape(nb, 1, 2 * tn)
    g = pipelined_matmul(u, w_up.astype(BF16), [(tn, 0), (tn, nb)],
                         epilogue=functools.partial(_epilogue_conv_gate, aligned=True),
                         col_aux=(cw, cb), out_cols=tn, out_dtype=BF16, nb=nb, halo_seq_len=seq_len,
                         tm=variant["tm"], lagged=variant["lagged"], single_buffer_a=True,
                         name=variant["name"])
    return matmul(g, w_down.astype(BF16), tm=1024, tn=256, residual=xf, single_buffer_a=True)


FFN_VARIANTS = (
    dict(name="ffn_up_lag_tm2048", lagged=True, tm=2048),
    dict(name="ffn_up_nolag_tm2048", lagged=False, tm=2048),
    dict(name="ffn_up_lag_tm2048", lagged=True, tm=2048),
    dict(name="ffn_up_nolag_tm1024", lagged=False, tm=1024),
)


def kernel(x, positions, l0_norm_mix, l0_m_w_in, l0_m_conv_w, l0_m_conv_b, l0_m_dt_bias, l0_m_a_log, l0_m_d, l0_m_norm, l0_m_w_out, l0_norm_ffn, l0_f_w_up, l0_f_conv_w, l0_f_conv_b, l0_f_w_down, l1_norm_mix, l1_a_w_qkv, l1_a_lq1, l1_a_lk1, l1_a_lq2, l1_a_lk2, l1_a_subln, l1_a_w_o, l1_norm_ffn, l1_f_w_up, l1_f_conv_w, l1_f_conv_b, l1_f_w_down, l2_norm_mix, l2_m_w_in, l2_m_conv_w, l2_m_conv_b, l2_m_dt_bias, l2_m_a_log, l2_m_d, l2_m_norm, l2_m_w_out, l2_norm_ffn, l2_f_w_up, l2_f_conv_w, l2_f_conv_b, l2_f_w_down, l3_norm_mix, l3_a_w_qkv, l3_a_lq1, l3_a_lk1, l3_a_lq2, l3_a_lk2, l3_a_subln, l3_a_w_o, l3_norm_ffn, l3_f_w_up, l3_f_conv_w, l3_f_conv_b, l3_f_w_down, final_norm):
    norm_mix = [l0_norm_mix, l1_norm_mix, l2_norm_mix, l3_norm_mix]
    norm_ffn = [l0_norm_ffn, l1_norm_ffn, l2_norm_ffn, l3_norm_ffn]
    mixer_params = [
        (l0_m_w_in, l0_m_conv_w, l0_m_conv_b, l0_m_dt_bias, l0_m_a_log, l0_m_d, l0_m_norm, l0_m_w_out),
        (l1_a_w_qkv, l1_a_lq1, l1_a_lk1, l1_a_lq2, l1_a_lk2, l1_a_subln, l1_a_w_o),
        (l2_m_w_in, l2_m_conv_w, l2_m_conv_b, l2_m_dt_bias, l2_m_a_log, l2_m_d, l2_m_norm, l2_m_w_out),
        (l3_a_w_qkv, l3_a_lq1, l3_a_lk1, l3_a_lq2, l3_a_lk2, l3_a_subln, l3_a_w_o),
    ]
    ffn_params = [
        (l0_f_w_up, l0_f_conv_w, l0_f_conv_b, l0_f_w_down),
        (l1_f_w_up, l1_f_conv_w, l1_f_conv_b, l1_f_w_down),
        (l2_f_w_up, l2_f_conv_w, l2_f_conv_b, l2_f_w_down),
        (l3_f_w_up, l3_f_conv_w, l3_f_conv_b, l3_f_w_down),
    ]
    batch, seq_len, d_model = x.shape
    xf = x.reshape(batch * seq_len, d_model)
    cos, sin = rope_tables(positions)
    n_layers = len(norm_mix)
    for i in range(n_layers):
        u = rmsnorm(xf, norm_mix[i], BF16)
        w_up, cw, cb, w_down = ffn_params[i]
        nxt = list(mixer_params[i + 1]) if i + 1 < n_layers else []
        nxt_idx = [] if not nxt else ([0] if (i + 1) % N_MIXERS == 0 else [0, len(nxt) - 1])
        cast_weights = [w_up, w_down] + [nxt[k] for k in nxt_idx]
        if i % N_MIXERS == 0:
            xf, casted = mamba_layer(xf, u, *mixer_params[i], batch, seq_len, cast_weights=cast_weights)
        else:
            lambda_init = 0.8 - 0.6 * math.exp(-0.3 * i)
            xf, casted = attention_layer(xf, u, cos, sin, *mixer_params[i], lambda_init, batch, seq_len,
                                         cast_weights=cast_weights)
        ffn_params[i] = (casted[0], cw, cb, casted[1])
        for k, w in zip(nxt_idx, casted[2:]):
            nxt[k] = w
        if nxt:
            mixer_params[i + 1] = tuple(nxt)
        u = rmsnorm(xf, norm_ffn[i], BF16)
        xf = ffn_layer(xf, u, *ffn_params[i], seq_len, FFN_VARIANTS[i])
    return rmsnorm(xf, final_norm, F32).reshape(batch, seq_len, d_model)
```

```python
import functools
import math

import jax
import jax.numpy as jnp
from jax import lax
from jax.experimental import pallas as pl
from jax.experimental.pallas import tpu as pltpu

F32 = jnp.float32
BF16 = jnp.bfloat16

EPS = 1e-5
N_MIXERS = 2

SSM_HEADDIM = 64
SSM_GROUPS = 8
SSM_STATE = 128
SSM_CONV = 4
CHUNK = 128

DIFF_HEAD_DIM = 128
ATTN_SCALE = DIFF_HEAD_DIM ** -0.5
Q_PRESCALE = ATTN_SCALE * math.log2(math.e)
ROPE_THETA = 10000.0

FFN_CONV = 3

V7X_VMEM_BYTES = 64 * 1024 * 1024
VMEM_LIMIT_BYTES = V7X_VMEM_BYTES - 8 * 1024 * 1024
SUBLANES = 8
BF16_ROWS = 16
LANES = 128
MXU_COLS = 256
CAST_BLOCK_BYTES = 4 * 1024 * 1024
EPILOGUE_VREGS = 32


def _params(*sem):
    return pltpu.CompilerParams(dimension_semantics=sem, vmem_limit_bytes=VMEM_LIMIT_BYTES)


def _pick(dim, pref):
    if dim <= pref:
        return dim
    t = pref
    while dim % t:
        t //= 2
    return t


def _silu(x):
    return x / (1.0 + jnp.exp(-x))


def _rmsnorm_kernel(x_ref, w_ref, o_ref):
    x = x_ref[...]
    y = x * lax.rsqrt(jnp.mean(x * x, axis=-1, keepdims=True) + EPS)
    o_ref[...] = (y * w_ref[...]).astype(o_ref.dtype)


def rmsnorm(x, w, out_dtype):
    m, d = x.shape
    tm = _pick(m, 256)
    return pl.pallas_call(
        _rmsnorm_kernel,
        grid=(m // tm,),
        in_specs=[pl.BlockSpec((tm, d), lambda i: (i, 0)),
                  pl.BlockSpec((1, d), lambda i: (0, 0))],
        out_specs=pl.BlockSpec((tm, d), lambda i: (i, 0)),
        out_shape=jax.ShapeDtypeStruct((m, d), out_dtype),
        compiler_params=_params("arbitrary"),
        name="rmsnorm",
    )(x, w.reshape(1, d))


def _mm_kernel(*refs, has_residual, n_cast):
    n_in = 3 if has_residual else 2
    a_ref, w_ref = refs[0], refs[1]
    cast_in = refs[n_in:n_in + n_cast]
    o_ref = refs[n_in + n_cast]
    cast_out = refs[n_in + n_cast + 1:]
    for src, dst in zip(cast_in, cast_out):
        dst[...] = src[...].astype(dst.dtype)
    acc = jnp.dot(a_ref[...], w_ref[...], preferred_element_type=F32)
    if has_residual:
        acc = refs[2][...] + acc
    o_ref[...] = acc.astype(o_ref.dtype)


def matmul(a, w, *, col0=0, n=None, tm, tn, out_dtype=F32, residual=None, single_buffer_a=False,
           cast_weights=None):
    want_casts = cast_weights is not None
    cast_weights = cast_weights or ()
    m, k = a.shape
    n = w.shape[1] - col0 if n is None else n
    tm = _pick(m, tm)
    tn = _pick(n, tn)
    assert col0 % tn == 0
    jb = col0 // tn
    nj = n // tn
    steps = (m // tm) * nj
    a_kwargs = dict(pipeline_mode=pl.Buffered(1)) if single_buffer_a else {}
    in_specs = [pl.BlockSpec((tm, k), lambda i, j: (i, 0), **a_kwargs),
                pl.BlockSpec((k, tn), lambda i, j: (0, j + jb))]
    args = [a, w]
    if residual is not None:
        in_specs.append(pl.BlockSpec((tm, tn), lambda i, j: (i, j)))
        args.append(residual)
    in_kernel = [c.shape[0] % (steps * BF16_ROWS) == 0
                 and (c.shape[0] // steps) * c.shape[1] * 4 <= CAST_BLOCK_BYTES for c in cast_weights]
    cast_arrays = [c for c, ok in zip(cast_weights, in_kernel) if ok]
    cast_specs = [pl.BlockSpec((c.shape[0] // steps, c.shape[1]), lambda i, j: (i * nj + j, 0))
                  for c in cast_arrays]
    outs = pl.pallas_call(
        functools.partial(_mm_kernel, has_residual=residual is not None, n_cast=len(cast_arrays)),
        grid=(m // tm, nj),
        in_specs=in_specs + cast_specs,
        out_specs=[pl.BlockSpec((tm, tn), lambda i, j: (i, j))] + cast_specs,
        out_shape=[jax.ShapeDtypeStruct((m, n), out_dtype)]
        + [jax.ShapeDtypeStruct(c.shape, BF16) for c in cast_arrays],
        compiler_params=_params("arbitrary", "arbitrary"),
        name="matmul",
    )(*args, *cast_arrays)
    if not want_casts:
        return outs[0]
    done = iter(outs[1:])
    return outs[0], [next(done) if ok else c.astype(BF16) for c, ok in zip(cast_weights, in_kernel)]


def _shift_rows(ext, s, halo_rows):
    if s == 0:
        return ext[halo_rows:]
    return pltpu.roll(ext, s, axis=0)[halo_rows:]


def _causal_conv_ext(ext, halo_rows, w, b):
    k_width = w.shape[0]
    y = b + w[0:1] * _shift_rows(ext, k_width - 1, halo_rows)
    for k in range(1, k_width):
        y = y + w[k:k + 1] * _shift_rows(ext, k_width - 1 - k, halo_rows)
    return y


def _causal_conv_ref(ext_ref, halo_rows, row0, rows, w, b):
    k_width = w.shape[0]
    base = halo_rows + row0 - (k_width - 1)
    y = b + w[0:1] * ext_ref[pl.ds(base, rows), :]
    for k in range(1, k_width):
        y = y + w[k:k + 1] * ext_ref[pl.ds(base + k, rows), :]
    return y


def _row_chunks(rows, cols):
    rc = max(BF16_ROWS, EPILOGUE_VREGS * SUBLANES * LANES // cols)
    rc = min(rc, rows)
    return [(r0, rc) for r0 in range(0, rows, rc)]


def _epilogue_conv_gate(h_ref, halo_rows, aux, o_ref, jp, *, aligned):
    cw, cb = aux[0][...], aux[1][...]
    tn = o_ref.shape[1]

    def chunk(r0, rc):
        if aligned:
            ext = h_ref[pl.ds(halo_rows + r0 - SUBLANES, rc + SUBLANES), :]
            y = _causal_conv_ext(ext, SUBLANES, cw, cb)
        else:
            y = _causal_conv_ref(h_ref, halo_rows, r0, rc, cw, cb)
        o_ref[r0:r0 + rc, :] = (_silu(y[:, :tn]) * y[:, tn:]).astype(o_ref.dtype)

    return [functools.partial(chunk, r0, rc) for r0, rc in _row_chunks(o_ref.shape[0], 2 * tn)]


def _epilogue_rope(h_ref, halo_rows, aux, o_ref, jp, *, rope_tiles):
    cos_ref, sin_ref = aux
    d = DIFF_HEAD_DIM
    is_rope = jp < rope_tiles
    scale = jnp.where(jp < rope_tiles // 2, Q_PRESCALE, 1.0)

    def chunk(r0, rc):
        cos = jnp.where(is_rope, cos_ref[r0:r0 + rc, :] * scale, 1.0)
        sin = jnp.where(is_rope, sin_ref[r0:r0 + rc, :] * scale, 0.0)
        for c in range(o_ref.shape[1] // d):
            t = h_ref[r0:r0 + rc, c * d:(c + 1) * d]
            rot = pltpu.roll(t, d // 2, axis=1)
            o_ref[r0:r0 + rc, c * d:(c + 1) * d] = (t * cos + rot * sin).astype(o_ref.dtype)

    return [functools.partial(chunk, r0, rc) for r0, rc in _row_chunks(o_ref.shape[0], 4 * d)]


def _pipelined_kernel(*refs, epilogue, n_w, n_aux, use_halo, lagged, nb, n_tiles, tiles_per_seq):
    it = iter(refs)
    a_ref = next(it)
    ah_ref = next(it) if use_halo else None
    w_refs = [next(it) for _ in range(n_w)]
    aux = [next(it) for _ in range(n_aux)]
    o_ref = next(it)
    aext_ref = next(it) if use_halo else None
    h = [next(it) for _ in range(2 if lagged else 1)]
    halo_rows = BF16_ROWS if use_halo else 0

    s = pl.program_id(0)
    t = jnp.minimum(s, n_tiles - 1)
    i = t // nb
    j = t % nb

    if use_halo:
        @pl.when((j == 0) & (s < n_tiles))
        def _():
            seq_start = (i % tiles_per_seq) == 0
            halo = ah_ref[...]
            aext_ref[0:BF16_ROWS, :] = jnp.where(seq_start, jnp.zeros_like(halo), halo)
            aext_ref[BF16_ROWS:, :] = a_ref[...]

    def matmul_into(h_ref):
        lhs = aext_ref[...] if use_halo else a_ref[...]
        col = 0
        for w_ref in w_refs:
            wn = w_ref.shape[1]
            step_cols = min(wn, MXU_COLS)
            for c0 in range(0, wn, step_cols):
                h_ref[:, col + c0:col + c0 + step_cols] = jnp.dot(
                    lhs, w_ref[:, c0:c0 + step_cols], preferred_element_type=F32)
            col += wn

    if not lagged:
        matmul_into(h[0])
        for chunk in epilogue(h[0], halo_rows, aux, o_ref, j):
            chunk()
        return

    jp = jnp.maximum(s - 1, 0) % nb
    slot = s % 2

    @pl.when(s == 0)
    def _():
        h[1][...] = jnp.zeros(h[1].shape, F32)

    def step(cur, prev):
        for chunk in epilogue(h[prev], halo_rows, aux, o_ref, jp):
            chunk()
        matmul_into(h[cur])

    @pl.when(slot == 0)
    def _():
        step(0, 1)

    @pl.when(slot == 1)
    def _():
        step(1, 0)


def pipelined_matmul(a, w, w_col_blocks, *, epilogue, col_aux=(), row_aux=(), out_cols, out_dtype,
                     nb, halo_seq_len=None, tm=1024, single_buffer_a=True, lagged=True, name):
    m, k = a.shape
    use_halo = halo_seq_len is not None
    tm = _pick(halo_seq_len if use_halo else m, tm)
    hb = tm // BF16_ROWS
    n_tiles = (m // tm) * nb
    hrows = tm + (BF16_ROWS if use_halo else 0)
    wn = sum(width for width, _ in w_col_blocks)

    def cur(s):
        t = jnp.minimum(s, n_tiles - 1)
        return t // nb, t % nb

    def prev(s):
        t = jnp.maximum(s - 1, 0) if lagged else s
        return t // nb, t % nb

    a_kwargs = dict(pipeline_mode=pl.Buffered(1)) if single_buffer_a else {}
    in_specs = [pl.BlockSpec((tm, k), lambda s: (cur(s)[0], 0), **a_kwargs)]
    args = [a]
    if use_halo:
        in_specs.append(pl.BlockSpec((BF16_ROWS, k), lambda s: (jnp.maximum(cur(s)[0] * hb - 1, 0), 0)))
        args.append(a)
    for width, first in w_col_blocks:
        in_specs.append(pl.BlockSpec((k, width), lambda s, first=first: (0, first + cur(s)[1])))
        args.append(w)
    for arr in col_aux:
        in_specs.append(pl.BlockSpec((None,) + arr.shape[1:], lambda s: (prev(s)[1], 0, 0)))
        args.append(arr)
    for arr in row_aux:
        in_specs.append(pl.BlockSpec((tm, arr.shape[1]), lambda s: (prev(s)[0], 0)))
        args.append(arr)
    scratch = ([pltpu.VMEM((hrows, k), BF16)] if use_halo else [])
    scratch += [pltpu.VMEM((hrows, wn), F32)] * (2 if lagged else 1)
    kern = functools.partial(
        _pipelined_kernel, epilogue=epilogue, n_w=len(w_col_blocks), n_aux=len(col_aux) + len(row_aux),
        use_halo=use_halo, lagged=lagged, nb=nb, n_tiles=n_tiles,
        tiles_per_seq=(halo_seq_len // tm) if use_halo else 1)
    return pl.pallas_call(
        kern,
        grid=(n_tiles + (1 if lagged else 0),),
        in_specs=in_specs,
        out_specs=pl.BlockSpec((tm, out_cols), lambda s: prev(s)),
        out_shape=jax.ShapeDtypeStruct((m, nb * out_cols), out_dtype),
        scratch_shapes=scratch,
        compiler_params=_params("arbitrary"),
        name=name,
    )(*args)


def _rope_table_kernel(pos_ref, inv_ref, cos_ref, sin_ref):
    freqs = pos_ref[...].astype(F32) * inv_ref[...]
    lane = lax.broadcasted_iota(jnp.int32, freqs.shape, 1)
    sign = jnp.where(lane < DIFF_HEAD_DIM // 2, -1.0, 1.0)
    cos_ref[...] = jnp.cos(freqs)
    sin_ref[...] = jnp.sin(freqs) * sign


def rope_tables(positions):
    m = positions.size
    inv = 1.0 / (ROPE_THETA ** (jnp.arange(0, DIFF_HEAD_DIM, 2, dtype=F32) / DIFF_HEAD_DIM))
    inv_full = jnp.concatenate([inv, inv]).reshape(1, DIFF_HEAD_DIM)
    tm = _pick(m, 1024)
    shape = jax.ShapeDtypeStruct((m, DIFF_HEAD_DIM), F32)
    return pl.pallas_call(
        _rope_table_kernel,
        grid=(m // tm,),
        in_specs=[pl.BlockSpec((tm, 1), lambda i: (i, 0)),
                  pl.BlockSpec((1, DIFF_HEAD_DIM), lambda i: (0, 0))],
        out_specs=[pl.BlockSpec((tm, DIFF_HEAD_DIM), lambda i: (i, 0))] * 2,
        out_shape=[shape, shape],
        compiler_params=_params("arbitrary"),
        name="rope_tables",
    )(positions.reshape(m, 1), inv_full)


def _diff_attn_kernel(q_ref, k_ref, v_ref, lq1_ref, lk1_ref, lq2_ref, lk2_ref, sw_ref, o_ref,
                      m_ref, l_ref, acc_ref, *, lambda_init, tq):
    qi = pl.program_id(2)
    d = DIFF_HEAD_DIM
    q = q_ref[...]

    m_ref[...] = jnp.full(m_ref.shape, -jnp.inf, F32)
    l_ref[...] = jnp.zeros(l_ref.shape, F32)
    acc_ref[...] = jnp.zeros(acc_ref.shape, F32)

    def kv_block(kb, masked):
        start = pl.multiple_of(kb * tq, tq)
        k = k_ref[pl.ds(start, tq), :]
        v = v_ref[pl.ds(start, tq), :]
        scores = [lax.dot_general(q[:, c * d:(c + 1) * d], k[:, c * d:(c + 1) * d],
                                  (((1,), (1,)), ((), ())), preferred_element_type=F32) for c in range(2)]
        probs = []
        alphas = []
        for c in range(2):
            s = scores[c]
            if masked:
                row = lax.broadcasted_iota(jnp.int32, s.shape, 0)
                col = lax.broadcasted_iota(jnp.int32, s.shape, 1)
                s = jnp.where(col <= row, s, -jnp.inf)
            chunks = [s[:, t * LANES:(t + 1) * LANES] for t in range(tq // LANES)]
            cmax = functools.reduce(jnp.maximum, chunks)
            m_old = m_ref[c]
            m_new = jnp.maximum(m_old, jnp.max(cmax, axis=-1, keepdims=True))
            alpha = jnp.exp2(m_old - m_new)
            ps = [jnp.exp2(ch - m_new) for ch in chunks]
            l_ref[c] = alpha * l_ref[c] + functools.reduce(jnp.add, ps)
            m_ref[c] = m_new
            probs.append(jnp.concatenate(ps, axis=1).astype(v.dtype))
            alphas.append(alpha)
        pv = jnp.dot(jnp.concatenate(probs, axis=0), v, preferred_element_type=F32)
        for c in range(2):
            a2 = jnp.concatenate([alphas[c], alphas[c]], axis=1)
            acc_ref[c] = a2 * acc_ref[c] + pv[c * tq:(c + 1) * tq]

    def body(kb, carry):
        kv_block(kb, False)
        return carry

    lax.fori_loop(0, qi, body, 0)
    kv_block(qi, True)

    lam = (jnp.exp(jnp.sum(lq1_ref[...] * lk1_ref[...], axis=-1, keepdims=True))
           - jnp.exp(jnp.sum(lq2_ref[...] * lk2_ref[...], axis=-1, keepdims=True)) + lambda_init)
    l0 = jnp.sum(l_ref[0], axis=-1, keepdims=True)
    l1 = jnp.sum(l_ref[1], axis=-1, keepdims=True)
    o = acc_ref[0] / l0 - lam * (acc_ref[1] / l1)
    o = o * lax.rsqrt(jnp.mean(o * o, axis=-1, keepdims=True) + EPS)
    o_ref[...] = ((o * sw_ref[...]) * (1.0 - lambda_init)).astype(o_ref.dtype)


def _cast_rows(w, steps):
    k, n = w.shape
    for rows in range(BF16_ROWS, k + 1, BF16_ROWS):
        if k % rows == 0 and k // rows <= steps:
            return rows if rows * n * 4 <= CAST_BLOCK_BYTES else None
    return None


def _plan_casts(weights, steps, lin):
    rows = [_cast_rows(w, steps) for w in weights]
    arrays, specs, spans = [], [], []
    at_end = False
    for w, r in zip(weights, rows):
        if not r:
            continue
        nb = w.shape[0] // r
        first = steps - nb if at_end else 0
        at_end = not at_end if nb < steps else at_end
        arrays.append(w)
        spans.append((first, nb))
        specs.append(pl.BlockSpec(
            (r, w.shape[1]),
            lambda *ids, first=first, nb=nb: (jnp.clip(lin(*ids) - first, 0, nb - 1), 0)))

    def finish(outputs):
        done = iter(outputs)
        return [next(done) if r else w.astype(BF16) for w, r in zip(weights, rows)]

    return arrays, specs, spans, finish


def _run_casts(step, spans, cast_in, cast_out):
    for (first, nb), src, dst in zip(spans, cast_in, cast_out):
        @pl.when((step >= first) & (step < first + nb))
        def _():
            dst[...] = src[...].astype(dst.dtype)


def diff_attention(qkv, lq1, lk1, lq2, lk2, subln_w, lambda_init, batch, seq_len, *, tq=512,
                   cast_weights=()):
    m, n = qkv.shape
    d_model = n // 3
    hw = 2 * DIFF_HEAD_DIM
    heads = d_model // hw
    tq = _pick(seq_len, tq)
    nq = seq_len // tq

    def lin(b, h, qi):
        return (b * heads + h) * nq + qi

    cast_arrays, cast_specs, spans, finish = _plan_casts(cast_weights, batch * heads * nq, lin)
    n_cast = len(cast_arrays)

    def kern(*refs):
        main_in, cast_in = refs[:8], refs[8:8 + n_cast]
        o_ref = refs[8 + n_cast]
        cast_out = refs[9 + n_cast:9 + 2 * n_cast]
        scratch = refs[9 + 2 * n_cast:]
        _run_casts(lin(pl.program_id(0), pl.program_id(1), pl.program_id(2)), spans, cast_in, cast_out)
        _diff_attn_kernel(*main_in, o_ref, *scratch, lambda_init=lambda_init, tq=tq)

    vec = lambda a: a.reshape(1, -1)
    lam_spec = pl.BlockSpec((1, DIFF_HEAD_DIM), lambda b, h, qi: (0, 0))
    outs = pl.pallas_call(
        kern,
        grid=(batch, heads, nq),
        in_specs=[
            pl.BlockSpec((tq, hw), lambda b, h, qi: (b * nq + qi, h)),
            pl.BlockSpec((seq_len, hw), lambda b, h, qi: (b, heads + h)),
            pl.BlockSpec((seq_len, hw), lambda b, h, qi: (b, 2 * heads + h)),
            lam_spec, lam_spec, lam_spec, lam_spec,
            pl.BlockSpec((1, hw), lambda b, h, qi: (0, 0)),
        ] + cast_specs,
        out_specs=[pl.BlockSpec((tq, hw), lambda b, h, qi: (b * nq + qi, h))] + cast_specs,
        out_shape=[jax.ShapeDtypeStruct((m, d_model), BF16)]
        + [jax.ShapeDtypeStruct(w.shape, BF16) for w in cast_arrays],
        scratch_shapes=[pltpu.VMEM((2, tq, LANES), F32), pltpu.VMEM((2, tq, LANES), F32),
                        pltpu.VMEM((2, tq, hw), F32)],
        compiler_params=_params("arbitrary", "arbitrary", "arbitrary"),
        name="diff_attention",
    )(qkv, qkv, qkv, vec(lq1), vec(lk1), vec(lq2), vec(lk2), vec(subln_w), *cast_arrays)
    return outs[0], finish(outs[1:])


def _dot_exact(a, b):
    return jnp.dot(a, b, preferred_element_type=F32, precision=lax.Precision.HIGHEST)


def _ssd_kernel(z_ref, x_ref, xh_ref, b_ref, bh_ref, c_ref, ch_ref, dt_ref,
                cwx_ref, cwb_ref, cwc_ref, cbx_ref, cbb_ref, cbc_ref,
                dtb_ref, alog_ref, dexp_ref, nw_ref, o_ref, state_ref, *, heads_per_group):
    g = pl.program_id(1)
    c = pl.program_id(2)
    t = CHUNK
    hpg = heads_per_group
    p = SSM_HEADDIM
    gw = hpg * p
    nh = dt_ref.shape[1]

    @pl.when(c == 0)
    def _():
        state_ref[...] = jnp.zeros(state_ref.shape, F32)

    seq_start = c == 0

    def conv_silu(cur_ref, halo_ref, w_ref, bias_ref):
        halo = halo_ref[...]
        halo = jnp.where(seq_start, jnp.zeros_like(halo), halo)
        ext = jnp.concatenate([halo, cur_ref[...]], axis=0)
        return _silu(_causal_conv_ext(ext, SUBLANES, w_ref[...], bias_ref[...]))

    xc = conv_silu(x_ref, xh_ref, cwx_ref, cbx_ref)
    bc = conv_silu(b_ref, bh_ref, cwb_ref, cbb_ref)
    cc = conv_silu(c_ref, ch_ref, cwc_ref, cbc_ref)

    dtr = dt_ref[...] + dtb_ref[...]
    dtv = jnp.maximum(dtr, 0.0) + jnp.log1p(jnp.exp(-jnp.abs(dtr)))
    a_all = dtv * (-jnp.exp(alog_ref[...]))

    row_t = lax.broadcasted_iota(jnp.int32, (t, t), 0)
    col_t = lax.broadcasted_iota(jnp.int32, (t, t), 1)
    causal = col_t <= row_t
    a_cs_all = _dot_exact(jnp.where(causal, 1.0, 0.0), a_all)
    a_last = a_cs_all[t - 1:t, :]

    row_h = lax.broadcasted_iota(jnp.int32, (nh, nh), 0)
    col_h = lax.broadcasted_iota(jnp.int32, (nh, nh), 1)
    sel = jnp.where((row_h == g * hpg + col_h) & (col_h < hpg), 1.0, 0.0)
    a_cs = _dot_exact(a_cs_all, sel)
    a_cs_t = a_cs.T

    row_e = lax.broadcasted_iota(jnp.int32, (nh, gw), 0) - g * hpg
    col_e = lax.broadcasted_iota(jnp.int32, (nh, gw), 1)
    expand = jnp.where((col_e >= row_e * p) & (col_e < (row_e + 1) * p), 1.0, 0.0).astype(BF16)
    stacked = jnp.concatenate([dtv, jnp.exp(a_cs_all), jnp.exp(a_last - a_cs_all)], axis=0)
    hi = stacked.astype(BF16)
    lo = (stacked - hi.astype(F32)).astype(BF16)
    expanded = jnp.dot(jnp.concatenate([hi, lo], axis=1), jnp.concatenate([expand, expand], axis=0),
                       preferred_element_type=F32)
    dt_e = expanded[0:t]
    exp_acs_e = expanded[t:2 * t]
    dec_e = expanded[2 * t:3 * t]

    xdt = xc * dt_e
    xdt_b = xdt.astype(BF16)
    bb = bc.astype(BF16)
    cb = cc.astype(BF16)
    cbm = lax.dot_general(cb, bb, (((1,), (1,)), ((), ())), preferred_element_type=F32)

    state = state_ref[...]
    y = jnp.dot(cb, state.astype(BF16), preferred_element_type=F32) * exp_acs_e

    lane = lax.broadcasted_iota(jnp.int32, (t, 2 * p), 1)
    pieces = []
    for pair in range(hpg // 2):
        x_pair = xdt_b[:, pair * 2 * p:(pair + 1) * 2 * p]
        outs = []
        for hh in (2 * pair, 2 * pair + 1):
            seg = a_cs[:, hh:hh + 1] - a_cs_t[hh:hh + 1, :]
            decay = jnp.exp(jnp.where(causal, seg, -jnp.inf))
            outs.append(jnp.dot((cbm * decay).astype(BF16), x_pair, preferred_element_type=F32))
        pieces.append(jnp.where(lane < p, outs[0], outs[1]))
    y = y + jnp.concatenate(pieces, axis=1) + dexp_ref[...] * xc

    new_contrib = jnp.dot(bc.T.astype(BF16), (xdt * dec_e).astype(BF16), preferred_element_type=F32)
    state_ref[...] = state * exp_acs_e[t - 1:t, :] + new_contrib

    gn = y * _silu(z_ref[...])
    gn = gn * lax.rsqrt(jnp.mean(gn * gn, axis=-1, keepdims=True) + EPS)
    o_ref[...] = (gn * nw_ref[...]).astype(o_ref.dtype)


def ssd_mixer(zx, bcm, dt, conv_w, conv_b, dt_bias, a_log, d_skip, norm_w, batch, seq_len, cast_weights=()):
    m = zx.shape[0]
    d_inner = zx.shape[1] // 2
    nh = dt.shape[1]
    hpg = nh // SSM_GROUPS
    gw = d_inner // SSM_GROUPS
    n = SSM_STATE
    t = CHUNK
    nc = seq_len // t
    hb = t // SUBLANES
    ng = SSM_GROUPS
    xb0 = d_inner // gw
    cwb0 = d_inner // n

    def row(b, g, c):
        return b * nc + c

    def halo(b, g, c):
        return jnp.maximum((b * nc + c) * hb - 1, 0)

    d_exp = jnp.repeat(d_skip.astype(F32), SSM_HEADDIM).reshape(1, d_inner)
    conv_b2 = conv_b.reshape(1, -1)
    def lin(b, g, c):
        return (b * ng + g) * nc + c

    cast_arrays, cast_specs, spans, finish = _plan_casts(cast_weights, batch * ng * nc, lin)
    n_cast = len(cast_arrays)
    n_main = 18

    def kern(*refs):
        main_in, cast_in = refs[:n_main], refs[n_main:n_main + n_cast]
        o_ref = refs[n_main + n_cast]
        cast_out = refs[n_main + n_cast + 1:n_main + 2 * n_cast + 1]
        scratch = refs[n_main + 2 * n_cast + 1:]
        _run_casts(lin(pl.program_id(0), pl.program_id(1), pl.program_id(2)), spans, cast_in, cast_out)
        _ssd_kernel(*main_in, o_ref, *scratch, heads_per_group=hpg)

    outs = pl.pallas_call(
        kern,
        grid=(batch, ng, nc),
        in_specs=[
            pl.BlockSpec((t, gw), lambda b, g, c: (row(b, g, c), g)),
            pl.BlockSpec((t, gw), lambda b, g, c: (row(b, g, c), xb0 + g)),
            pl.BlockSpec((SUBLANES, gw), lambda b, g, c: (halo(b, g, c), xb0 + g)),
            pl.BlockSpec((t, n), lambda b, g, c: (row(b, g, c), g)),
            pl.BlockSpec((SUBLANES, n), lambda b, g, c: (halo(b, g, c), g)),
            pl.BlockSpec((t, n), lambda b, g, c: (row(b, g, c), ng + g)),
            pl.BlockSpec((SUBLANES, n), lambda b, g, c: (halo(b, g, c), ng + g)),
            pl.BlockSpec((t, nh), lambda b, g, c: (row(b, g, c), 0)),
            pl.BlockSpec((SSM_CONV, gw), lambda b, g, c: (0, g)),
            pl.BlockSpec((SSM_CONV, n), lambda b, g, c: (0, cwb0 + g)),
            pl.BlockSpec((SSM_CONV, n), lambda b, g, c: (0, cwb0 + ng + g)),
            pl.BlockSpec((1, gw), lambda b, g, c: (0, g)),
            pl.BlockSpec((1, n), lambda b, g, c: (0, cwb0 + g)),
            pl.BlockSpec((1, n), lambda b, g, c: (0, cwb0 + ng + g)),
            pl.BlockSpec((1, nh), lambda b, g, c: (0, 0)),
            pl.BlockSpec((1, nh), lambda b, g, c: (0, 0)),
            pl.BlockSpec((1, gw), lambda b, g, c: (0, g)),
            pl.BlockSpec((1, gw), lambda b, g, c: (0, g)),
        ] + cast_specs,
        out_specs=[pl.BlockSpec((t, gw), lambda b, g, c: (row(b, g, c), g))] + cast_specs,
        out_shape=[jax.ShapeDtypeStruct((m, d_inner), BF16)]
        + [jax.ShapeDtypeStruct(w.shape, BF16) for w in cast_arrays],
        scratch_shapes=[pltpu.VMEM((n, gw), F32)],
        compiler_params=_params("arbitrary", "arbitrary", "arbitrary"),
        name="ssd_mixer",
    )(zx, zx, zx, bcm, bcm, bcm, bcm, dt, conv_w, conv_w, conv_w, conv_b2, conv_b2, conv_b2,
      dt_bias.reshape(1, nh), a_log.reshape(1, nh), d_exp, norm_w.reshape(1, d_inner), *cast_arrays)
    return outs[0], finish(outs[1:])


def mamba_layer(xf, u, w_in, conv_w, conv_b, dt_bias, a_log, d_skip, norm_w, w_out, batch, seq_len,
                cast_in_proj=(), cast_ssd=(), cast_out_proj=()):
    nh = dt_bias.shape[0]
    d_inner = norm_w.shape[0]
    bc_cols = 2 * SSM_GROUPS * SSM_STATE
    w_in = w_in.astype(BF16)
    zx, c_in = matmul(u, w_in, col0=0, n=2 * d_inner, tm=1024, tn=1024, cast_weights=list(cast_in_proj))
    bcm = matmul(u, w_in, col0=2 * d_inner, n=bc_cols, tm=1024, tn=1024)
    dt = matmul(u, w_in, col0=2 * d_inner + bc_cols, n=nh, tm=1024, tn=nh)
    yn, c_ssd = ssd_mixer(zx, bcm, dt, conv_w, conv_b, dt_bias, a_log, d_skip, norm_w, batch, seq_len,
                          cast_weights=(w_out,) + tuple(cast_ssd))
    xf, c_out = matmul(yn, c_ssd[0], tm=1024, tn=512, residual=xf, single_buffer_a=True,
                       cast_weights=list(cast_out_proj))
    return xf, c_in, c_ssd[1:], c_out


def attention_layer(xf, u, cos, sin, w_qkv, lq1, lk1, lq2, lk2, subln_w, w_o, lambda_init, batch, seq_len,
                    cast_weights=()):
    d_model = w_o.shape[0]
    tn = _pick(d_model, 1024)
    nb = w_qkv.shape[1] // tn
    qkv = pipelined_matmul(u, w_qkv.astype(BF16), [(tn, 0)],
                           epilogue=functools.partial(_epilogue_rope, rope_tiles=2 * d_model // tn),
                           row_aux=(cos, sin), out_cols=tn, out_dtype=BF16, nb=nb,
                           single_buffer_a=False, name="qkv_rope")
    o, casted = diff_attention(qkv, lq1, lk1, lq2, lk2, subln_w, lambda_init, batch, seq_len,
                               cast_weights=cast_weights)
    return matmul(o, w_o.astype(BF16), tm=1024, tn=1024, residual=xf), casted


def ffn_layer(xf, u, w_up, conv_w, conv_b, w_down, seq_len, cast_down_proj=()):
    d_ff = w_up.shape[1] // 2
    tn = _pick(d_ff, 256)
    nb = d_ff // tn
    cw = conv_w.astype(F32).reshape(FFN_CONV, 2, nb, tn).transpose(2, 0, 1, 3).reshape(nb, FFN_CONV, 2 * tn)
    cb = conv_b.astype(F32).reshape(2, nb, tn).transpose(1, 0, 2).reshape(nb, 1, 2 * tn)
    g = pipelined_matmul(u, w_up.astype(BF16), [(tn, 0), (tn, nb)],
                         epilogue=functools.partial(_epilogue_conv_gate, aligned=True),
                         col_aux=(cw, cb), out_cols=tn, out_dtype=BF16, nb=nb, halo_seq_len=seq_len,
                         tm=2048, lagged=False, single_buffer_a=True, name="ffn_up")
    return matmul(g, w_down.astype(BF16), tm=1024, tn=256, residual=xf, single_buffer_a=True,
                  cast_weights=list(cast_down_proj))


def kernel(x, positions, l0_norm_mix, l0_m_w_in, l0_m_conv_w, l0_m_conv_b, l0_m_dt_bias, l0_m_a_log, l0_m_d, l0_m_norm, l0_m_w_out, l0_norm_ffn, l0_f_w_up, l0_f_conv_w, l0_f_conv_b, l0_f_w_down, l1_norm_mix, l1_a_w_qkv, l1_a_lq1, l1_a_lk1, l1_a_lq2, l1_a_lk2, l1_a_subln, l1_a_w_o, l1_norm_ffn, l1_f_w_up, l1_f_conv_w, l1_f_conv_b, l1_f_w_down, l2_norm_mix, l2_m_w_in, l2_m_conv_w, l2_m_conv_b, l2_m_dt_bias, l2_m_a_log, l2_m_d, l2_m_norm, l2_m_w_out, l2_norm_ffn, l2_f_w_up, l2_f_conv_w, l2_f_conv_b, l2_f_w_down, l3_norm_mix, l3_a_w_qkv, l3_a_lq1, l3_a_lk1, l3_a_lq2, l3_a_lk2, l3_a_subln, l3_a_w_o, l3_norm_ffn, l3_f_w_up, l3_f_conv_w, l3_f_conv_b, l3_f_w_down, final_norm):
    norm_mix = [l0_norm_mix, l1_norm_mix, l2_norm_mix, l3_norm_mix]
    norm_ffn = [l0_norm_ffn, l1_norm_ffn, l2_norm_ffn, l3_norm_ffn]
    mixer_params = [
        (l0_m_w_in, l0_m_conv_w, l0_m_conv_b, l0_m_dt_bias, l0_m_a_log, l0_m_d, l0_m_norm, l0_m_w_out),
        (l1_a_w_qkv, l1_a_lq1, l1_a_lk1, l1_a_lq2, l1_a_lk2, l1_a_subln, l1_a_w_o),
        (l2_m_w_in, l2_m_conv_w, l2_m_conv_b, l2_m_dt_bias, l2_m_a_log, l2_m_d, l2_m_norm, l2_m_w_out),
        (l3_a_w_qkv, l3_a_lq1, l3_a_lk1, l3_a_lq2, l3_a_lk2, l3_a_subln, l3_a_w_o),
    ]
    ffn_params = [
        (l0_f_w_up, l0_f_conv_w, l0_f_conv_b, l0_f_w_down),
        (l1_f_w_up, l1_f_conv_w, l1_f_conv_b, l1_f_w_down),
        (l2_f_w_up, l2_f_conv_w, l2_f_conv_b, l2_f_w_down),
        (l3_f_w_up, l3_f_conv_w, l3_f_conv_b, l3_f_w_down),
    ]
    batch, seq_len, d_model = x.shape
    xf = x.reshape(batch * seq_len, d_model)
    cos, sin = rope_tables(positions)
    n_layers = len(norm_mix)
    for i in range(n_layers):
        u = rmsnorm(xf, norm_mix[i], BF16)
        w_up, cw, cb, w_down = ffn_params[i]
        nxt = list(mixer_params[i + 1]) if i + 1 < n_layers else None
        if i % N_MIXERS == 0:
            xf, (w_up,), (w_down,), c_out = mamba_layer(
                xf, u, *mixer_params[i], batch, seq_len, cast_in_proj=[w_up], cast_ssd=[w_down],
                cast_out_proj=[nxt[0]] if nxt else [])
            if nxt:
                nxt[0] = c_out[0]
            cast_down = [nxt[-1]] if nxt else []
        else:
            lambda_init = 0.8 - 0.6 * math.exp(-0.3 * i)
            xf, (w_up, w_down) = attention_layer(xf, u, cos, sin, *mixer_params[i], lambda_init, batch,
                                                 seq_len, cast_weights=[w_up, w_down])
            cast_down = [nxt[0]] if nxt else []
        u = rmsnorm(xf, norm_ffn[i], BF16)
        xf, c_down = ffn_layer(xf, u, w_up, cw, cb, w_down, seq_len, cast_down_proj=cast_down)
        if nxt:
            nxt[-1 if i % N_MIXERS == 0 else 0] = c_down[0]
            mixer_params[i + 1] = tuple(nxt)
    return rmsnorm(xf, final_norm, F32).reshape(batch, seq_len, d_model)
```

```python
import functools
import math

import jax
import jax.numpy as jnp
from jax import lax
from jax.experimental import pallas as pl
from jax.experimental.pallas import tpu as pltpu

F32 = jnp.float32
BF16 = jnp.bfloat16

EPS = 1e-5
N_MIXERS = 2

SSM_HEADDIM = 64
SSM_GROUPS = 8
SSM_STATE = 128
SSM_CONV = 4
CHUNK = 128

DIFF_HEAD_DIM = 128
ATTN_SCALE = DIFF_HEAD_DIM ** -0.5
Q_PRESCALE = ATTN_SCALE * math.log2(math.e)
ROPE_THETA = 10000.0

FFN_CONV = 3

V7X_VMEM_BYTES = 64 * 1024 * 1024
VMEM_LIMIT_BYTES = V7X_VMEM_BYTES - 8 * 1024 * 1024
SUBLANES = 8
BF16_ROWS = 16
LANES = 128
MXU_COLS = 256
CAST_BLOCK_BYTES = 4 * 1024 * 1024
EPILOGUE_VREGS = 32


def _params(*sem):
    return pltpu.CompilerParams(dimension_semantics=sem, vmem_limit_bytes=VMEM_LIMIT_BYTES)


def _pick(dim, pref):
    if dim <= pref:
        return dim
    t = pref
    while dim % t:
        t //= 2
    return t


def _silu(x):
    return x / (1.0 + jnp.exp(-x))


def _rmsnorm_kernel(x_ref, w_ref, o_ref):
    x = x_ref[...]
    y = x * lax.rsqrt(jnp.mean(x * x, axis=-1, keepdims=True) + EPS)
    o_ref[...] = (y * w_ref[...]).astype(o_ref.dtype)


def rmsnorm(x, w, out_dtype):
    m, d = x.shape
    tm = _pick(m, 256)
    return pl.pallas_call(
        _rmsnorm_kernel,
        grid=(m // tm,),
        in_specs=[pl.BlockSpec((tm, d), lambda i: (i, 0)),
                  pl.BlockSpec((1, d), lambda i: (0, 0))],
        out_specs=pl.BlockSpec((tm, d), lambda i: (i, 0)),
        out_shape=jax.ShapeDtypeStruct((m, d), out_dtype),
        compiler_params=_params("arbitrary"),
        name="rmsnorm",
    )(x, w.reshape(1, d))


def _mm_kernel(*refs, has_residual, n_cast):
    n_in = 3 if has_residual else 2
    a_ref, w_ref = refs[0], refs[1]
    cast_in = refs[n_in:n_in + n_cast]
    o_ref = refs[n_in + n_cast]
    cast_out = refs[n_in + n_cast + 1:]
    for src, dst in zip(cast_in, cast_out):
        dst[...] = src[...].astype(dst.dtype)
    acc = jnp.dot(a_ref[...], w_ref[...], preferred_element_type=F32)
    if has_residual:
        acc = refs[2][...] + acc
    o_ref[...] = acc.astype(o_ref.dtype)


def matmul(a, w, *, col0=0, n=None, tm, tn, out_dtype=F32, residual=None, single_buffer_a=False,
           cast_weights=None):
    want_casts = cast_weights is not None
    cast_weights = cast_weights or ()
    m, k = a.shape
    n = w.shape[1] - col0 if n is None else n
    tm = _pick(m, tm)
    tn = _pick(n, tn)
    assert col0 % tn == 0
    jb = col0 // tn
    nj = n // tn
    steps = (m // tm) * nj
    a_kwargs = dict(pipeline_mode=pl.Buffered(1)) if single_buffer_a else {}
    in_specs = [pl.BlockSpec((tm, k), lambda i, j: (i, 0), **a_kwargs),
                pl.BlockSpec((k, tn), lambda i, j: (0, j + jb))]
    args = [a, w]
    if residual is not None:
        in_specs.append(pl.BlockSpec((tm, tn), lambda i, j: (i, j)))
        args.append(residual)
    in_kernel = [c.shape[0] % (steps * BF16_ROWS) == 0
                 and (c.shape[0] // steps) * c.shape[1] * 4 <= CAST_BLOCK_BYTES for c in cast_weights]
    cast_arrays = [c for c, ok in zip(cast_weights, in_kernel) if ok]
    cast_specs = [pl.BlockSpec((c.shape[0] // steps, c.shape[1]), lambda i, j: (i * nj + j, 0))
                  for c in cast_arrays]
    outs = pl.pallas_call(
        functools.partial(_mm_kernel, has_residual=residual is not None, n_cast=len(cast_arrays)),
        grid=(m // tm, nj),
        in_specs=in_specs + cast_specs,
        out_specs=[pl.BlockSpec((tm, tn), lambda i, j: (i, j))] + cast_specs,
        out_shape=[jax.ShapeDtypeStruct((m, n), out_dtype)]
        + [jax.ShapeDtypeStruct(c.shape, BF16) for c in cast_arrays],
        compiler_params=_params("arbitrary", "arbitrary"),
        name="matmul",
    )(*args, *cast_arrays)
    if not want_casts:
        return outs[0]
    done = iter(outs[1:])
    return outs[0], [next(done) if ok else c.astype(BF16) for c, ok in zip(cast_weights, in_kernel)]


def _shift_rows(ext, s, halo_rows):
    if s == 0:
        return ext[halo_rows:]
    return pltpu.roll(ext, s, axis=0)[halo_rows:]


def _causal_conv_ext(ext, halo_rows, w, b):
    k_width = w.shape[0]
    y = b + w[0:1] * _shift_rows(ext, k_width - 1, halo_rows)
    for k in range(1, k_width):
        y = y + w[k:k + 1] * _shift_rows(ext, k_width - 1 - k, halo_rows)
    return y


def _causal_conv_ref(ext_ref, halo_rows, row0, rows, w, b):
    k_width = w.shape[0]
    base = halo_rows + row0 - (k_width - 1)
    y = b + w[0:1] * ext_ref[pl.ds(base, rows), :]
    for k in range(1, k_width):
        y = y + w[k:k + 1] * ext_ref[pl.ds(base + k, rows), :]
    return y


def _row_chunks(rows, cols):
    rc = max(BF16_ROWS, EPILOGUE_VREGS * SUBLANES * LANES // cols)
    rc = min(rc, rows)
    return [(r0, rc) for r0 in range(0, rows, rc)]


def _epilogue_conv_gate(h_ref, halo_rows, aux, o_ref, jp, *, aligned):
    cw, cb = aux[0][...], aux[1][...]
    tn = o_ref.shape[1]

    def chunk(r0, rc):
        if aligned:
            ext = h_ref[pl.ds(halo_rows + r0 - SUBLANES, rc + SUBLANES), :]
            y = _causal_conv_ext(ext, SUBLANES, cw, cb)
        else:
            y = _causal_conv_ref(h_ref, halo_rows, r0, rc, cw, cb)
        o_ref[r0:r0 + rc, :] = (_silu(y[:, :tn]) * y[:, tn:]).astype(o_ref.dtype)

    return [functools.partial(chunk, r0, rc) for r0, rc in _row_chunks(o_ref.shape[0], 2 * tn)]


def _epilogue_rope(h_ref, halo_rows, aux, o_ref, jp, *, rope_tiles):
    cos_ref, sin_ref = aux
    d = DIFF_HEAD_DIM
    is_rope = jp < rope_tiles
    scale = jnp.where(jp < rope_tiles // 2, Q_PRESCALE, 1.0)

    def chunk(r0, rc):
        cos = jnp.where(is_rope, cos_ref[r0:r0 + rc, :] * scale, 1.0)
        sin = jnp.where(is_rope, sin_ref[r0:r0 + rc, :] * scale, 0.0)
        for c in range(o_ref.shape[1] // d):
            t = h_ref[r0:r0 + rc, c * d:(c + 1) * d]
            rot = pltpu.roll(t, d // 2, axis=1)
            o_ref[r0:r0 + rc, c * d:(c + 1) * d] = (t * cos + rot * sin).astype(o_ref.dtype)

    return [functools.partial(chunk, r0, rc) for r0, rc in _row_chunks(o_ref.shape[0], 4 * d)]


def _pipelined_kernel(*refs, epilogue, n_w, n_aux, use_halo, lagged, nb, n_tiles, tiles_per_seq):
    it = iter(refs)
    a_ref = next(it)
    ah_ref = next(it) if use_halo else None
    w_refs = [next(it) for _ in range(n_w)]
    aux = [next(it) for _ in range(n_aux)]
    o_ref = next(it)
    aext_ref = next(it) if use_halo else None
    h = [next(it) for _ in range(2 if lagged else 1)]
    halo_rows = BF16_ROWS if use_halo else 0

    s = pl.program_id(0)
    t = jnp.minimum(s, n_tiles - 1)
    i = t // nb
    j = t % nb

    if use_halo:
        @pl.when((j == 0) & (s < n_tiles))
        def _():
            seq_start = (i % tiles_per_seq) == 0
            halo = ah_ref[...]
            aext_ref[0:BF16_ROWS, :] = jnp.where(seq_start, jnp.zeros_like(halo), halo)
            aext_ref[BF16_ROWS:, :] = a_ref[...]

    def matmul_into(h_ref):
        lhs = aext_ref[...] if use_halo else a_ref[...]
        col = 0
        for w_ref in w_refs:
            wn = w_ref.shape[1]
            step_cols = min(wn, MXU_COLS)
            for c0 in range(0, wn, step_cols):
                h_ref[:, col + c0:col + c0 + step_cols] = jnp.dot(
                    lhs, w_ref[:, c0:c0 + step_cols], preferred_element_type=F32)
            col += wn

    if not lagged:
        matmul_into(h[0])
        for chunk in epilogue(h[0], halo_rows, aux, o_ref, j):
            chunk()
        return

    jp = jnp.maximum(s - 1, 0) % nb
    slot = s % 2

    @pl.when(s == 0)
    def _():
        h[1][...] = jnp.zeros(h[1].shape, F32)

    def step(cur, prev):
        for chunk in epilogue(h[prev], halo_rows, aux, o_ref, jp):
            chunk()
        matmul_into(h[cur])

    @pl.when(slot == 0)
    def _():
        step(0, 1)

    @pl.when(slot == 1)
    def _():
        step(1, 0)


def pipelined_matmul(a, w, w_col_blocks, *, epilogue, col_aux=(), row_aux=(), out_cols, out_dtype,
                     nb, halo_seq_len=None, tm=1024, single_buffer_a=True, lagged=True, name):
    m, k = a.shape
    use_halo = halo_seq_len is not None
    tm = _pick(halo_seq_len if use_halo else m, tm)
    hb = tm // BF16_ROWS
    n_tiles = (m // tm) * nb
    hrows = tm + (BF16_ROWS if use_halo else 0)
    wn = sum(width for width, _ in w_col_blocks)

    def cur(s):
        t = jnp.minimum(s, n_tiles - 1)
        return t // nb, t % nb

    def prev(s):
        t = jnp.maximum(s - 1, 0) if lagged else s
        return t // nb, t % nb

    a_kwargs = dict(pipeline_mode=pl.Buffered(1)) if single_buffer_a else {}
    in_specs = [pl.BlockSpec((tm, k), lambda s: (cur(s)[0], 0), **a_kwargs)]
    args = [a]
    if use_halo:
        in_specs.append(pl.BlockSpec((BF16_ROWS, k), lambda s: (jnp.maximum(cur(s)[0] * hb - 1, 0), 0)))
        args.append(a)
    for width, first in w_col_blocks:
        in_specs.append(pl.BlockSpec((k, width), lambda s, first=first: (0, first + cur(s)[1])))
        args.append(w)
    for arr in col_aux:
        in_specs.append(pl.BlockSpec((None,) + arr.shape[1:], lambda s: (prev(s)[1], 0, 0)))
        args.append(arr)
    for arr in row_aux:
        in_specs.append(pl.BlockSpec((tm, arr.shape[1]), lambda s: (prev(s)[0], 0)))
        args.append(arr)
    scratch = ([pltpu.VMEM((hrows, k), BF16)] if use_halo else [])
    scratch += [pltpu.VMEM((hrows, wn), F32)] * (2 if lagged else 1)
    kern = functools.partial(
        _pipelined_kernel, epilogue=epilogue, n_w=len(w_col_blocks), n_aux=len(col_aux) + len(row_aux),
        use_halo=use_halo, lagged=lagged, nb=nb, n_tiles=n_tiles,
        tiles_per_seq=(halo_seq_len // tm) if use_halo else 1)
    return pl.pallas_call(
        kern,
        grid=(n_tiles + (1 if lagged else 0),),
        in_specs=in_specs,
        out_specs=pl.BlockSpec((tm, out_cols), lambda s: prev(s)),
        out_shape=jax.ShapeDtypeStruct((m, nb * out_cols), out_dtype),
        scratch_shapes=scratch,
        compiler_params=_params("arbitrary"),
        name=name,
    )(*args)


def _rope_table_kernel(pos_ref, inv_ref, cos_ref, sin_ref):
    freqs = pos_ref[...].astype(F32) * inv_ref[...]
    lane = lax.broadcasted_iota(jnp.int32, freqs.shape, 1)
    sign = jnp.where(lane < DIFF_HEAD_DIM // 2, -1.0, 1.0)
    cos_ref[...] = jnp.cos(freqs)
    sin_ref[...] = jnp.sin(freqs) * sign


def rope_tables(positions):
    m = positions.size
    inv = 1.0 / (ROPE_THETA ** (jnp.arange(0, DIFF_HEAD_DIM, 2, dtype=F32) / DIFF_HEAD_DIM))
    inv_full = jnp.concatenate([inv, inv]).reshape(1, DIFF_HEAD_DIM)
    tm = _pick(m, 1024)
    shape = jax.ShapeDtypeStruct((m, DIFF_HEAD_DIM), F32)
    return pl.pallas_call(
        _rope_table_kernel,
        grid=(m // tm,),
        in_specs=[pl.BlockSpec((tm, 1), lambda i: (i, 0)),
                  pl.BlockSpec((1, DIFF_HEAD_DIM), lambda i: (0, 0))],
        out_specs=[pl.BlockSpec((tm, DIFF_HEAD_DIM), lambda i: (i, 0))] * 2,
        out_shape=[shape, shape],
        compiler_params=_params("arbitrary"),
        name="rope_tables",
    )(positions.reshape(m, 1), inv_full)


def _diff_attn_kernel(q_ref, k_ref, v_ref, lq1_ref, lk1_ref, lq2_ref, lk2_ref, sw_ref, o_ref,
                      m_ref, l_ref, acc_ref, *, lambda_init, tq):
    qi = pl.program_id(2)
    d = DIFF_HEAD_DIM
    q = q_ref[...]

    m_ref[...] = jnp.full(m_ref.shape, -jnp.inf, F32)
    l_ref[...] = jnp.zeros(l_ref.shape, F32)
    acc_ref[...] = jnp.zeros(acc_ref.shape, F32)

    def kv_block(kb, masked):
        start = pl.multiple_of(kb * tq, tq)
        k = k_ref[pl.ds(start, tq), :]
        v = v_ref[pl.ds(start, tq), :]
        scores = [lax.dot_general(q[:, c * d:(c + 1) * d], k[:, c * d:(c + 1) * d],
                                  (((1,), (1,)), ((), ())), preferred_element_type=F32) for c in range(2)]
        probs = []
        alphas = []
        for c in range(2):
            s = scores[c]
            if masked:
                row = lax.broadcasted_iota(jnp.int32, s.shape, 0)
                col = lax.broadcasted_iota(jnp.int32, s.shape, 1)
                s = jnp.where(col <= row, s, -jnp.inf)
            chunks = [s[:, t * LANES:(t + 1) * LANES] for t in range(tq // LANES)]
            cmax = functools.reduce(jnp.maximum, chunks)
            m_old = m_ref[c]
            m_new = jnp.maximum(m_old, jnp.max(cmax, axis=-1, keepdims=True))
            alpha = jnp.exp2(m_old - m_new)
            ps = [jnp.exp2(ch - m_new) for ch in chunks]
            l_ref[c] = alpha * l_ref[c] + functools.reduce(jnp.add, ps)
            m_ref[c] = m_new
            probs.append(jnp.concatenate(ps, axis=1).astype(v.dtype))
            alphas.append(alpha)
        pv = jnp.dot(jnp.concatenate(probs, axis=0), v, preferred_element_type=F32)
        for c in range(2):
            a2 = jnp.concatenate([alphas[c], alphas[c]], axis=1)
            acc_ref[c] = a2 * acc_ref[c] + pv[c * tq:(c + 1) * tq]

    def body(kb, carry):
        kv_block(kb, False)
        return carry

    lax.fori_loop(0, qi, body, 0)
    kv_block(qi, True)

    lam = (jnp.exp(jnp.sum(lq1_ref[...] * lk1_ref[...], axis=-1, keepdims=True))
           - jnp.exp(jnp.sum(lq2_ref[...] * lk2_ref[...], axis=-1, keepdims=True)) + lambda_init)
    l0 = jnp.sum(l_ref[0], axis=-1, keepdims=True)
    l1 = jnp.sum(l_ref[1], axis=-1, keepdims=True)
    o = acc_ref[0] / l0 - lam * (acc_ref[1] / l1)
    o = o * lax.rsqrt(jnp.mean(o * o, axis=-1, keepdims=True) + EPS)
    o_ref[...] = ((o * sw_ref[...]) * (1.0 - lambda_init)).astype(o_ref.dtype)


def _cast_rows(w, steps):
    k, n = w.shape
    for rows in range(BF16_ROWS, k + 1, BF16_ROWS):
        if k % rows == 0 and k // rows <= steps:
            return rows if rows * n * 4 <= CAST_BLOCK_BYTES else None
    return None


def _plan_casts(weights, steps, lin):
    rows = [None if w.dtype == BF16 else _cast_rows(w, steps) for w in weights]
    arrays, specs, spans = [], [], []
    at_end = False
    for w, r in zip(weights, rows):
        if not r:
            continue
        nb = w.shape[0] // r
        first = steps - nb if at_end else 0
        at_end = not at_end if nb < steps else at_end
        arrays.append(w)
        spans.append((first, nb))
        specs.append(pl.BlockSpec(
            (r, w.shape[1]),
            lambda *ids, first=first, nb=nb: (jnp.clip(lin(*ids) - first, 0, nb - 1), 0)))

    def finish(outputs):
        done = iter(outputs)
        return [next(done) if r else w.astype(BF16) for w, r in zip(weights, rows)]

    return arrays, specs, spans, finish


def _run_casts(step, spans, cast_in, cast_out):
    for (first, nb), src, dst in zip(spans, cast_in, cast_out):
        @pl.when((step >= first) & (step < first + nb))
        def _():
            dst[...] = src[...].astype(dst.dtype)


def diff_attention(qkv, lq1, lk1, lq2, lk2, subln_w, lambda_init, batch, seq_len, *, tq=512,
                   cast_weights=()):
    m, n = qkv.shape
    d_model = n // 3
    hw = 2 * DIFF_HEAD_DIM
    heads = d_model // hw
    tq = _pick(seq_len, tq)
    nq = seq_len // tq

    def lin(b, h, qi):
        return (b * heads + h) * nq + qi

    cast_arrays, cast_specs, spans, finish = _plan_casts(cast_weights, batch * heads * nq, lin)
    n_cast = len(cast_arrays)

    def kern(*refs):
        main_in, cast_in = refs[:8], refs[8:8 + n_cast]
        o_ref = refs[8 + n_cast]
        cast_out = refs[9 + n_cast:9 + 2 * n_cast]
        scratch = refs[9 + 2 * n_cast:]
        _run_casts(lin(pl.program_id(0), pl.program_id(1), pl.program_id(2)), spans, cast_in, cast_out)
        _diff_attn_kernel(*main_in, o_ref, *scratch, lambda_init=lambda_init, tq=tq)

    vec = lambda a: a.reshape(1, -1)
    lam_spec = pl.BlockSpec((1, DIFF_HEAD_DIM), lambda b, h, qi: (0, 0))
    outs = pl.pallas_call(
        kern,
        grid=(batch, heads, nq),
        in_specs=[
            pl.BlockSpec((tq, hw), lambda b, h, qi: (b * nq + qi, h)),
            pl.BlockSpec((seq_len, hw), lambda b, h, qi: (b, heads + h)),
            pl.BlockSpec((seq_len, hw), lambda b, h, qi: (b, 2 * heads + h)),
            lam_spec, lam_spec, lam_spec, lam_spec,
            pl.BlockSpec((1, hw), lambda b, h, qi: (0, 0)),
        ] + cast_specs,
        out_specs=[pl.BlockSpec((tq, hw), lambda b, h, qi: (b * nq + qi, h))] + cast_specs,
        out_shape=[jax.ShapeDtypeStruct((m, d_model), BF16)]
        + [jax.ShapeDtypeStruct(w.shape, BF16) for w in cast_arrays],
        scratch_shapes=[pltpu.VMEM((2, tq, LANES), F32), pltpu.VMEM((2, tq, LANES), F32),
                        pltpu.VMEM((2, tq, hw), F32)],
        compiler_params=_params("arbitrary", "arbitrary", "arbitrary"),
        name="diff_attention",
    )(qkv, qkv, qkv, vec(lq1), vec(lk1), vec(lq2), vec(lk2), vec(subln_w), *cast_arrays)
    return outs[0], finish(outs[1:])


def _dot_exact(a, b):
    return jnp.dot(a, b, preferred_element_type=F32, precision=lax.Precision.HIGHEST)


def _ssd_kernel(z_ref, x_ref, xh_ref, b_ref, bh_ref, c_ref, ch_ref, dt_ref,
                cwx_ref, cwb_ref, cwc_ref, cbx_ref, cbb_ref, cbc_ref,
                dtb_ref, alog_ref, dexp_ref, nw_ref, sel_ref, expand_ref, o_ref, state_ref, *, heads_per_group):
    g = pl.program_id(1)
    c = pl.program_id(2)
    t = CHUNK
    hpg = heads_per_group
    p = SSM_HEADDIM
    gw = hpg * p
    nh = dt_ref.shape[1]

    @pl.when(c == 0)
    def _():
        state_ref[...] = jnp.zeros(state_ref.shape, F32)

    seq_start = c == 0

    def conv_silu(cur_ref, halo_ref, w_ref, bias_ref):
        halo = halo_ref[...]
        halo = jnp.where(seq_start, jnp.zeros_like(halo), halo)
        ext = jnp.concatenate([halo, cur_ref[...]], axis=0)
        return _silu(_causal_conv_ext(ext, SUBLANES, w_ref[...], bias_ref[...]))

    xc = conv_silu(x_ref, xh_ref, cwx_ref, cbx_ref)
    bc = conv_silu(b_ref, bh_ref, cwb_ref, cbb_ref)
    cc = conv_silu(c_ref, ch_ref, cwc_ref, cbc_ref)

    dtr = dt_ref[...] + dtb_ref[...]
    dtv = jnp.maximum(dtr, 0.0) + jnp.log1p(jnp.exp(-jnp.abs(dtr)))
    a_all = dtv * (-jnp.exp(alog_ref[...]))

    row_t = lax.broadcasted_iota(jnp.int32, (t, t), 0)
    col_t = lax.broadcasted_iota(jnp.int32, (t, t), 1)
    causal = col_t <= row_t
    a_cs_all = _dot_exact(jnp.where(causal, 1.0, 0.0), a_all)
    a_last = a_cs_all[t - 1:t, :]

    a_cs = _dot_exact(a_cs_all, sel_ref[...])
    a_cs_t = a_cs.T

    stacked = jnp.concatenate([dtv, jnp.exp(a_cs_all), jnp.exp(a_last - a_cs_all)], axis=0)
    hi = stacked.astype(BF16)
    lo = (stacked - hi.astype(F32)).astype(BF16)
    expanded = jnp.dot(jnp.concatenate([hi, lo], axis=1), expand_ref[...],
                       preferred_element_type=F32)
    dt_e = expanded[0:t]
    exp_acs_e = expanded[t:2 * t]
    dec_e = expanded[2 * t:3 * t]

    xdt = xc * dt_e
    xdt_b = xdt.astype(BF16)
    bb = bc.astype(BF16)
    cb = cc.astype(BF16)
    cbm = lax.dot_general(cb, bb, (((1,), (1,)), ((), ())), preferred_element_type=F32)

    state = state_ref[...]
    y = jnp.dot(cb, state.astype(BF16), preferred_element_type=F32) * exp_acs_e

    lane = lax.broadcasted_iota(jnp.int32, (t, 2 * p), 1)
    pieces = []
    for pair in range(hpg // 2):
        x_pair = xdt_b[:, pair * 2 * p:(pair + 1) * 2 * p]
        outs = []
        for hh in (2 * pair, 2 * pair + 1):
            seg = a_cs[:, hh:hh + 1] - a_cs_t[hh:hh + 1, :]
            decay = jnp.exp(jnp.where(causal, seg, -jnp.inf))
            outs.append(jnp.dot((cbm * decay).astype(BF16), x_pair, preferred_element_type=F32))
        pieces.append(jnp.where(lane < p, outs[0], outs[1]))
    y = y + jnp.concatenate(pieces, axis=1) + dexp_ref[...] * xc

    new_contrib = jnp.dot(bc.T.astype(BF16), (xdt * dec_e).astype(BF16), preferred_element_type=F32)
    state_ref[...] = state * exp_acs_e[t - 1:t, :] + new_contrib

    gn = y * _silu(z_ref[...])
    gn = gn * lax.rsqrt(jnp.mean(gn * gn, axis=-1, keepdims=True) + EPS)
    o_ref[...] = (gn * nw_ref[...]).astype(o_ref.dtype)


def ssd_mixer(zx, bcm, dt, conv_w, conv_b, dt_bias, a_log, d_skip, norm_w, batch, seq_len, cast_weights=()):
    m = zx.shape[0]
    d_inner = zx.shape[1] // 2
    nh = dt.shape[1]
    hpg = nh // SSM_GROUPS
    gw = d_inner // SSM_GROUPS
    n = SSM_STATE
    t = CHUNK
    nc = seq_len // t
    hb = t // SUBLANES
    ng = SSM_GROUPS
    xb0 = d_inner // gw
    cwb0 = d_inner // n

    def row(b, g, c):
        return b * nc + c

    def halo(b, g, c):
        return jnp.maximum((b * nc + c) * hb - 1, 0)

    d_exp = jnp.repeat(d_skip.astype(F32), SSM_HEADDIM).reshape(1, d_inner)
    conv_b2 = conv_b.reshape(1, -1)
    head = jnp.arange(nh)
    sel = (head[None, :, None] == (jnp.arange(ng) * hpg)[:, None, None] + head[None, None, :]) \
        & (head[None, None, :] < hpg)
    chan_head = jnp.arange(gw) // SSM_HEADDIM
    expand = head[None, :, None] == (jnp.arange(ng) * hpg)[:, None, None] + chan_head[None, None, :]
    sel = sel.astype(F32)
    expand = jnp.concatenate([expand, expand], axis=1).astype(BF16)
    def lin(b, g, c):
        return (b * ng + g) * nc + c

    cast_arrays, cast_specs, spans, finish = _plan_casts(cast_weights, batch * ng * nc, lin)
    n_cast = len(cast_arrays)
    n_main = 20

    def kern(*refs):
        main_in, cast_in = refs[:n_main], refs[n_main:n_main + n_cast]
        o_ref = refs[n_main + n_cast]
        cast_out = refs[n_main + n_cast + 1:n_main + 2 * n_cast + 1]
        scratch = refs[n_main + 2 * n_cast + 1:]
        _run_casts(lin(pl.program_id(0), pl.program_id(1), pl.program_id(2)), spans, cast_in, cast_out)
        _ssd_kernel(*main_in, o_ref, *scratch, heads_per_group=hpg)

    outs = pl.pallas_call(
        kern,
        grid=(batch, ng, nc),
        in_specs=[
            pl.BlockSpec((t, gw), lambda b, g, c: (row(b, g, c), g)),
            pl.BlockSpec((t, gw), lambda b, g, c: (row(b, g, c), xb0 + g)),
            pl.BlockSpec((SUBLANES, gw), lambda b, g, c: (halo(b, g, c), xb0 + g)),
            pl.BlockSpec((t, n), lambda b, g, c: (row(b, g, c), g)),
            pl.BlockSpec((SUBLANES, n), lambda b, g, c: (halo(b, g, c), g)),
            pl.BlockSpec((t, n), lambda b, g, c: (row(b, g, c), ng + g)),
            pl.BlockSpec((SUBLANES, n), lambda b, g, c: (halo(b, g, c), ng + g)),
            pl.BlockSpec((t, nh), lambda b, g, c: (row(b, g, c), 0)),
            pl.BlockSpec((SSM_CONV, gw), lambda b, g, c: (0, g)),
            pl.BlockSpec((SSM_CONV, n), lambda b, g, c: (0, cwb0 + g)),
            pl.BlockSpec((SSM_CONV, n), lambda b, g, c: (0, cwb0 + ng + g)),
            pl.BlockSpec((1, gw), lambda b, g, c: (0, g)),
            pl.BlockSpec((1, n), lambda b, g, c: (0, cwb0 + g)),
            pl.BlockSpec((1, n), lambda b, g, c: (0, cwb0 + ng + g)),
            pl.BlockSpec((1, nh), lambda b, g, c: (0, 0)),
            pl.BlockSpec((1, nh), lambda b, g, c: (0, 0)),
            pl.BlockSpec((1, gw), lambda b, g, c: (0, g)),
            pl.BlockSpec((1, gw), lambda b, g, c: (0, g)),
            pl.BlockSpec((None, nh, nh), lambda b, g, c: (g, 0, 0)),
            pl.BlockSpec((None, 2 * nh, gw), lambda b, g, c: (g, 0, 0)),
        ] + cast_specs,
        out_specs=[pl.BlockSpec((t, gw), lambda b, g, c: (row(b, g, c), g))] + cast_specs,
        out_shape=[jax.ShapeDtypeStruct((m, d_inner), BF16)]
        + [jax.ShapeDtypeStruct(w.shape, BF16) for w in cast_arrays],
        scratch_shapes=[pltpu.VMEM((n, gw), F32)],
        compiler_params=_params("arbitrary", "arbitrary", "arbitrary"),
        name="ssd_mixer",
    )(zx, zx, zx, bcm, bcm, bcm, bcm, dt, conv_w, conv_w, conv_w, conv_b2, conv_b2, conv_b2,
      dt_bias.reshape(1, nh), a_log.reshape(1, nh), d_exp, norm_w.reshape(1, d_inner), sel, expand,
      *cast_arrays)
    return outs[0], finish(outs[1:])


def mamba_layer(xf, u, w_in, conv_w, conv_b, dt_bias, a_log, d_skip, norm_w, w_out, batch, seq_len,
                cast_in_proj=(), cast_ssd=(), cast_out_proj=()):
    nh = dt_bias.shape[0]
    d_inner = norm_w.shape[0]
    bc_cols = 2 * SSM_GROUPS * SSM_STATE
    w_in = w_in.astype(BF16)
    zx, c_in = matmul(u, w_in, col0=0, n=2 * d_inner, tm=1024, tn=1024, cast_weights=list(cast_in_proj))
    bcm = matmul(u, w_in, col0=2 * d_inner, n=bc_cols, tm=1024, tn=1024)
    dt = matmul(u, w_in, col0=2 * d_inner + bc_cols, n=nh, tm=1024, tn=nh)
    yn, c_ssd = ssd_mixer(zx, bcm, dt, conv_w, conv_b, dt_bias, a_log, d_skip, norm_w, batch, seq_len,
                          cast_weights=(w_out,) + tuple(cast_ssd))
    xf, c_out = matmul(yn, c_ssd[0], tm=1024, tn=512, residual=xf, single_buffer_a=True,
                       cast_weights=list(cast_out_proj))
    return xf, c_in, c_ssd[1:], c_out


def attention_layer(xf, u, cos, sin, w_qkv, lq1, lk1, lq2, lk2, subln_w, w_o, lambda_init, batch, seq_len,
                    cast_weights=()):
    d_model = w_o.shape[0]
    tn = _pick(d_model, 1024)
    nb = w_qkv.shape[1] // tn
    qkv = pipelined_matmul(u, w_qkv.astype(BF16), [(tn, 0)],
                           epilogue=functools.partial(_epilogue_rope, rope_tiles=2 * d_model // tn),
                           row_aux=(cos, sin), out_cols=tn, out_dtype=BF16, nb=nb,
                           single_buffer_a=False, name="qkv_rope")
    o, casted = diff_attention(qkv, lq1, lk1, lq2, lk2, subln_w, lambda_init, batch, seq_len,
                               cast_weights=cast_weights)
    return matmul(o, w_o.astype(BF16), tm=1024, tn=1024, residual=xf), casted


def ffn_layer(xf, u, w_up, conv_w, conv_b, w_down, seq_len, cast_down_proj=()):
    d_ff = w_up.shape[1] // 2
    tn = _pick(d_ff, 256)
    nb = d_ff // tn
    cw = conv_w.astype(F32).reshape(FFN_CONV, 2, nb, tn).transpose(2, 0, 1, 3).reshape(nb, FFN_CONV, 2 * tn)
    cb = conv_b.astype(F32).reshape(2, nb, tn).transpose(1, 0, 2).reshape(nb, 1, 2 * tn)
    g = pipelined_matmul(u, w_up.astype(BF16), [(tn, 0), (tn, nb)],
                         epilogue=functools.partial(_epilogue_conv_gate, aligned=True),
                         col_aux=(cw, cb), out_cols=tn, out_dtype=BF16, nb=nb, halo_seq_len=seq_len,
                         tm=2048, lagged=False, single_buffer_a=True, name="ffn_up")
    return matmul(g, w_down.astype(BF16), tm=1024, tn=256, residual=xf, single_buffer_a=True,
                  cast_weights=list(cast_down_proj))


def kernel(x, positions, l0_norm_mix, l0_m_w_in, l0_m_conv_w, l0_m_conv_b, l0_m_dt_bias, l0_m_a_log, l0_m_d, l0_m_norm, l0_m_w_out, l0_norm_ffn, l0_f_w_up, l0_f_conv_w, l0_f_conv_b, l0_f_w_down, l1_norm_mix, l1_a_w_qkv, l1_a_lq1, l1_a_lk1, l1_a_lq2, l1_a_lk2, l1_a_subln, l1_a_w_o, l1_norm_ffn, l1_f_w_up, l1_f_conv_w, l1_f_conv_b, l1_f_w_down, l2_norm_mix, l2_m_w_in, l2_m_conv_w, l2_m_conv_b, l2_m_dt_bias, l2_m_a_log, l2_m_d, l2_m_norm, l2_m_w_out, l2_norm_ffn, l2_f_w_up, l2_f_conv_w, l2_f_conv_b, l2_f_w_down, l3_norm_mix, l3_a_w_qkv, l3_a_lq1, l3_a_lk1, l3_a_lq2, l3_a_lk2, l3_a_subln, l3_a_w_o, l3_norm_ffn, l3_f_w_up, l3_f_conv_w, l3_f_conv_b, l3_f_w_down, final_norm):
    norm_mix = [l0_norm_mix, l1_norm_mix, l2_norm_mix, l3_norm_mix]
    norm_ffn = [l0_norm_ffn, l1_norm_ffn, l2_norm_ffn, l3_norm_ffn]
    mixer_params = [
        (l0_m_w_in, l0_m_conv_w, l0_m_conv_b, l0_m_dt_bias, l0_m_a_log, l0_m_d, l0_m_norm, l0_m_w_out),
        (l1_a_w_qkv, l1_a_lq1, l1_a_lk1, l1_a_lq2, l1_a_lk2, l1_a_subln, l1_a_w_o),
        (l2_m_w_in, l2_m_conv_w, l2_m_conv_b, l2_m_dt_bias, l2_m_a_log, l2_m_d, l2_m_norm, l2_m_w_out),
        (l3_a_w_qkv, l3_a_lq1, l3_a_lk1, l3_a_lq2, l3_a_lk2, l3_a_subln, l3_a_w_o),
    ]
    ffn_params = [
        (l0_f_w_up, l0_f_conv_w, l0_f_conv_b, l0_f_w_down),
        (l1_f_w_up, l1_f_conv_w, l1_f_conv_b, l1_f_w_down),
        (l2_f_w_up, l2_f_conv_w, l2_f_conv_b, l2_f_w_down),
        (l3_f_w_up, l3_f_conv_w, l3_f_conv_b, l3_f_w_down),
    ]
    batch, seq_len, d_model = x.shape
    xf = x.reshape(batch * seq_len, d_model)
    cos, sin = rope_tables(positions)
    n_layers = len(norm_mix)
    for i in range(n_layers):
        u = rmsnorm(xf, norm_mix[i], BF16)
        w_up, cw, cb, w_down = ffn_params[i]
        nxt = list(mixer_params[i + 1]) if i + 1 < n_layers else None
        if i % N_MIXERS == 0:
            xf, (w_up,), (w_down,), c_out = mamba_layer(
                xf, u, *mixer_params[i], batch, seq_len, cast_in_proj=[w_up], cast_ssd=[w_down],
                cast_out_proj=[nxt[0]] if nxt else [])
            if nxt:
                nxt[0] = c_out[0]
            cast_down = [nxt[-1], ffn_params[i + 1][0]] if nxt else []
        else:
            lambda_init = 0.8 - 0.6 * math.exp(-0.3 * i)
            xf, (w_up, w_down) = attention_layer(xf, u, cos, sin, *mixer_params[i], lambda_init, batch,
                                                 seq_len, cast_weights=[w_up, w_down])
            cast_down = [nxt[0]] if nxt else []
        u = rmsnorm(xf, norm_ffn[i], BF16)
        xf, c_down = ffn_layer(xf, u, w_up, cw, cb, w_down, seq_len, cast_down_proj=cast_down)
        if nxt:
            nxt[-1 if i % N_MIXERS == 0 else 0] = c_down[0]
            mixer_params[i + 1] = tuple(nxt)
            if len(c_down) > 1:
                ffn_params[i + 1] = (c_down[1],) + tuple(ffn_params[i + 1][1:])
    return rmsnorm(xf, final_norm, F32).reshape(batch, seq_len, d_model)
```

```python
import functools
import math

import jax
import jax.numpy as jnp
from jax import lax
from jax.experimental import pallas as pl
from jax.experimental.pallas import tpu as pltpu

F32 = jnp.float32
BF16 = jnp.bfloat16

EPS = 1e-5
N_MIXERS = 2

SSM_HEADDIM = 64
SSM_GROUPS = 8
SSM_STATE = 128
SSM_CONV = 4
CHUNK = 128

DIFF_HEAD_DIM = 128
ATTN_SCALE = DIFF_HEAD_DIM ** -0.5
Q_PRESCALE = ATTN_SCALE * math.log2(math.e)
ROPE_THETA = 10000.0

FFN_CONV = 3

V7X_VMEM_BYTES = 64 * 1024 * 1024
VMEM_LIMIT_BYTES = V7X_VMEM_BYTES - 8 * 1024 * 1024
SUBLANES = 8
BF16_ROWS = 16
LANES = 128
MXU_COLS = 256
CAST_BLOCK_BYTES = 4 * 1024 * 1024
EPILOGUE_VREGS = 32


def _params(*sem):
    return pltpu.CompilerParams(dimension_semantics=sem, vmem_limit_bytes=VMEM_LIMIT_BYTES)


def _pick(dim, pref):
    if dim <= pref:
        return dim
    t = pref
    while dim % t:
        t //= 2
    return t


def _silu(x):
    return x / (1.0 + jnp.exp(-x))


def _rmsnorm_kernel(x_ref, w_ref, o_ref):
    x = x_ref[...]
    y = x * lax.rsqrt(jnp.mean(x * x, axis=-1, keepdims=True) + EPS)
    o_ref[...] = (y * w_ref[...]).astype(o_ref.dtype)


def rmsnorm(x, w, out_dtype):
    m, d = x.shape
    tm = _pick(m, 256)
    return pl.pallas_call(
        _rmsnorm_kernel,
        grid=(m // tm,),
        in_specs=[pl.BlockSpec((tm, d), lambda i: (i, 0)),
                  pl.BlockSpec((1, d), lambda i: (0, 0))],
        out_specs=pl.BlockSpec((tm, d), lambda i: (i, 0)),
        out_shape=jax.ShapeDtypeStruct((m, d), out_dtype),
        compiler_params=_params("arbitrary"),
        name="rmsnorm",
    )(x, w.reshape(1, d))


def _row_rstd(ssq, d):
    return lax.rsqrt(jnp.sum(ssq, axis=-1, keepdims=True) / d + EPS)


def _mm_kernel(*refs, has_residual, has_norm, scale_d, n_cast):
    it = iter(refs)
    a_ref, w_ref = next(it), next(it)
    r_ref = next(it) if has_residual else None
    nw_ref = next(it) if has_norm else None
    ssq_in_ref = next(it) if scale_d else None
    cast_in = [next(it) for _ in range(n_cast)]
    o_ref = next(it)
    xw_ref, ssq_ref = (next(it), next(it)) if has_norm else (None, None)
    cast_out = [next(it) for _ in range(n_cast)]
    for src, dst in zip(cast_in, cast_out):
        dst[...] = src[...].astype(dst.dtype)
    acc = jnp.dot(a_ref[...], w_ref[...], preferred_element_type=F32)
    if scale_d:
        acc = acc * _row_rstd(ssq_in_ref[...], scale_d)
    if has_residual:
        acc = r_ref[...] + acc
    o_ref[...] = acc.astype(o_ref.dtype)
    if has_norm:
        xw_ref[...] = (acc * nw_ref[...]).astype(xw_ref.dtype)
        sq = acc * acc
        part = functools.reduce(jnp.add, [sq[:, t * LANES:(t + 1) * LANES] for t in range(sq.shape[1] // LANES)])

        @pl.when(pl.program_id(1) == 0)
        def _():
            ssq_ref[...] = jnp.zeros(ssq_ref.shape, F32)

        ssq_ref[...] += part


def matmul(a, w, *, col0=0, n=None, tm, tn, out_dtype=F32, residual=None, single_buffer_a=False,
           cast_weights=None, row_ssq=None, norm_w=None):
    want_casts = cast_weights is not None
    cast_weights = cast_weights or ()
    m, k = a.shape
    n = w.shape[1] - col0 if n is None else n
    tm = _pick(m, tm)
    tn = _pick(n, tn)
    assert col0 % tn == 0
    jb = col0 // tn
    nj = n // tn
    steps = (m // tm) * nj
    a_kwargs = dict(pipeline_mode=pl.Buffered(1)) if single_buffer_a else {}
    in_specs = [pl.BlockSpec((tm, k), lambda i, j: (i, 0), **a_kwargs),
                pl.BlockSpec((k, tn), lambda i, j: (0, j + jb))]
    args = [a, w]
    tile = pl.BlockSpec((tm, tn), lambda i, j: (i, j))
    rows = pl.BlockSpec((tm, LANES), lambda i, j: (i, 0))
    if residual is not None:
        in_specs.append(tile)
        args.append(residual)
    if norm_w is not None:
        in_specs.append(pl.BlockSpec((1, tn), lambda i, j: (0, j)))
        args.append(norm_w.reshape(1, n).astype(F32))
    if row_ssq is not None:
        in_specs.append(rows)
        args.append(row_ssq[0])
    in_kernel = [c.shape[0] % (steps * BF16_ROWS) == 0
                 and (c.shape[0] // steps) * c.shape[1] * 4 <= CAST_BLOCK_BYTES for c in cast_weights]
    cast_arrays = [c for c, ok in zip(cast_weights, in_kernel) if ok]
    cast_specs = [pl.BlockSpec((c.shape[0] // steps, c.shape[1]), lambda i, j: (i * nj + j, 0))
                  for c in cast_arrays]
    norm_specs = [tile, rows] if norm_w is not None else []
    norm_shapes = ([jax.ShapeDtypeStruct((m, n), BF16), jax.ShapeDtypeStruct((m, LANES), F32)]
                   if norm_w is not None else [])
    outs = pl.pallas_call(
        functools.partial(_mm_kernel, has_residual=residual is not None, has_norm=norm_w is not None,
                          scale_d=row_ssq[1] if row_ssq is not None else 0, n_cast=len(cast_arrays)),
        grid=(m // tm, nj),
        in_specs=in_specs + cast_specs,
        out_specs=[tile] + norm_specs + cast_specs,
        out_shape=[jax.ShapeDtypeStruct((m, n), out_dtype)] + norm_shapes
        + [jax.ShapeDtypeStruct(c.shape, BF16) for c in cast_arrays],
        compiler_params=_params("arbitrary", "arbitrary"),
        name="matmul",
    )(*args, *cast_arrays)
    result = [outs[0]]
    n_norm = len(norm_specs)
    if norm_w is not None:
        result.append((outs[1], outs[2]))
    if want_casts:
        done = iter(outs[1 + n_norm:])
        result.append([next(done) if ok else c.astype(BF16) for c, ok in zip(cast_weights, in_kernel)])
    return result[0] if len(result) == 1 else tuple(result)


def _shift_rows(ext, s, halo_rows):
    if s == 0:
        return ext[halo_rows:]
    return pltpu.roll(ext, s, axis=0)[halo_rows:]


def _causal_conv_ext(ext, halo_rows, w, b):
    k_width = w.shape[0]
    y = b + w[0:1] * _shift_rows(ext, k_width - 1, halo_rows)
    for k in range(1, k_width):
        y = y + w[k:k + 1] * _shift_rows(ext, k_width - 1 - k, halo_rows)
    return y


def _row_chunks(rows, cols):
    rc = max(BF16_ROWS, EPILOGUE_VREGS * SUBLANES * LANES // cols)
    rc = min(rc, rows)
    return [(r0, rc) for r0 in range(0, rows, rc)]


def _epilogue_conv_gate(h_ref, halo_rows, aux, o_ref, jp):
    cw, cb = aux[0][...], aux[1][...]
    tn = o_ref.shape[1]

    def chunk(r0, rc):
        ext = h_ref[pl.ds(halo_rows + r0 - SUBLANES, rc + SUBLANES), :]
        y = _causal_conv_ext(ext, SUBLANES, cw, cb)
        o_ref[r0:r0 + rc, :] = (_silu(y[:, :tn]) * y[:, tn:]).astype(o_ref.dtype)

    return [functools.partial(chunk, r0, rc) for r0, rc in _row_chunks(o_ref.shape[0], 2 * tn)]


def _epilogue_rope(h_ref, halo_rows, aux, o_ref, jp, *, rope_tiles):
    cos_ref, sin_ref = aux
    d = DIFF_HEAD_DIM
    is_rope = jp < rope_tiles
    scale = jnp.where(jp < rope_tiles // 2, Q_PRESCALE, 1.0)

    def chunk(r0, rc):
        cos = jnp.where(is_rope, cos_ref[r0:r0 + rc, :] * scale, 1.0)
        sin = jnp.where(is_rope, sin_ref[r0:r0 + rc, :] * scale, 0.0)
        for c in range(o_ref.shape[1] // d):
            t = h_ref[r0:r0 + rc, c * d:(c + 1) * d]
            rot = pltpu.roll(t, d // 2, axis=1)
            o_ref[r0:r0 + rc, c * d:(c + 1) * d] = (t * cos + rot * sin).astype(o_ref.dtype)

    return [functools.partial(chunk, r0, rc) for r0, rc in _row_chunks(o_ref.shape[0], 4 * d)]


def _pipelined_kernel(*refs, epilogue, n_w, n_aux, use_halo, lagged, scale_d, nb, n_tiles, tiles_per_seq):
    it = iter(refs)
    a_ref = next(it)
    ah_ref = next(it) if use_halo else None
    w_refs = [next(it) for _ in range(n_w)]
    ssq_ref = next(it) if scale_d else None
    ssqh_ref = next(it) if (scale_d and use_halo) else None
    aux = [next(it) for _ in range(n_aux)]
    o_ref = next(it)
    aext_ref = next(it) if use_halo else None
    h = [next(it) for _ in range(2 if lagged else 1)]
    halo_rows = BF16_ROWS if use_halo else 0

    s = pl.program_id(0)
    t = jnp.minimum(s, n_tiles - 1)
    i = t // nb
    j = t % nb

    if use_halo:
        @pl.when((j == 0) & (s < n_tiles))
        def _():
            seq_start = (i % tiles_per_seq) == 0
            halo = ah_ref[...]
            aext_ref[0:BF16_ROWS, :] = jnp.where(seq_start, jnp.zeros_like(halo), halo)
            aext_ref[BF16_ROWS:, :] = a_ref[...]

    def matmul_into(h_ref):
        lhs_ref = aext_ref if use_halo else a_ref
        rstd = None
        if scale_d:
            rstd = _row_rstd(ssq_ref[...], scale_d)
            if use_halo:
                rstd = jnp.concatenate([_row_rstd(ssqh_ref[...], scale_d), rstd], axis=0)
        col = 0
        for w_ref in w_refs:
            wn = w_ref.shape[1]
            step_cols = min(wn, MXU_COLS)
            for c0 in range(0, wn, step_cols):
                prod = jnp.dot(lhs_ref[...], w_ref[:, c0:c0 + step_cols], preferred_element_type=F32)
                h_ref[:, col + c0:col + c0 + step_cols] = prod if rstd is None else prod * rstd
            col += wn

    if not lagged:
        matmul_into(h[0])
        for chunk in epilogue(h[0], halo_rows, aux, o_ref, j):
            chunk()
        return

    jp = jnp.maximum(s - 1, 0) % nb
    slot = s % 2

    @pl.when(s == 0)
    def _():
        h[1][...] = jnp.zeros(h[1].shape, F32)

    def step(cur, prev):
        for chunk in epilogue(h[prev], halo_rows, aux, o_ref, jp):
            chunk()
        matmul_into(h[cur])

    @pl.when(slot == 0)
    def _():
        step(0, 1)

    @pl.when(slot == 1)
    def _():
        step(1, 0)


def pipelined_matmul(a, w, w_col_blocks, *, epilogue, col_aux=(), row_aux=(), out_cols, out_dtype,
                     nb, halo_seq_len=None, tm=1024, single_buffer_a=True, lagged=True, row_ssq=None,
                     name):
    m, k = a.shape
    use_halo = halo_seq_len is not None
    tm = _pick(halo_seq_len if use_halo else m, tm)
    hb = tm // BF16_ROWS
    n_tiles = (m // tm) * nb
    hrows = tm + (BF16_ROWS if use_halo else 0)
    wn = sum(width for width, _ in w_col_blocks)

    def cur(s):
        t = jnp.minimum(s, n_tiles - 1)
        return t // nb, t % nb

    def prev(s):
        t = jnp.maximum(s - 1, 0) if lagged else s
        return t // nb, t % nb

    a_kwargs = dict(pipeline_mode=pl.Buffered(1)) if single_buffer_a else {}
    in_specs = [pl.BlockSpec((tm, k), lambda s: (cur(s)[0], 0), **a_kwargs)]
    args = [a]
    if use_halo:
        in_specs.append(pl.BlockSpec((BF16_ROWS, k), lambda s: (jnp.maximum(cur(s)[0] * hb - 1, 0), 0)))
        args.append(a)
    for width, first in w_col_blocks:
        in_specs.append(pl.BlockSpec((k, width), lambda s, first=first: (0, first + cur(s)[1])))
        args.append(w)
    if row_ssq is not None:
        in_specs.append(pl.BlockSpec((tm, LANES), lambda s: (cur(s)[0], 0)))
        args.append(row_ssq[0])
        if use_halo:
            in_specs.append(pl.BlockSpec((BF16_ROWS, LANES), lambda s: (jnp.maximum(cur(s)[0] * hb - 1, 0), 0)))
            args.append(row_ssq[0])
    for arr in col_aux:
        in_specs.append(pl.BlockSpec((None,) + arr.shape[1:], lambda s: (prev(s)[1], 0, 0)))
        args.append(arr)
    for arr in row_aux:
        in_specs.append(pl.BlockSpec((tm, arr.shape[1]), lambda s: (prev(s)[0], 0)))
        args.append(arr)
    scratch = ([pltpu.VMEM((hrows, k), BF16)] if use_halo else [])
    scratch += [pltpu.VMEM((hrows, wn), F32)] * (2 if lagged else 1)
    kern = functools.partial(
        _pipelined_kernel, epilogue=epilogue, n_w=len(w_col_blocks), n_aux=len(col_aux) + len(row_aux),
        use_halo=use_halo, lagged=lagged, scale_d=row_ssq[1] if row_ssq is not None else 0, nb=nb,
        n_tiles=n_tiles,
        tiles_per_seq=(halo_seq_len // tm) if use_halo else 1)
    return pl.pallas_call(
        kern,
        grid=(n_tiles + (1 if lagged else 0),),
        in_specs=in_specs,
        out_specs=pl.BlockSpec((tm, out_cols), lambda s: prev(s)),
        out_shape=jax.ShapeDtypeStruct((m, nb * out_cols), out_dtype),
        scratch_shapes=scratch,
        compiler_params=_params("arbitrary"),
        name=name,
    )(*args)


def _rope_table_kernel(pos_ref, inv_ref, cos_ref, sin_ref):
    freqs = pos_ref[...].astype(F32) * inv_ref[...]
    lane = lax.broadcasted_iota(jnp.int32, freqs.shape, 1)
    sign = jnp.where(lane < DIFF_HEAD_DIM // 2, -1.0, 1.0)
    cos_ref[...] = jnp.cos(freqs)
    sin_ref[...] = jnp.sin(freqs) * sign


def rope_tables(positions):
    m = positions.size
    inv = 1.0 / (ROPE_THETA ** (jnp.arange(0, DIFF_HEAD_DIM, 2, dtype=F32) / DIFF_HEAD_DIM))
    inv_full = jnp.concatenate([inv, inv]).reshape(1, DIFF_HEAD_DIM)
    tm = _pick(m, 1024)
    shape = jax.ShapeDtypeStruct((m, DIFF_HEAD_DIM), F32)
    return pl.pallas_call(
        _rope_table_kernel,
        grid=(m // tm,),
        in_specs=[pl.BlockSpec((tm, 1), lambda i: (i, 0)),
                  pl.BlockSpec((1, DIFF_HEAD_DIM), lambda i: (0, 0))],
        out_specs=[pl.BlockSpec((tm, DIFF_HEAD_DIM), lambda i: (i, 0))] * 2,
        out_shape=[shape, shape],
        compiler_params=_params("arbitrary"),
        name="rope_tables",
    )(positions.reshape(m, 1), inv_full)


def _diff_attn_kernel(q_ref, k_ref, v_ref, lq1_ref, lk1_ref, lq2_ref, lk2_ref, sw_ref, o_ref,
                      m_ref, l_ref, acc_ref, *, lambda_init, tq):
    qi = pl.program_id(2)
    d = DIFF_HEAD_DIM
    q = q_ref[...]

    m_ref[...] = jnp.full(m_ref.shape, -jnp.inf, F32)
    l_ref[...] = jnp.zeros(l_ref.shape, F32)
    acc_ref[...] = jnp.zeros(acc_ref.shape, F32)

    def kv_block(kb, masked):
        start = pl.multiple_of(kb * tq, tq)
        k = k_ref[pl.ds(start, tq), :]
        v = v_ref[pl.ds(start, tq), :]
        scores = [lax.dot_general(q[:, c * d:(c + 1) * d], k[:, c * d:(c + 1) * d],
                                  (((1,), (1,)), ((), ())), preferred_element_type=F32) for c in range(2)]
        probs = []
        alphas = []
        for c in range(2):
            s = scores[c]
            if masked:
                row = lax.broadcasted_iota(jnp.int32, s.shape, 0)
                col = lax.broadcasted_iota(jnp.int32, s.shape, 1)
                s = jnp.where(col <= row, s, -jnp.inf)
            chunks = [s[:, t * LANES:(t + 1) * LANES] for t in range(tq // LANES)]
            cmax = functools.reduce(jnp.maximum, chunks)
            m_old = m_ref[c]
            m_new = jnp.maximum(m_old, jnp.max(cmax, axis=-1, keepdims=True))
            alpha = jnp.exp2(m_old - m_new)
            ps = [jnp.exp2(ch - m_new) for ch in chunks]
            l_ref[c] = alpha * l_ref[c] + functools.reduce(jnp.add, ps)
            m_ref[c] = m_new
            probs.append(jnp.concatenate(ps, axis=1).astype(v.dtype))
            alphas.append(alpha)
        pv = jnp.dot(jnp.concatenate(probs, axis=0), v, preferred_element_type=F32)
        for c in range(2):
            a2 = jnp.concatenate([alphas[c], alphas[c]], axis=1)
            acc_ref[c] = a2 * acc_ref[c] + pv[c * tq:(c + 1) * tq]

    def body(kb, carry):
        kv_block(kb, False)
        return carry

    lax.fori_loop(0, qi, body, 0)
    kv_block(qi, True)

    lam = (jnp.exp(jnp.sum(lq1_ref[...] * lk1_ref[...], axis=-1, keepdims=True))
           - jnp.exp(jnp.sum(lq2_ref[...] * lk2_ref[...], axis=-1, keepdims=True)) + lambda_init)
    l0 = jnp.sum(l_ref[0], axis=-1, keepdims=True)
    l1 = jnp.sum(l_ref[1], axis=-1, keepdims=True)
    o = acc_ref[0] / l0 - lam * (acc_ref[1] / l1)
    o = o * lax.rsqrt(jnp.mean(o * o, axis=-1, keepdims=True) + EPS)
    o_ref[...] = ((o * sw_ref[...]) * (1.0 - lambda_init)).astype(o_ref.dtype)


def _cast_rows(w, steps):
    k, n = w.shape
    for rows in range(BF16_ROWS, k + 1, BF16_ROWS):
        if k % rows == 0 and k // rows <= steps:
            return rows if rows * n * 4 <= CAST_BLOCK_BYTES else None
    return None


def _plan_casts(weights, steps, lin):
    rows = [None if w.dtype == BF16 else _cast_rows(w, steps) for w in weights]
    arrays, specs, spans = [], [], []
    at_end = False
    for w, r in zip(weights, rows):
        if not r:
            continue
        nb = w.shape[0] // r
        first = steps - nb if at_end else 0
        at_end = not at_end if nb < steps else at_end
        arrays.append(w)
        spans.append((first, nb))
        specs.append(pl.BlockSpec(
            (r, w.shape[1]),
            lambda *ids, first=first, nb=nb: (jnp.clip(lin(*ids) - first, 0, nb - 1), 0)))

    def finish(outputs):
        done = iter(outputs)
        return [next(done) if r else w.astype(BF16) for w, r in zip(weights, rows)]

    return arrays, specs, spans, finish


def _run_casts(step, spans, cast_in, cast_out):
    for (first, nb), src, dst in zip(spans, cast_in, cast_out):
        @pl.when((step >= first) & (step < first + nb))
        def _():
            dst[...] = src[...].astype(dst.dtype)


def diff_attention(qkv, lq1, lk1, lq2, lk2, subln_w, lambda_init, batch, seq_len, *, tq=512,
                   cast_weights=()):
    m, n = qkv.shape
    d_model = n // 3
    hw = 2 * DIFF_HEAD_DIM
    heads = d_model // hw
    tq = _pick(seq_len, tq)
    nq = seq_len // tq

    def lin(b, h, qi):
        return (b * heads + h) * nq + qi

    cast_arrays, cast_specs, spans, finish = _plan_casts(cast_weights, batch * heads * nq, lin)
    n_cast = len(cast_arrays)

    def kern(*refs):
        main_in, cast_in = refs[:8], refs[8:8 + n_cast]
        o_ref = refs[8 + n_cast]
        cast_out = refs[9 + n_cast:9 + 2 * n_cast]
        scratch = refs[9 + 2 * n_cast:]
        _run_casts(lin(pl.program_id(0), pl.program_id(1), pl.program_id(2)), spans, cast_in, cast_out)
        _diff_attn_kernel(*main_in, o_ref, *scratch, lambda_init=lambda_init, tq=tq)

    vec = lambda a: a.reshape(1, -1)
    lam_spec = pl.BlockSpec((1, DIFF_HEAD_DIM), lambda b, h, qi: (0, 0))
    outs = pl.pallas_call(
        kern,
        grid=(batch, heads, nq),
        in_specs=[
            pl.BlockSpec((tq, hw), lambda b, h, qi: (b * nq + qi, h)),
            pl.BlockSpec((seq_len, hw), lambda b, h, qi: (b, heads + h)),
            pl.BlockSpec((seq_len, hw), lambda b, h, qi: (b, 2 * heads + h)),
            lam_spec, lam_spec, lam_spec, lam_spec,
            pl.BlockSpec((1, hw), lambda b, h, qi: (0, 0)),
        ] + cast_specs,
        out_specs=[pl.BlockSpec((tq, hw), lambda b, h, qi: (b * nq + qi, h))] + cast_specs,
        out_shape=[jax.ShapeDtypeStruct((m, d_model), BF16)]
        + [jax.ShapeDtypeStruct(w.shape, BF16) for w in cast_arrays],
        scratch_shapes=[pltpu.VMEM((2, tq, LANES), F32), pltpu.VMEM((2, tq, LANES), F32),
                        pltpu.VMEM((2, tq, hw), F32)],
        compiler_params=_params("arbitrary", "arbitrary", "arbitrary"),
        name="diff_attention",
    )(qkv, qkv, qkv, vec(lq1), vec(lk1), vec(lq2), vec(lk2), vec(subln_w), *cast_arrays)
    return outs[0], finish(outs[1:])


def _dot_exact(a, b):
    return jnp.dot(a, b, preferred_element_type=F32, precision=lax.Precision.HIGHEST)


def _ssd_kernel(z_ref, x_ref, xh_ref, b_ref, bh_ref, c_ref, ch_ref, dt_ref,
                cwx_ref, cwb_ref, cwc_ref, cbx_ref, cbb_ref, cbc_ref,
                dtb_ref, alog_ref, dexp_ref, nw_ref, sel_ref, expand_ref, o_ref, state_ref, *, heads_per_group):
    g = pl.program_id(1)
    c = pl.program_id(2)
    t = CHUNK
    hpg = heads_per_group
    p = SSM_HEADDIM
    gw = hpg * p
    nh = dt_ref.shape[1]

    @pl.when(c == 0)
    def _():
        state_ref[...] = jnp.zeros(state_ref.shape, F32)

    seq_start = c == 0

    def conv_silu(cur_ref, halo_ref, w_ref, bias_ref):
        halo = halo_ref[...]
        halo = jnp.where(seq_start, jnp.zeros_like(halo), halo)
        ext = jnp.concatenate([halo, cur_ref[...]], axis=0)
        return _silu(_causal_conv_ext(ext, SUBLANES, w_ref[...], bias_ref[...]))

    xc = conv_silu(x_ref, xh_ref, cwx_ref, cbx_ref)
    bc = conv_silu(b_ref, bh_ref, cwb_ref, cbb_ref)
    cc = conv_silu(c_ref, ch_ref, cwc_ref, cbc_ref)

    dtr = dt_ref[...] + dtb_ref[...]
    dtv = jnp.maximum(dtr, 0.0) + jnp.log1p(jnp.exp(-jnp.abs(dtr)))
    a_all = dtv * (-jnp.exp(alog_ref[...]))

    row_t = lax.broadcasted_iota(jnp.int32, (t, t), 0)
    col_t = lax.broadcasted_iota(jnp.int32, (t, t), 1)
    causal = col_t <= row_t
    a_cs_all = _dot_exact(jnp.where(causal, 1.0, 0.0), a_all)
    a_last = a_cs_all[t - 1:t, :]

    a_cs = _dot_exact(a_cs_all, sel_ref[...])
    a_cs_t = a_cs.T

    stacked = jnp.concatenate([dtv, jnp.exp(a_cs_all), jnp.exp(a_last - a_cs_all)], axis=0)
    hi = stacked.astype(BF16)
    lo = (stacked - hi.astype(F32)).astype(BF16)
    expanded = jnp.dot(jnp.concatenate([hi, lo], axis=1), expand_ref[...],
                       preferred_element_type=F32)
    dt_e = expanded[0:t]
    exp_acs_e = expanded[t:2 * t]
    dec_e = expanded[2 * t:3 * t]

    xdt = xc * dt_e
    xdt_b = xdt.astype(BF16)
    bb = bc.astype(BF16)
    cb = cc.astype(BF16)
    cbm = lax.dot_general(cb, bb, (((1,), (1,)), ((), ())), preferred_element_type=F32)

    state = state_ref[...]
    y = jnp.dot(cb, state.astype(BF16), preferred_element_type=F32) * exp_acs_e

    lane = lax.broadcasted_iota(jnp.int32, (t, 2 * p), 1)
    pieces = []
    for pair in range(hpg // 2):
        x_pair = xdt_b[:, pair * 2 * p:(pair + 1) * 2 * p]
        outs = []
        for hh in (2 * pair, 2 * pair + 1):
            seg = a_cs[:, hh:hh + 1] - a_cs_t[hh:hh + 1, :]
            decay = jnp.exp(jnp.where(causal, seg, -jnp.inf))
            outs.append(jnp.dot((cbm * decay).astype(BF16), x_pair, preferred_element_type=F32))
        pieces.append(jnp.where(lane < p, outs[0], outs[1]))
    y = y + jnp.concatenate(pieces, axis=1) + dexp_ref[...] * xc

    new_contrib = jnp.dot(bc.T.astype(BF16), (xdt * dec_e).astype(BF16), preferred_element_type=F32)
    state_ref[...] = state * exp_acs_e[t - 1:t, :] + new_contrib

    gn = y * _silu(z_ref[...])
    gn = gn * lax.rsqrt(jnp.mean(gn * gn, axis=-1, keepdims=True) + EPS)
    o_ref[...] = (gn * nw_ref[...]).astype(o_ref.dtype)


def ssd_mixer(zx, bcm, dt, conv_w, conv_b, dt_bias, a_log, d_skip, norm_w, batch, seq_len, cast_weights=()):
    m = zx.shape[0]
    d_inner = zx.shape[1] // 2
    nh = dt.shape[1]
    hpg = nh // SSM_GROUPS
    gw = d_inner // SSM_GROUPS
    n = SSM_STATE
    t = CHUNK
    nc = seq_len // t
    hb = t // SUBLANES
    ng = SSM_GROUPS
    xb0 = d_inner // gw
    cwb0 = d_inner // n

    def row(b, g, c):
        return b * nc + c

    def halo(b, g, c):
        return jnp.maximum((b * nc + c) * hb - 1, 0)

    d_exp = jnp.repeat(d_skip.astype(F32), SSM_HEADDIM).reshape(1, d_inner)
    conv_b2 = conv_b.reshape(1, -1)
    head = jnp.arange(nh)
    sel = (head[None, :, None] == (jnp.arange(ng) * hpg)[:, None, None] + head[None, None, :]) \
        & (head[None, None, :] < hpg)
    chan_head = jnp.arange(gw) // SSM_HEADDIM
    expand = head[None, :, None] == (jnp.arange(ng) * hpg)[:, None, None] + chan_head[None, None, :]
    sel = sel.astype(F32)
    expand = jnp.concatenate([expand, expand], axis=1).astype(BF16)
    def lin(b, g, c):
        return (b * ng + g) * nc + c

    cast_arrays, cast_specs, spans, finish = _plan_casts(cast_weights, batch * ng * nc, lin)
    n_cast = len(cast_arrays)
    n_main = 20

    def kern(*refs):
        main_in, cast_in = refs[:n_main], refs[n_main:n_main + n_cast]
        o_ref = refs[n_main + n_cast]
        cast_out = refs[n_main + n_cast + 1:n_main + 2 * n_cast + 1]
        scratch = refs[n_main + 2 * n_cast + 1:]
        _run_casts(lin(pl.program_id(0), pl.program_id(1), pl.program_id(2)), spans, cast_in, cast_out)
        _ssd_kernel(*main_in, o_ref, *scratch, heads_per_group=hpg)

    outs = pl.pallas_call(
        kern,
        grid=(batch, ng, nc),
        in_specs=[
            pl.BlockSpec((t, gw), lambda b, g, c: (row(b, g, c), g)),
            pl.BlockSpec((t, gw), lambda b, g, c: (row(b, g, c), xb0 + g)),
            pl.BlockSpec((SUBLANES, gw), lambda b, g, c: (halo(b, g, c), xb0 + g)),
            pl.BlockSpec((t, n), lambda b, g, c: (row(b, g, c), g)),
            pl.BlockSpec((SUBLANES, n), lambda b, g, c: (halo(b, g, c), g)),
            pl.BlockSpec((t, n), lambda b, g, c: (row(b, g, c), ng + g)),
            pl.BlockSpec((SUBLANES, n), lambda b, g, c: (halo(b, g, c), ng + g)),
            pl.BlockSpec((t, nh), lambda b, g, c: (row(b, g, c), 0)),
            pl.BlockSpec((SSM_CONV, gw), lambda b, g, c: (0, g)),
            pl.BlockSpec((SSM_CONV, n), lambda b, g, c: (0, cwb0 + g)),
            pl.BlockSpec((SSM_CONV, n), lambda b, g, c: (0, cwb0 + ng + g)),
            pl.BlockSpec((1, gw), lambda b, g, c: (0, g)),
            pl.BlockSpec((1, n), lambda b, g, c: (0, cwb0 + g)),
            pl.BlockSpec((1, n), lambda b, g, c: (0, cwb0 + ng + g)),
            pl.BlockSpec((1, nh), lambda b, g, c: (0, 0)),
            pl.BlockSpec((1, nh), lambda b, g, c: (0, 0)),
            pl.BlockSpec((1, gw), lambda b, g, c: (0, g)),
            pl.BlockSpec((1, gw), lambda b, g, c: (0, g)),
            pl.BlockSpec((None, nh, nh), lambda b, g, c: (g, 0, 0)),
            pl.BlockSpec((None, 2 * nh, gw), lambda b, g, c: (g, 0, 0)),
        ] + cast_specs,
        out_specs=[pl.BlockSpec((t, gw), lambda b, g, c: (row(b, g, c), g))] + cast_specs,
        out_shape=[jax.ShapeDtypeStruct((m, d_inner), BF16)]
        + [jax.ShapeDtypeStruct(w.shape, BF16) for w in cast_arrays],
        scratch_shapes=[pltpu.VMEM((n, gw), F32)],
        compiler_params=_params("arbitrary", "arbitrary", "arbitrary"),
        name="ssd_mixer",
    )(zx, zx, zx, bcm, bcm, bcm, bcm, dt, conv_w, conv_w, conv_w, conv_b2, conv_b2, conv_b2,
      dt_bias.reshape(1, nh), a_log.reshape(1, nh), d_exp, norm_w.reshape(1, d_inner), sel, expand,
      *cast_arrays)
    return outs[0], finish(outs[1:])


def mamba_layer(xf, act, w_in, conv_w, conv_b, dt_bias, a_log, d_skip, norm_w, w_out, batch, seq_len,
                next_norm_w, cast_in_proj=(), cast_ssd=(), cast_out_proj=()):
    u, rs = act
    nh = dt_bias.shape[0]
    d_inner = norm_w.shape[0]
    bc_cols = 2 * SSM_GROUPS * SSM_STATE
    w_in = w_in.astype(BF16)
    zx, c_in = matmul(u, w_in, col0=0, n=2 * d_inner, tm=1024, tn=1024, row_ssq=rs, single_buffer_a=True,
                      cast_weights=list(cast_in_proj))
    bcm = matmul(u, w_in, col0=2 * d_inner, n=bc_cols, tm=1024, tn=1024, row_ssq=rs)
    dt = matmul(u, w_in, col0=2 * d_inner + bc_cols, n=nh, tm=1024, tn=nh, row_ssq=rs)
    yn, c_ssd = ssd_mixer(zx, bcm, dt, conv_w, conv_b, dt_bias, a_log, d_skip, norm_w, batch, seq_len,
                          cast_weights=(w_out,) + tuple(cast_ssd))
    xf, (xw, ssq), c_out = matmul(yn, c_ssd[0], tm=1024, tn=512, residual=xf, single_buffer_a=True,
                                  norm_w=next_norm_w, cast_weights=list(cast_out_proj))
    return xf, (xw, (ssq, xf.shape[1])), c_in, c_ssd[1:], c_out


def attention_layer(xf, act, cos, sin, w_qkv, lq1, lk1, lq2, lk2, subln_w, w_o, lambda_init, batch, seq_len,
                    next_norm_w, cast_weights=()):
    u, rs = act
    d_model = w_o.shape[0]
    tn = _pick(d_model, 1024)
    nb = w_qkv.shape[1] // tn
    qkv = pipelined_matmul(u, w_qkv.astype(BF16), [(tn, 0)],
                           epilogue=functools.partial(_epilogue_rope, rope_tiles=2 * d_model // tn),
                           row_aux=(cos, sin), out_cols=tn, out_dtype=BF16, nb=nb, row_ssq=rs,
                           single_buffer_a=True, name="qkv_rope")
    o, casted = diff_attention(qkv, lq1, lk1, lq2, lk2, subln_w, lambda_init, batch, seq_len,
                               cast_weights=cast_weights)
    xf, (xw, ssq) = matmul(o, w_o.astype(BF16), tm=1024, tn=512, residual=xf, norm_w=next_norm_w)
    return xf, (xw, (ssq, d_model)), casted


def ffn_layer(xf, act, w_up, conv_w, conv_b, w_down, seq_len, next_norm_w, cast_down_proj=()):
    u, rs = act
    d_ff = w_up.shape[1] // 2
    tn = _pick(d_ff, 256)
    nb = d_ff // tn
    cw = conv_w.astype(F32).reshape(FFN_CONV, 2, nb, tn).transpose(2, 0, 1, 3).reshape(nb, FFN_CONV, 2 * tn)
    cb = conv_b.astype(F32).reshape(2, nb, tn).transpose(1, 0, 2).reshape(nb, 1, 2 * tn)
    g = pipelined_matmul(u, w_up.astype(BF16), [(tn, 0), (tn, nb)], epilogue=_epilogue_conv_gate,
                         col_aux=(cw, cb), out_cols=tn, out_dtype=BF16, nb=nb, halo_seq_len=seq_len,
                         tm=2048, lagged=False, single_buffer_a=True, row_ssq=rs, name="ffn_up")
    outs = matmul(g, w_down.astype(BF16), tm=1024, tn=256, residual=xf, single_buffer_a=True,
                  norm_w=next_norm_w, cast_weights=list(cast_down_proj))
    if next_norm_w is None:
        return outs[0], None, outs[1]
    xf, (xw, ssq), c_down = outs
    return xf, (xw, (ssq, xf.shape[1])), c_down


def kernel(x, positions, l0_norm_mix, l0_m_w_in, l0_m_conv_w, l0_m_conv_b, l0_m_dt_bias, l0_m_a_log, l0_m_d, l0_m_norm, l0_m_w_out, l0_norm_ffn, l0_f_w_up, l0_f_conv_w, l0_f_conv_b, l0_f_w_down, l1_norm_mix, l1_a_w_qkv, l1_a_lq1, l1_a_lk1, l1_a_lq2, l1_a_lk2, l1_a_subln, l1_a_w_o, l1_norm_ffn, l1_f_w_up, l1_f_conv_w, l1_f_conv_b, l1_f_w_down, l2_norm_mix, l2_m_w_in, l2_m_conv_w, l2_m_conv_b, l2_m_dt_bias, l2_m_a_log, l2_m_d, l2_m_norm, l2_m_w_out, l2_norm_ffn, l2_f_w_up, l2_f_conv_w, l2_f_conv_b, l2_f_w_down, l3_norm_mix, l3_a_w_qkv, l3_a_lq1, l3_a_lk1, l3_a_lq2, l3_a_lk2, l3_a_subln, l3_a_w_o, l3_norm_ffn, l3_f_w_up, l3_f_conv_w, l3_f_conv_b, l3_f_w_down, final_norm):
    norm_mix = [l0_norm_mix, l1_norm_mix, l2_norm_mix, l3_norm_mix]
    norm_ffn = [l0_norm_ffn, l1_norm_ffn, l2_norm_ffn, l3_norm_ffn]
    mixer_params = [
        (l0_m_w_in, l0_m_conv_w, l0_m_conv_b, l0_m_dt_bias, l0_m_a_log, l0_m_d, l0_m_norm, l0_m_w_out),
        (l1_a_w_qkv, l1_a_lq1, l1_a_lk1, l1_a_lq2, l1_a_lk2, l1_a_subln, l1_a_w_o),
        (l2_m_w_in, l2_m_conv_w, l2_m_conv_b, l2_m_dt_bias, l2_m_a_log, l2_m_d, l2_m_norm, l2_m_w_out),
        (l3_a_w_qkv, l3_a_lq1, l3_a_lk1, l3_a_lq2, l3_a_lk2, l3_a_subln, l3_a_w_o),
    ]
    ffn_params = [
        (l0_f_w_up, l0_f_conv_w, l0_f_conv_b, l0_f_w_down),
        (l1_f_w_up, l1_f_conv_w, l1_f_conv_b, l1_f_w_down),
        (l2_f_w_up, l2_f_conv_w, l2_f_conv_b, l2_f_w_down),
        (l3_f_w_up, l3_f_conv_w, l3_f_conv_b, l3_f_w_down),
    ]
    batch, seq_len, d_model = x.shape
    xf = x.reshape(batch * seq_len, d_model)
    cos, sin = rope_tables(positions)
    n_layers = len(norm_mix)
    act = (rmsnorm(xf, norm_mix[0], BF16), None)
    for i in range(n_layers):
        w_up, cw, cb, w_down = ffn_params[i]
        nxt = list(mixer_params[i + 1]) if i + 1 < n_layers else None
        if i % N_MIXERS == 0:
            xf, act, (w_up,), (w_down,), c_out = mamba_layer(
                xf, act, *mixer_params[i], batch, seq_len, norm_ffn[i], cast_in_proj=[w_up],
                cast_ssd=[w_down], cast_out_proj=[nxt[0]] if nxt else [])
            if nxt:
                nxt[0] = c_out[0]
            cast_down = [nxt[-1], ffn_params[i + 1][0]] if nxt else []
        else:
            lambda_init = 0.8 - 0.6 * math.exp(-0.3 * i)
            xf, act, (w_up, w_down) = attention_layer(xf, act, cos, sin, *mixer_params[i], lambda_init, batch,
                                                      seq_len, norm_ffn[i], cast_weights=[w_up, w_down])
            cast_down = [nxt[0]] if nxt else []
        xf, act, c_down = ffn_layer(xf, act, w_up, cw, cb, w_down, seq_len,
                                    norm_mix[i + 1] if nxt else None, cast_down_proj=cast_down)
        if nxt:
            nxt[-1 if i % N_MIXERS == 0 else 0] = c_down[0]
            mixer_params[i + 1] = tuple(nxt)
            if len(c_down) > 1:
                ffn_params[i + 1] = (c_down[1],) + tuple(ffn_params[i + 1][1:])
    return rmsnorm(xf, final_norm, F32).reshape(batch, seq_len, d_model)
```

```python
import functools
import math

import jax
import jax.numpy as jnp
from jax import lax
from jax.experimental import pallas as pl
from jax.experimental.pallas import tpu as pltpu

F32 = jnp.float32
BF16 = jnp.bfloat16

EPS = 1e-5
N_MIXERS = 2

SSM_HEADDIM = 64
SSM_GROUPS = 8
SSM_STATE = 128
SSM_CONV = 4
CHUNK = 128

DIFF_HEAD_DIM = 128
ATTN_SCALE = DIFF_HEAD_DIM ** -0.5
Q_PRESCALE = ATTN_SCALE * math.log2(math.e)
ROPE_THETA = 10000.0

FFN_CONV = 3

V7X_VMEM_BYTES = 64 * 1024 * 1024
VMEM_LIMIT_BYTES = V7X_VMEM_BYTES - 8 * 1024 * 1024
SUBLANES = 8
BF16_ROWS = 16
LANES = 128
MXU_COLS = 256
CAST_BLOCK_BYTES = 4 * 1024 * 1024
EPILOGUE_VREGS = 32


def _params(*sem):
    return pltpu.CompilerParams(dimension_semantics=sem, vmem_limit_bytes=VMEM_LIMIT_BYTES)


def _pick(dim, pref):
    if dim <= pref:
        return dim
    t = pref
    while dim % t:
        t //= 2
    return t


def _silu(x):
    return x / (1.0 + jnp.exp(-x))


def _rmsnorm_kernel(x_ref, w_ref, o_ref):
    x = x_ref[...]
    y = x * lax.rsqrt(jnp.mean(x * x, axis=-1, keepdims=True) + EPS)
    o_ref[...] = (y * w_ref[...]).astype(o_ref.dtype)


def rmsnorm(x, w, out_dtype):
    m, d = x.shape
    tm = _pick(m, 256)
    return pl.pallas_call(
        _rmsnorm_kernel,
        grid=(m // tm,),
        in_specs=[pl.BlockSpec((tm, d), lambda i: (i, 0)),
                  pl.BlockSpec((1, d), lambda i: (0, 0))],
        out_specs=pl.BlockSpec((tm, d), lambda i: (i, 0)),
        out_shape=jax.ShapeDtypeStruct((m, d), out_dtype),
        compiler_params=_params("arbitrary"),
        name="rmsnorm",
    )(x, w.reshape(1, d))


def _row_rstd(ssq, d):
    return lax.rsqrt(jnp.sum(ssq, axis=-1, keepdims=True) / d + EPS)


def _mm_kernel(*refs, has_residual, has_norm, scale_d, n_cast):
    it = iter(refs)
    a_ref, w_ref = next(it), next(it)
    r_ref = next(it) if has_residual else None
    nw_ref = next(it) if has_norm else None
    ssq_in_ref = next(it) if scale_d else None
    cast_in = [next(it) for _ in range(n_cast)]
    o_ref = next(it)
    xw_ref, ssq_ref = (next(it), next(it)) if has_norm else (None, None)
    cast_out = [next(it) for _ in range(n_cast)]
    for src, dst in zip(cast_in, cast_out):
        dst[...] = src[...].astype(dst.dtype)
    acc = jnp.dot(a_ref[...], w_ref[...], preferred_element_type=F32)
    if scale_d:
        acc = acc * _row_rstd(ssq_in_ref[...], scale_d)
    if has_residual:
        acc = r_ref[...] + acc
    o_ref[...] = acc.astype(o_ref.dtype)
    if has_norm:
        xw_ref[...] = (acc * nw_ref[...]).astype(xw_ref.dtype)
        sq = acc * acc
        part = functools.reduce(jnp.add, [sq[:, t * LANES:(t + 1) * LANES] for t in range(sq.shape[1] // LANES)])

        @pl.when(pl.program_id(1) == 0)
        def _():
            ssq_ref[...] = jnp.zeros(ssq_ref.shape, F32)

        ssq_ref[...] += part


def matmul(a, w, *, col0=0, n=None, tm, tn, out_dtype=F32, residual=None, single_buffer_a=False,
           cast_weights=None, row_ssq=None, norm_w=None):
    want_casts = cast_weights is not None
    cast_weights = cast_weights or ()
    m, k = a.shape
    n = w.shape[1] - col0 if n is None else n
    tm = _pick(m, tm)
    tn = _pick(n, tn)
    assert col0 % tn == 0
    jb = col0 // tn
    nj = n // tn
    steps = (m // tm) * nj
    a_kwargs = dict(pipeline_mode=pl.Buffered(1)) if single_buffer_a else {}
    in_specs = [pl.BlockSpec((tm, k), lambda i, j: (i, 0), **a_kwargs),
                pl.BlockSpec((k, tn), lambda i, j: (0, j + jb))]
    args = [a, w]
    tile = pl.BlockSpec((tm, tn), lambda i, j: (i, j))
    rows = pl.BlockSpec((tm, LANES), lambda i, j: (i, 0))
    if residual is not None:
        in_specs.append(tile)
        args.append(residual)
    if norm_w is not None:
        in_specs.append(pl.BlockSpec((1, tn), lambda i, j: (0, j)))
        args.append(norm_w.reshape(1, n).astype(F32))
    if row_ssq is not None:
        in_specs.append(rows)
        args.append(row_ssq[0])
    in_kernel = [c.shape[0] % (steps * BF16_ROWS) == 0
                 and (c.shape[0] // steps) * c.shape[1] * 4 <= CAST_BLOCK_BYTES for c in cast_weights]
    cast_arrays = [c for c, ok in zip(cast_weights, in_kernel) if ok]
    cast_specs = [pl.BlockSpec((c.shape[0] // steps, c.shape[1]), lambda i, j: (i * nj + j, 0))
                  for c in cast_arrays]
    norm_specs = [tile, rows] if norm_w is not None else []
    norm_shapes = ([jax.ShapeDtypeStruct((m, n), BF16), jax.ShapeDtypeStruct((m, LANES), F32)]
                   if norm_w is not None else [])
    outs = pl.pallas_call(
        functools.partial(_mm_kernel, has_residual=residual is not None, has_norm=norm_w is not None,
                          scale_d=row_ssq[1] if row_ssq is not None else 0, n_cast=len(cast_arrays)),
        grid=(m // tm, nj),
        in_specs=in_specs + cast_specs,
        out_specs=[tile] + norm_specs + cast_specs,
        out_shape=[jax.ShapeDtypeStruct((m, n), out_dtype)] + norm_shapes
        + [jax.ShapeDtypeStruct(c.shape, BF16) for c in cast_arrays],
        compiler_params=_params("arbitrary", "arbitrary"),
        name="matmul",
    )(*args, *cast_arrays)
    result = [outs[0]]
    n_norm = len(norm_specs)
    if norm_w is not None:
        result.append((outs[1], outs[2]))
    if want_casts:
        done = iter(outs[1 + n_norm:])
        result.append([next(done) if ok else c.astype(BF16) for c, ok in zip(cast_weights, in_kernel)])
    return result[0] if len(result) == 1 else tuple(result)


def _shift_rows(ext, s, halo_rows):
    if s == 0:
        return ext[halo_rows:]
    return pltpu.roll(ext, s, axis=0)[halo_rows:]


def _causal_conv_ext(ext, halo_rows, w, b):
    k_width = w.shape[0]
    y = b + w[0:1] * _shift_rows(ext, k_width - 1, halo_rows)
    for k in range(1, k_width):
        y = y + w[k:k + 1] * _shift_rows(ext, k_width - 1 - k, halo_rows)
    return y


def _row_chunks(rows, cols):
    rc = max(BF16_ROWS, EPILOGUE_VREGS * SUBLANES * LANES // cols)
    rc = min(rc, rows)
    return [(r0, rc) for r0 in range(0, rows, rc)]


def _epilogue_conv_gate(h_ref, halo_rows, aux, o_ref, jp):
    cw, cb = aux[0][...], aux[1][...]
    tn = o_ref.shape[1]

    def chunk(r0, rc):
        ext = h_ref[pl.ds(halo_rows + r0 - SUBLANES, rc + SUBLANES), :]
        y = _causal_conv_ext(ext, SUBLANES, cw, cb)
        o_ref[r0:r0 + rc, :] = (_silu(y[:, :tn]) * y[:, tn:]).astype(o_ref.dtype)

    return [functools.partial(chunk, r0, rc) for r0, rc in _row_chunks(o_ref.shape[0], 2 * tn)]


def _epilogue_rope(h_ref, halo_rows, aux, o_ref, jp, *, rope_tiles):
    cos_ref, sin_ref = aux
    d = DIFF_HEAD_DIM
    is_rope = jp < rope_tiles
    scale = jnp.where(jp < rope_tiles // 2, Q_PRESCALE, 1.0)

    def chunk(r0, rc):
        cos = jnp.where(is_rope, cos_ref[r0:r0 + rc, :] * scale, 1.0)
        sin = jnp.where(is_rope, sin_ref[r0:r0 + rc, :] * scale, 0.0)
        for c in range(o_ref.shape[1] // d):
            t = h_ref[r0:r0 + rc, c * d:(c + 1) * d]
            rot = pltpu.roll(t, d // 2, axis=1)
            o_ref[r0:r0 + rc, c * d:(c + 1) * d] = (t * cos + rot * sin).astype(o_ref.dtype)

    return [functools.partial(chunk, r0, rc) for r0, rc in _row_chunks(o_ref.shape[0], 4 * d)]


def _pipelined_kernel(*refs, epilogue, n_w, n_aux, use_halo, lagged, scale_d, nb, n_tiles, tiles_per_seq):
    it = iter(refs)
    a_ref = next(it)
    ah_ref = next(it) if use_halo else None
    w_refs = [next(it) for _ in range(n_w)]
    ssq_ref = next(it) if scale_d else None
    ssqh_ref = next(it) if (scale_d and use_halo) else None
    aux = [next(it) for _ in range(n_aux)]
    o_ref = next(it)
    aext_ref = next(it) if use_halo else None
    h = [next(it) for _ in range(2 if lagged else 1)]
    halo_rows = BF16_ROWS if use_halo else 0

    s = pl.program_id(0)
    t = jnp.minimum(s, n_tiles - 1)
    i = t // nb
    j = t % nb

    if use_halo:
        @pl.when((j == 0) & (s < n_tiles))
        def _():
            seq_start = (i % tiles_per_seq) == 0
            halo = ah_ref[...]
            aext_ref[0:BF16_ROWS, :] = jnp.where(seq_start, jnp.zeros_like(halo), halo)
            aext_ref[BF16_ROWS:, :] = a_ref[...]

    def matmul_into(h_ref):
        lhs_ref = aext_ref if use_halo else a_ref
        rstd = None
        if scale_d:
            rstd = _row_rstd(ssq_ref[...], scale_d)
            if use_halo:
                rstd = jnp.concatenate([_row_rstd(ssqh_ref[...], scale_d), rstd], axis=0)
        col = 0
        for w_ref in w_refs:
            wn = w_ref.shape[1]
            step_cols = min(wn, MXU_COLS)
            for c0 in range(0, wn, step_cols):
                prod = jnp.dot(lhs_ref[...], w_ref[:, c0:c0 + step_cols], preferred_element_type=F32)
                h_ref[:, col + c0:col + c0 + step_cols] = prod if rstd is None else prod * rstd
            col += wn

    if not lagged:
        matmul_into(h[0])
        for chunk in epilogue(h[0], halo_rows, aux, o_ref, j):
            chunk()
        return

    jp = jnp.maximum(s - 1, 0) % nb
    slot = s % 2

    @pl.when(s == 0)
    def _():
        h[1][...] = jnp.zeros(h[1].shape, F32)

    def step(cur, prev):
        for chunk in epilogue(h[prev], halo_rows, aux, o_ref, jp):
            chunk()
        matmul_into(h[cur])

    @pl.when(slot == 0)
    def _():
        step(0, 1)

    @pl.when(slot == 1)
    def _():
        step(1, 0)


def pipelined_matmul(a, w, w_col_blocks, *, epilogue, col_aux=(), row_aux=(), out_cols, out_dtype,
                     nb, halo_seq_len=None, tm=1024, single_buffer_a=True, lagged=True, row_ssq=None,
                     name):
    m, k = a.shape
    use_halo = halo_seq_len is not None
    tm = _pick(halo_seq_len if use_halo else m, tm)
    hb = tm // BF16_ROWS
    n_tiles = (m // tm) * nb
    hrows = tm + (BF16_ROWS if use_halo else 0)
    wn = sum(width for width, _ in w_col_blocks)

    def cur(s):
        t = jnp.minimum(s, n_tiles - 1)
        return t // nb, t % nb

    def prev(s):
        t = jnp.maximum(s - 1, 0) if lagged else s
        return t // nb, t % nb

    a_kwargs = dict(pipeline_mode=pl.Buffered(1)) if single_buffer_a else {}
    in_specs = [pl.BlockSpec((tm, k), lambda s: (cur(s)[0], 0), **a_kwargs)]
    args = [a]
    if use_halo:
        in_specs.append(pl.BlockSpec((BF16_ROWS, k), lambda s: (jnp.maximum(cur(s)[0] * hb - 1, 0), 0)))
        args.append(a)
    for width, first in w_col_blocks:
        in_specs.append(pl.BlockSpec((k, width), lambda s, first=first: (0, first + cur(s)[1])))
        args.append(w)
    if row_ssq is not None:
        in_specs.append(pl.BlockSpec((tm, LANES), lambda s: (cur(s)[0], 0)))
        args.append(row_ssq[0])
        if use_halo:
            in_specs.append(pl.BlockSpec((BF16_ROWS, LANES), lambda s: (jnp.maximum(cur(s)[0] * hb - 1, 0), 0)))
            args.append(row_ssq[0])
    for arr in col_aux:
        in_specs.append(pl.BlockSpec((None,) + arr.shape[1:], lambda s: (prev(s)[1], 0, 0)))
        args.append(arr)
    for arr in row_aux:
        in_specs.append(pl.BlockSpec((tm, arr.shape[1]), lambda s: (prev(s)[0], 0)))
        args.append(arr)
    scratch = ([pltpu.VMEM((hrows, k), BF16)] if use_halo else [])
    scratch += [pltpu.VMEM((hrows, wn), F32)] * (2 if lagged else 1)
    kern = functools.partial(
        _pipelined_kernel, epilogue=epilogue, n_w=len(w_col_blocks), n_aux=len(col_aux) + len(row_aux),
        use_halo=use_halo, lagged=lagged, scale_d=row_ssq[1] if row_ssq is not None else 0, nb=nb,
        n_tiles=n_tiles,
        tiles_per_seq=(halo_seq_len // tm) if use_halo else 1)
    return pl.pallas_call(
        kern,
        grid=(n_tiles + (1 if lagged else 0),),
        in_specs=in_specs,
        out_specs=pl.BlockSpec((tm, out_cols), lambda s: prev(s)),
        out_shape=jax.ShapeDtypeStruct((m, nb * out_cols), out_dtype),
        scratch_shapes=scratch,
        compiler_params=_params("arbitrary"),
        name=name,
    )(*args)


def _rope_table_kernel(pos_ref, inv_ref, cos_ref, sin_ref):
    freqs = pos_ref[...].astype(F32) * inv_ref[...]
    lane = lax.broadcasted_iota(jnp.int32, freqs.shape, 1)
    sign = jnp.where(lane < DIFF_HEAD_DIM // 2, -1.0, 1.0)
    cos_ref[...] = jnp.cos(freqs)
    sin_ref[...] = jnp.sin(freqs) * sign


def rope_tables(positions):
    m = positions.size
    inv = 1.0 / (ROPE_THETA ** (jnp.arange(0, DIFF_HEAD_DIM, 2, dtype=F32) / DIFF_HEAD_DIM))
    inv_full = jnp.concatenate([inv, inv]).reshape(1, DIFF_HEAD_DIM)
    tm = _pick(m, 1024)
    shape = jax.ShapeDtypeStruct((m, DIFF_HEAD_DIM), F32)
    return pl.pallas_call(
        _rope_table_kernel,
        grid=(m // tm,),
        in_specs=[pl.BlockSpec((tm, 1), lambda i: (i, 0)),
                  pl.BlockSpec((1, DIFF_HEAD_DIM), lambda i: (0, 0))],
        out_specs=[pl.BlockSpec((tm, DIFF_HEAD_DIM), lambda i: (i, 0))] * 2,
        out_shape=[shape, shape],
        compiler_params=_params("arbitrary"),
        name="rope_tables",
    )(positions.reshape(m, 1), inv_full)


def _diff_attn_kernel(q_ref, k_ref, v_ref, lq1_ref, lk1_ref, lq2_ref, lk2_ref, sw_ref, o_ref,
                      m_ref, l_ref, acc_ref, *, lambda_init, tq):
    qi = pl.program_id(2)
    d = DIFF_HEAD_DIM
    q = q_ref[...]

    m_ref[...] = jnp.full(m_ref.shape, -jnp.inf, F32)
    l_ref[...] = jnp.zeros(l_ref.shape, F32)
    acc_ref[...] = jnp.zeros(acc_ref.shape, F32)

    def kv_block(kb, masked):
        start = pl.multiple_of(kb * tq, tq)
        k = k_ref[pl.ds(start, tq), :]
        v = v_ref[pl.ds(start, tq), :]
        scores = [lax.dot_general(q[:, c * d:(c + 1) * d], k[:, c * d:(c + 1) * d],
                                  (((1,), (1,)), ((), ())), preferred_element_type=F32) for c in range(2)]
        probs = []
        alphas = []
        for c in range(2):
            s = scores[c]
            if masked:
                row = lax.broadcasted_iota(jnp.int32, s.shape, 0)
                col = lax.broadcasted_iota(jnp.int32, s.shape, 1)
                s = jnp.where(col <= row, s, -jnp.inf)
            chunks = [s[:, t * LANES:(t + 1) * LANES] for t in range(tq // LANES)]
            cmax = functools.reduce(jnp.maximum, chunks)
            m_old = m_ref[c]
            m_new = jnp.maximum(m_old, jnp.max(cmax, axis=-1, keepdims=True))
            alpha = jnp.exp2(m_old - m_new)
            ps = [jnp.exp2(ch - m_new) for ch in chunks]
            l_ref[c] = alpha * l_ref[c] + functools.reduce(jnp.add, ps)
            m_ref[c] = m_new
            probs.append(jnp.concatenate(ps, axis=1).astype(v.dtype))
            alphas.append(alpha)
        pv = jnp.dot(jnp.concatenate(probs, axis=0), v, preferred_element_type=F32)
        for c in range(2):
            a2 = jnp.concatenate([alphas[c], alphas[c]], axis=1)
            acc_ref[c] = a2 * acc_ref[c] + pv[c * tq:(c + 1) * tq]

    def body(kb, carry):
        kv_block(kb, False)
        return carry

    lax.fori_loop(0, qi, body, 0)
    kv_block(qi, True)

    lam = (jnp.exp(jnp.sum(lq1_ref[...] * lk1_ref[...], axis=-1, keepdims=True))
           - jnp.exp(jnp.sum(lq2_ref[...] * lk2_ref[...], axis=-1, keepdims=True)) + lambda_init)
    l0 = jnp.sum(l_ref[0], axis=-1, keepdims=True)
    l1 = jnp.sum(l_ref[1], axis=-1, keepdims=True)
    o = acc_ref[0] / l0 - lam * (acc_ref[1] / l1)
    o = o * lax.rsqrt(jnp.mean(o * o, axis=-1, keepdims=True) + EPS)
    o_ref[...] = ((o * sw_ref[...]) * (1.0 - lambda_init)).astype(o_ref.dtype)


def _cast_rows(w, steps):
    k, n = w.shape
    for rows in range(BF16_ROWS, k + 1, BF16_ROWS):
        if k % rows == 0 and k // rows <= steps:
            return rows if rows * n * 4 <= CAST_BLOCK_BYTES else None
    return None


def _plan_casts(weights, steps, lin):
    rows = [None if w.dtype == BF16 else _cast_rows(w, steps) for w in weights]
    arrays, specs, spans = [], [], []
    at_end = False
    for w, r in zip(weights, rows):
        if not r:
            continue
        nb = w.shape[0] // r
        first = steps - nb if at_end else 0
        at_end = not at_end if nb < steps else at_end
        arrays.append(w)
        spans.append((first, nb))
        specs.append(pl.BlockSpec(
            (r, w.shape[1]),
            lambda *ids, first=first, nb=nb: (jnp.clip(lin(*ids) - first, 0, nb - 1), 0)))

    def finish(outputs):
        done = iter(outputs)
        return [next(done) if r else w.astype(BF16) for w, r in zip(weights, rows)]

    return arrays, specs, spans, finish


def _run_casts(step, spans, cast_in, cast_out):
    for (first, nb), src, dst in zip(spans, cast_in, cast_out):
        @pl.when((step >= first) & (step < first + nb))
        def _():
            dst[...] = src[...].astype(dst.dtype)


def diff_attention(qkv, lq1, lk1, lq2, lk2, subln_w, lambda_init, batch, seq_len, *, tq=512,
                   cast_weights=()):
    m, n = qkv.shape
    d_model = n // 3
    hw = 2 * DIFF_HEAD_DIM
    heads = d_model // hw
    tq = _pick(seq_len, tq)
    nq = seq_len // tq

    def lin(b, h, qi):
        return (b * heads + h) * nq + qi

    cast_arrays, cast_specs, spans, finish = _plan_casts(cast_weights, batch * heads * nq, lin)
    n_cast = len(cast_arrays)

    def kern(*refs):
        main_in, cast_in = refs[:8], refs[8:8 + n_cast]
        o_ref = refs[8 + n_cast]
        cast_out = refs[9 + n_cast:9 + 2 * n_cast]
        scratch = refs[9 + 2 * n_cast:]
        _run_casts(lin(pl.program_id(0), pl.program_id(1), pl.program_id(2)), spans, cast_in, cast_out)
        _diff_attn_kernel(*main_in, o_ref, *scratch, lambda_init=lambda_init, tq=tq)

    vec = lambda a: a.reshape(1, -1)
    lam_spec = pl.BlockSpec((1, DIFF_HEAD_DIM), lambda b, h, qi: (0, 0))
    outs = pl.pallas_call(
        kern,
        grid=(batch, heads, nq),
        in_specs=[
            pl.BlockSpec((tq, hw), lambda b, h, qi: (b * nq + qi, h)),
            pl.BlockSpec((seq_len, hw), lambda b, h, qi: (b, heads + h)),
            pl.BlockSpec((seq_len, hw), lambda b, h, qi: (b, 2 * heads + h)),
            lam_spec, lam_spec, lam_spec, lam_spec,
            pl.BlockSpec((1, hw), lambda b, h, qi: (0, 0)),
        ] + cast_specs,
        out_specs=[pl.BlockSpec((tq, hw), lambda b, h, qi: (b * nq + qi, h))] + cast_specs,
        out_shape=[jax.ShapeDtypeStruct((m, d_model), BF16)]
        + [jax.ShapeDtypeStruct(w.shape, BF16) for w in cast_arrays],
        scratch_shapes=[pltpu.VMEM((2, tq, LANES), F32), pltpu.VMEM((2, tq, LANES), F32),
                        pltpu.VMEM((2, tq, hw), F32)],
        compiler_params=_params("arbitrary", "arbitrary", "arbitrary"),
        name="diff_attention",
    )(qkv, qkv, qkv, vec(lq1), vec(lk1), vec(lq2), vec(lk2), vec(subln_w), *cast_arrays)
    return outs[0], finish(outs[1:])


def _dot_exact(a, b):
    return jnp.dot(a, b, preferred_element_type=F32, precision=lax.Precision.HIGHEST)


def _ssd_kernel(z_ref, x_ref, xh_ref, b_ref, bh_ref, c_ref, ch_ref, dt_ref,
                cwx_ref, cwb_ref, cwc_ref, cbx_ref, cbb_ref, cbc_ref,
                dtb_ref, alog_ref, dexp_ref, nw_ref, sel_ref, expand_ref, o_ref, state_ref, *, heads_per_group):
    g = pl.program_id(1)
    c = pl.program_id(2)
    t = CHUNK
    hpg = heads_per_group
    p = SSM_HEADDIM
    gw = hpg * p
    nh = dt_ref.shape[1]

    @pl.when(c == 0)
    def _():
        state_ref[...] = jnp.zeros(state_ref.shape, F32)

    seq_start = c == 0

    def conv_silu(cur_ref, halo_ref, w_ref, bias_ref):
        halo = halo_ref[...]
        halo = jnp.where(seq_start, jnp.zeros_like(halo), halo)
        ext = jnp.concatenate([halo, cur_ref[...]], axis=0)
        return _silu(_causal_conv_ext(ext, SUBLANES, w_ref[...], bias_ref[...]))

    xc = conv_silu(x_ref, xh_ref, cwx_ref, cbx_ref)
    bc = conv_silu(b_ref, bh_ref, cwb_ref, cbb_ref)
    cc = conv_silu(c_ref, ch_ref, cwc_ref, cbc_ref)

    dtr = dt_ref[...] + dtb_ref[...]
    dtv = jnp.maximum(dtr, 0.0) + jnp.log1p(jnp.exp(-jnp.abs(dtr)))
    a_all = dtv * (-jnp.exp(alog_ref[...]))

    row_t = lax.broadcasted_iota(jnp.int32, (t, t), 0)
    col_t = lax.broadcasted_iota(jnp.int32, (t, t), 1)
    causal = col_t <= row_t
    a_cs_all = _dot_exact(jnp.where(causal, 1.0, 0.0), a_all)
    a_last = a_cs_all[t - 1:t, :]

    a_cs = _dot_exact(a_cs_all, sel_ref[...])
    a_cs_t = a_cs.T

    stacked = jnp.concatenate([dtv, jnp.exp(a_cs_all), jnp.exp(a_last - a_cs_all)], axis=0)
    hi = stacked.astype(BF16)
    lo = (stacked - hi.astype(F32)).astype(BF16)
    expanded = jnp.dot(jnp.concatenate([hi, lo], axis=1), expand_ref[...],
                       preferred_element_type=F32)
    dt_e = expanded[0:t]
    exp_acs_e = expanded[t:2 * t]
    dec_e = expanded[2 * t:3 * t]

    xdt = xc * dt_e
    xdt_b = xdt.astype(BF16)
    bb = bc.astype(BF16)
    cb = cc.astype(BF16)
    cbm = lax.dot_general(cb, bb, (((1,), (1,)), ((), ())), preferred_element_type=F32)

    state = state_ref[...]
    y = jnp.dot(cb, state.astype(BF16), preferred_element_type=F32) * exp_acs_e

    lane = lax.broadcasted_iota(jnp.int32, (t, 2 * p), 1)
    pieces = []
    for pair in range(hpg // 2):
        x_pair = xdt_b[:, pair * 2 * p:(pair + 1) * 2 * p]
        outs = []
        for hh in (2 * pair, 2 * pair + 1):
            seg = a_cs[:, hh:hh + 1] - a_cs_t[hh:hh + 1, :]
            decay = jnp.exp(jnp.where(causal, seg, -jnp.inf))
            outs.append(jnp.dot((cbm * decay).astype(BF16), x_pair, preferred_element_type=F32))
        pieces.append(jnp.where(lane < p, outs[0], outs[1]))
    y = y + jnp.concatenate(pieces, axis=1) + dexp_ref[...] * xc

    new_contrib = jnp.dot(bc.T.astype(BF16), (xdt * dec_e).astype(BF16), preferred_element_type=F32)
    state_ref[...] = state * exp_acs_e[t - 1:t, :] + new_contrib

    gn = y * _silu(z_ref[...])
    gn = gn * lax.rsqrt(jnp.mean(gn * gn, axis=-1, keepdims=True) + EPS)
    o_ref[...] = (gn * nw_ref[...]).astype(o_ref.dtype)


def ssd_mixer(zx, bcm, dt, conv_w, conv_b, dt_bias, a_log, d_skip, norm_w, batch, seq_len, cast_weights=()):
    m = zx.shape[0]
    d_inner = zx.shape[1] // 2
    nh = dt.shape[1]
    hpg = nh // SSM_GROUPS
    gw = d_inner // SSM_GROUPS
    n = SSM_STATE
    t = CHUNK
    nc = seq_len // t
    hb = t // SUBLANES
    ng = SSM_GROUPS
    xb0 = d_inner // gw
    cwb0 = d_inner // n

    def row(b, g, c):
        return b * nc + c

    def halo(b, g, c):
        return jnp.maximum((b * nc + c) * hb - 1, 0)

    d_exp = jnp.repeat(d_skip.astype(F32), SSM_HEADDIM).reshape(1, d_inner)
    conv_b2 = conv_b.reshape(1, -1)
    head = jnp.arange(nh)
    sel = (head[None, :, None] == (jnp.arange(ng) * hpg)[:, None, None] + head[None, None, :]) \
        & (head[None, None, :] < hpg)
    chan_head = jnp.arange(gw) // SSM_HEADDIM
    expand = head[None, :, None] == (jnp.arange(ng) * hpg)[:, None, None] + chan_head[None, None, :]
    sel = sel.astype(F32)
    expand = jnp.concatenate([expand, expand], axis=1).astype(BF16)
    def lin(b, g, c):
        return (b * ng + g) * nc + c

    cast_arrays, cast_specs, spans, finish = _plan_casts(cast_weights, batch * ng * nc, lin)
    n_cast = len(cast_arrays)
    n_main = 20

    def kern(*refs):
        main_in, cast_in = refs[:n_main], refs[n_main:n_main + n_cast]
        o_ref = refs[n_main + n_cast]
        cast_out = refs[n_main + n_cast + 1:n_main + 2 * n_cast + 1]
        scratch = refs[n_main + 2 * n_cast + 1:]
        _run_casts(lin(pl.program_id(0), pl.program_id(1), pl.program_id(2)), spans, cast_in, cast_out)
        _ssd_kernel(*main_in, o_ref, *scratch, heads_per_group=hpg)

    outs = pl.pallas_call(
        kern,
        grid=(batch, ng, nc),
        in_specs=[
            pl.BlockSpec((t, gw), lambda b, g, c: (row(b, g, c), g)),
            pl.BlockSpec((t, gw), lambda b, g, c: (row(b, g, c), xb0 + g)),
            pl.BlockSpec((SUBLANES, gw), lambda b, g, c: (halo(b, g, c), xb0 + g)),
            pl.BlockSpec((t, n), lambda b, g, c: (row(b, g, c), g)),
            pl.BlockSpec((SUBLANES, n), lambda b, g, c: (halo(b, g, c), g)),
            pl.BlockSpec((t, n), lambda b, g, c: (row(b, g, c), ng + g)),
            pl.BlockSpec((SUBLANES, n), lambda b, g, c: (halo(b, g, c), ng + g)),
            pl.BlockSpec((t, nh), lambda b, g, c: (row(b, g, c), 0)),
            pl.BlockSpec((SSM_CONV, gw), lambda b, g, c: (0, g)),
            pl.BlockSpec((SSM_CONV, n), lambda b, g, c: (0, cwb0 + g)),
            pl.BlockSpec((SSM_CONV, n), lambda b, g, c: (0, cwb0 + ng + g)),
            pl.BlockSpec((1, gw), lambda b, g, c: (0, g)),
            pl.BlockSpec((1, n), lambda b, g, c: (0, cwb0 + g)),
            pl.BlockSpec((1, n), lambda b, g, c: (0, cwb0 + ng + g)),
            pl.BlockSpec((1, nh), lambda b, g, c: (0, 0)),
            pl.BlockSpec((1, nh), lambda b, g, c: (0, 0)),
            pl.BlockSpec((1, gw), lambda b, g, c: (0, g)),
            pl.BlockSpec((1, gw), lambda b, g, c: (0, g)),
            pl.BlockSpec((None, nh, nh), lambda b, g, c: (g, 0, 0)),
            pl.BlockSpec((None, 2 * nh, gw), lambda b, g, c: (g, 0, 0)),
        ] + cast_specs,
        out_specs=[pl.BlockSpec((t, gw), lambda b, g, c: (row(b, g, c), g))] + cast_specs,
        out_shape=[jax.ShapeDtypeStruct((m, d_inner), BF16)]
        + [jax.ShapeDtypeStruct(w.shape, BF16) for w in cast_arrays],
        scratch_shapes=[pltpu.VMEM((n, gw), F32)],
        compiler_params=_params("arbitrary", "arbitrary", "arbitrary"),
        name="ssd_mixer",
    )(zx, zx, zx, bcm, bcm, bcm, bcm, dt, conv_w, conv_w, conv_w, conv_b2, conv_b2, conv_b2,
      dt_bias.reshape(1, nh), a_log.reshape(1, nh), d_exp, norm_w.reshape(1, d_inner), sel, expand,
      *cast_arrays)
    return outs[0], finish(outs[1:])


def mamba_layer(xf, act, w_in, conv_w, conv_b, dt_bias, a_log, d_skip, norm_w, w_out, batch, seq_len,
                next_norm_w, cast_in_proj=(), cast_ssd=(), cast_out_proj=()):
    u, rs = act
    nh = dt_bias.shape[0]
    d_inner = norm_w.shape[0]
    bc_cols = 2 * SSM_GROUPS * SSM_STATE
    w_in = w_in.astype(BF16)
    zx, c_in = matmul(u, w_in, col0=0, n=2 * d_inner, tm=1024, tn=1024, row_ssq=rs, single_buffer_a=True,
                      cast_weights=list(cast_in_proj))
    bcm = matmul(u, w_in, col0=2 * d_inner, n=bc_cols, tm=1024, tn=1024, row_ssq=rs)
    dt = matmul(u, w_in, col0=2 * d_inner + bc_cols, n=nh, tm=1024, tn=nh, row_ssq=rs)
    yn, c_ssd = ssd_mixer(zx, bcm, dt, conv_w, conv_b, dt_bias, a_log, d_skip, norm_w, batch, seq_len,
                          cast_weights=(w_out,) + tuple(cast_ssd))
    xf, (xw, ssq), c_out = matmul(yn, c_ssd[0], tm=1024, tn=512, residual=xf, single_buffer_a=True,
                                  norm_w=next_norm_w, cast_weights=list(cast_out_proj))
    return xf, (xw, (ssq, xf.shape[1])), c_in, c_ssd[1:], c_out


def attention_layer(xf, act, cos, sin, w_qkv, lq1, lk1, lq2, lk2, subln_w, w_o, lambda_init, batch, seq_len,
                    next_norm_w, cast_weights=()):
    u, rs = act
    d_model = w_o.shape[0]
    tn = _pick(d_model, 1024)
    nb = w_qkv.shape[1] // tn
    qkv = pipelined_matmul(u, w_qkv.astype(BF16), [(tn, 0)],
                           epilogue=functools.partial(_epilogue_rope, rope_tiles=2 * d_model // tn),
                           row_aux=(cos, sin), out_cols=tn, out_dtype=BF16, nb=nb, row_ssq=rs,
                           single_buffer_a=False, name="qkv_rope")
    o, casted = diff_attention(qkv, lq1, lk1, lq2, lk2, subln_w, lambda_init, batch, seq_len,
                               cast_weights=cast_weights)
    xf, (xw, ssq) = matmul(o, w_o.astype(BF16), tm=1024, tn=1024, residual=xf, norm_w=next_norm_w,
                           single_buffer_a=True)
    return xf, (xw, (ssq, d_model)), casted


def ffn_layer(xf, act, w_up, conv_w, conv_b, w_down, seq_len, next_norm_w, cast_down_proj=()):
    u, rs = act
    d_ff = w_up.shape[1] // 2
    tn = _pick(d_ff, 256)
    nb = d_ff // tn
    cw = conv_w.astype(F32).reshape(FFN_CONV, 2, nb, tn).transpose(2, 0, 1, 3).reshape(nb, FFN_CONV, 2 * tn)
    cb = conv_b.astype(F32).reshape(2, nb, tn).transpose(1, 0, 2).reshape(nb, 1, 2 * tn)
    g = pipelined_matmul(u, w_up.astype(BF16), [(tn, 0), (tn, nb)], epilogue=_epilogue_conv_gate,
                         col_aux=(cw, cb), out_cols=tn, out_dtype=BF16, nb=nb, halo_seq_len=seq_len,
                         tm=2048, lagged=False, single_buffer_a=True, row_ssq=rs, name="ffn_up")
    outs = matmul(g, w_down.astype(BF16), tm=1024, tn=256, residual=xf, single_buffer_a=True,
                  norm_w=next_norm_w, cast_weights=list(cast_down_proj))
    if next_norm_w is None:
        return outs[0], None, outs[1]
    xf, (xw, ssq), c_down = outs
    return xf, (xw, (ssq, xf.shape[1])), c_down


def kernel(x, positions, l0_norm_mix, l0_m_w_in, l0_m_conv_w, l0_m_conv_b, l0_m_dt_bias, l0_m_a_log, l0_m_d, l0_m_norm, l0_m_w_out, l0_norm_ffn, l0_f_w_up, l0_f_conv_w, l0_f_conv_b, l0_f_w_down, l1_norm_mix, l1_a_w_qkv, l1_a_lq1, l1_a_lk1, l1_a_lq2, l1_a_lk2, l1_a_subln, l1_a_w_o, l1_norm_ffn, l1_f_w_up, l1_f_conv_w, l1_f_conv_b, l1_f_w_down, l2_norm_mix, l2_m_w_in, l2_m_conv_w, l2_m_conv_b, l2_m_dt_bias, l2_m_a_log, l2_m_d, l2_m_norm, l2_m_w_out, l2_norm_ffn, l2_f_w_up, l2_f_conv_w, l2_f_conv_b, l2_f_w_down, l3_norm_mix, l3_a_w_qkv, l3_a_lq1, l3_a_lk1, l3_a_lq2, l3_a_lk2, l3_a_subln, l3_a_w_o, l3_norm_ffn, l3_f_w_up, l3_f_conv_w, l3_f_conv_b, l3_f_w_down, final_norm):
    norm_mix = [l0_norm_mix, l1_norm_mix, l2_norm_mix, l3_norm_mix]
    norm_ffn = [l0_norm_ffn, l1_norm_ffn, l2_norm_ffn, l3_norm_ffn]
    mixer_params = [
        (l0_m_w_in, l0_m_conv_w, l0_m_conv_b, l0_m_dt_bias, l0_m_a_log, l0_m_d, l0_m_norm, l0_m_w_out),
        (l1_a_w_qkv, l1_a_lq1, l1_a_lk1, l1_a_lq2, l1_a_lk2, l1_a_subln, l1_a_w_o),
        (l2_m_w_in, l2_m_conv_w, l2_m_conv_b, l2_m_dt_bias, l2_m_a_log, l2_m_d, l2_m_norm, l2_m_w_out),
        (l3_a_w_qkv, l3_a_lq1, l3_a_lk1, l3_a_lq2, l3_a_lk2, l3_a_subln, l3_a_w_o),
    ]
    ffn_params = [
        (l0_f_w_up, l0_f_conv_w, l0_f_conv_b, l0_f_w_down),
        (l1_f_w_up, l1_f_conv_w, l1_f_conv_b, l1_f_w_down),
        (l2_f_w_up, l2_f_conv_w, l2_f_conv_b, l2_f_w_down),
        (l3_f_w_up, l3_f_conv_w, l3_f_conv_b, l3_f_w_down),
    ]
    batch, seq_len, d_model = x.shape
    xf = x.reshape(batch * seq_len, d_model)
    cos, sin = rope_tables(positions)
    n_layers = len(norm_mix)
    act = (rmsnorm(xf, norm_mix[0], BF16), None)
    for i in range(n_layers):
        w_up, cw, cb, w_down = ffn_params[i]
        nxt = list(mixer_params[i + 1]) if i + 1 < n_layers else None
        if i % N_MIXERS == 0:
            xf, act, (w_up,), (w_down,), c_out = mamba_layer(
                xf, act, *mixer_params[i], batch, seq_len, norm_ffn[i], cast_in_proj=[w_up],
                cast_ssd=[w_down], cast_out_proj=[nxt[0]] if nxt else [])
            if nxt:
                nxt[0] = c_out[0]
            cast_down = [nxt[-1], ffn_params[i + 1][0]] if nxt else []
        else:
            lambda_init = 0.8 - 0.6 * math.exp(-0.3 * i)
            xf, act, (w_up, w_down) = attention_layer(xf, act, cos, sin, *mixer_params[i], lambda_init, batch,
                                                      seq_len, norm_ffn[i], cast_weights=[w_up, w_down])
            cast_down = [nxt[0]] if nxt else []
        xf, act, c_down = ffn_layer(xf, act, w_up, cw, cb, w_down, seq_len,
                                    norm_mix[i + 1] if nxt else None, cast_down_proj=cast_down)
        if nxt:
            nxt[-1 if i % N_MIXERS == 0 else 0] = c_down[0]
            mixer_params[i + 1] = tuple(nxt)
            if len(c_down) > 1:
                ffn_params[i + 1] = (c_down[1],) + tuple(ffn_params[i + 1][1:])
    return rmsnorm(xf, final_norm, F32).reshape(batch, seq_len, d_model)
```

```python
import functools
import math

import jax
import jax.numpy as jnp
from jax import lax
from jax.experimental import pallas as pl
from jax.experimental.pallas import tpu as pltpu

F32 = jnp.float32
BF16 = jnp.bfloat16

EPS = 1e-5
N_MIXERS = 2

SSM_HEADDIM = 64
SSM_GROUPS = 8
SSM_STATE = 128
SSM_CONV = 4
CHUNK = 128

DIFF_HEAD_DIM = 128
ATTN_SCALE = DIFF_HEAD_DIM ** -0.5
Q_PRESCALE = ATTN_SCALE * math.log2(math.e)
ROPE_THETA = 10000.0

FFN_CONV = 3

V7X_VMEM_BYTES = 64 * 1024 * 1024
VMEM_LIMIT_BYTES = V7X_VMEM_BYTES - 8 * 1024 * 1024
SUBLANES = 8
BF16_ROWS = 16
LANES = 128
MXU_COLS = 256
CAST_BLOCK_BYTES = 4 * 1024 * 1024
EPILOGUE_VREGS = 32


def _params(*sem):
    return pltpu.CompilerParams(dimension_semantics=sem, vmem_limit_bytes=VMEM_LIMIT_BYTES)


def _pick(dim, pref):
    if dim <= pref:
        return dim
    t = pref
    while dim % t:
        t //= 2
    return t


def _silu(x):
    return x / (1.0 + jnp.exp(-x))


def _rmsnorm_kernel(x_ref, w_ref, o_ref):
    x = x_ref[...]
    y = x * lax.rsqrt(jnp.mean(x * x, axis=-1, keepdims=True) + EPS)
    o_ref[...] = (y * w_ref[...]).astype(o_ref.dtype)


def rmsnorm(x, w, out_dtype):
    m, d = x.shape
    tm = _pick(m, 256)
    return pl.pallas_call(
        _rmsnorm_kernel,
        grid=(m // tm,),
        in_specs=[pl.BlockSpec((tm, d), lambda i: (i, 0)),
                  pl.BlockSpec((1, d), lambda i: (0, 0))],
        out_specs=pl.BlockSpec((tm, d), lambda i: (i, 0)),
        out_shape=jax.ShapeDtypeStruct((m, d), out_dtype),
        compiler_params=_params("arbitrary"),
        name="rmsnorm",
    )(x, w.reshape(1, d))


def _row_rstd(ssq, d):
    return lax.rsqrt(jnp.sum(ssq, axis=-1, keepdims=True) / d + EPS)


def _mm_kernel(*refs, has_residual, has_norm, scale_d, n_cast):
    it = iter(refs)
    a_ref, w_ref = next(it), next(it)
    r_ref = next(it) if has_residual else None
    nw_ref = next(it) if has_norm else None
    ssq_in_ref = next(it) if scale_d else None
    cast_in = [next(it) for _ in range(n_cast)]
    o_ref = next(it)
    xw_ref, ssq_ref = (next(it), next(it)) if has_norm else (None, None)
    cast_out = [next(it) for _ in range(n_cast)]
    for src, dst in zip(cast_in, cast_out):
        dst[...] = src[...].astype(dst.dtype)
    acc = jnp.dot(a_ref[...], w_ref[...], preferred_element_type=F32)
    if scale_d:
        acc = acc * _row_rstd(ssq_in_ref[...], scale_d)
    if has_residual:
        acc = r_ref[...] + acc
    o_ref[...] = acc.astype(o_ref.dtype)
    if has_norm:
        xw_ref[...] = (acc * nw_ref[...]).astype(xw_ref.dtype)
        sq = acc * acc
        part = functools.reduce(jnp.add, [sq[:, t * LANES:(t + 1) * LANES] for t in range(sq.shape[1] // LANES)])

        @pl.when(pl.program_id(1) == 0)
        def _():
            ssq_ref[...] = jnp.zeros(ssq_ref.shape, F32)

        ssq_ref[...] += part


def matmul(a, w, *, col0=0, n=None, tm, tn, out_dtype=F32, residual=None, single_buffer_a=False,
           cast_weights=None, row_ssq=None, norm_w=None):
    want_casts = cast_weights is not None
    cast_weights = cast_weights or ()
    m, k = a.shape
    n = w.shape[1] - col0 if n is None else n
    tm = _pick(m, tm)
    tn = _pick(n, tn)
    assert col0 % tn == 0
    jb = col0 // tn
    nj = n // tn
    steps = (m // tm) * nj
    a_kwargs = dict(pipeline_mode=pl.Buffered(1)) if single_buffer_a else {}
    in_specs = [pl.BlockSpec((tm, k), lambda i, j: (i, 0), **a_kwargs),
                pl.BlockSpec((k, tn), lambda i, j: (0, j + jb))]
    args = [a, w]
    tile = pl.BlockSpec((tm, tn), lambda i, j: (i, j))
    rows = pl.BlockSpec((tm, LANES), lambda i, j: (i, 0))
    if residual is not None:
        in_specs.append(tile)
        args.append(residual)
    if norm_w is not None:
        in_specs.append(pl.BlockSpec((1, tn), lambda i, j: (0, j)))
        args.append(norm_w.reshape(1, n).astype(F32))
    if row_ssq is not None:
        in_specs.append(rows)
        args.append(row_ssq[0])
    in_kernel = [c.shape[0] % (steps * BF16_ROWS) == 0
                 and (c.shape[0] // steps) * c.shape[1] * 4 <= CAST_BLOCK_BYTES for c in cast_weights]
    cast_arrays = [c for c, ok in zip(cast_weights, in_kernel) if ok]
    cast_specs = [pl.BlockSpec((c.shape[0] // steps, c.shape[1]), lambda i, j: (i * nj + j, 0))
                  for c in cast_arrays]
    norm_specs = [tile, rows] if norm_w is not None else []
    norm_shapes = ([jax.ShapeDtypeStruct((m, n), BF16), jax.ShapeDtypeStruct((m, LANES), F32)]
                   if norm_w is not None else [])
    outs = pl.pallas_call(
        functools.partial(_mm_kernel, has_residual=residual is not None, has_norm=norm_w is not None,
                          scale_d=row_ssq[1] if row_ssq is not None else 0, n_cast=len(cast_arrays)),
        grid=(m // tm, nj),
        in_specs=in_specs + cast_specs,
        out_specs=[tile] + norm_specs + cast_specs,
        out_shape=[jax.ShapeDtypeStruct((m, n), out_dtype)] + norm_shapes
        + [jax.ShapeDtypeStruct(c.shape, BF16) for c in cast_arrays],
        compiler_params=_params("arbitrary", "arbitrary"),
        name="matmul",
    )(*args, *cast_arrays)
    result = [outs[0]]
    n_norm = len(norm_specs)
    if norm_w is not None:
        result.append((outs[1], outs[2]))
    if want_casts:
        done = iter(outs[1 + n_norm:])
        result.append([next(done) if ok else c.astype(BF16) for c, ok in zip(cast_weights, in_kernel)])
    return result[0] if len(result) == 1 else tuple(result)


def _shift_rows(ext, s, halo_rows):
    if s == 0:
        return ext[halo_rows:]
    return pltpu.roll(ext, s, axis=0)[halo_rows:]


def _causal_conv_ext(ext, halo_rows, w, b):
    k_width = w.shape[0]
    y = b + w[0:1] * _shift_rows(ext, k_width - 1, halo_rows)
    for k in range(1, k_width):
        y = y + w[k:k + 1] * _shift_rows(ext, k_width - 1 - k, halo_rows)
    return y


def _row_chunks(rows, cols):
    rc = max(BF16_ROWS, EPILOGUE_VREGS * SUBLANES * LANES // cols)
    rc = min(rc, rows)
    return [(r0, rc) for r0 in range(0, rows, rc)]


def _epilogue_conv_gate(h_ref, halo_rows, aux, o_ref, jp):
    cw, cb = aux[0][...], aux[1][...]
    tn = o_ref.shape[1]

    def chunk(r0, rc):
        ext = h_ref[pl.ds(halo_rows + r0 - SUBLANES, rc + SUBLANES), :]
        y = _causal_conv_ext(ext, SUBLANES, cw, cb)
        o_ref[r0:r0 + rc, :] = (_silu(y[:, :tn]) * y[:, tn:]).astype(o_ref.dtype)

    return [functools.partial(chunk, r0, rc) for r0, rc in _row_chunks(o_ref.shape[0], 2 * tn)]


def _epilogue_rope(h_ref, halo_rows, aux, o_ref, jp, *, rope_tiles):
    cos_ref, sin_ref = aux
    d = DIFF_HEAD_DIM
    is_rope = jp < rope_tiles
    scale = jnp.where(jp < rope_tiles // 2, Q_PRESCALE, 1.0)

    def chunk(r0, rc):
        cos = jnp.where(is_rope, cos_ref[r0:r0 + rc, :] * scale, 1.0)
        sin = jnp.where(is_rope, sin_ref[r0:r0 + rc, :] * scale, 0.0)
        for c in range(o_ref.shape[1] // d):
            t = h_ref[r0:r0 + rc, c * d:(c + 1) * d]
            rot = pltpu.roll(t, d // 2, axis=1)
            o_ref[r0:r0 + rc, c * d:(c + 1) * d] = (t * cos + rot * sin).astype(o_ref.dtype)

    return [functools.partial(chunk, r0, rc) for r0, rc in _row_chunks(o_ref.shape[0], 4 * d)]


def _pipelined_kernel(*refs, epilogue, n_w, n_aux, use_halo, lagged, scale_d, nb, n_tiles, tiles_per_seq):
    it = iter(refs)
    a_ref = next(it)
    ah_ref = next(it) if use_halo else None
    w_refs = [next(it) for _ in range(n_w)]
    ssq_ref = next(it) if scale_d else None
    ssqh_ref = next(it) if (scale_d and use_halo) else None
    aux = [next(it) for _ in range(n_aux)]
    o_ref = next(it)
    aext_ref = next(it) if use_halo else None
    h = [next(it) for _ in range(2 if lagged else 1)]
    halo_rows = BF16_ROWS if use_halo else 0

    s = pl.program_id(0)
    t = jnp.minimum(s, n_tiles - 1)
    i = t // nb
    j = t % nb

    if use_halo:
        @pl.when((j == 0) & (s < n_tiles))
        def _():
            seq_start = (i % tiles_per_seq) == 0
            halo = ah_ref[...]
            aext_ref[0:BF16_ROWS, :] = jnp.where(seq_start, jnp.zeros_like(halo), halo)
            aext_ref[BF16_ROWS:, :] = a_ref[...]

    def matmul_into(h_ref):
        lhs_ref = aext_ref if use_halo else a_ref
        rstd = None
        if scale_d:
            rstd = _row_rstd(ssq_ref[...], scale_d)
            if use_halo:
                rstd = jnp.concatenate([_row_rstd(ssqh_ref[...], scale_d), rstd], axis=0)
        col = 0
        for w_ref in w_refs:
            wn = w_ref.shape[1]
            step_cols = min(wn, MXU_COLS)
            for c0 in range(0, wn, step_cols):
                prod = jnp.dot(lhs_ref[...], w_ref[:, c0:c0 + step_cols], preferred_element_type=F32)
                h_ref[:, col + c0:col + c0 + step_cols] = prod if rstd is None else prod * rstd
            col += wn

    if not lagged:
        matmul_into(h[0])
        for chunk in epilogue(h[0], halo_rows, aux, o_ref, j):
            chunk()
        return

    jp = jnp.maximum(s - 1, 0) % nb
    slot = s % 2

    @pl.when(s == 0)
    def _():
        h[1][...] = jnp.zeros(h[1].shape, F32)

    def step(cur, prev):
        for chunk in epilogue(h[prev], halo_rows, aux, o_ref, jp):
            chunk()
        matmul_into(h[cur])

    @pl.when(slot == 0)
    def _():
        step(0, 1)

    @pl.when(slot == 1)
    def _():
        step(1, 0)


def pipelined_matmul(a, w, w_col_blocks, *, epilogue, col_aux=(), row_aux=(), out_cols, out_dtype,
                     nb, halo_seq_len=None, tm=1024, single_buffer_a=True, lagged=True, row_ssq=None,
                     name):
    m, k = a.shape
    use_halo = halo_seq_len is not None
    tm = _pick(halo_seq_len if use_halo else m, tm)
    hb = tm // BF16_ROWS
    n_tiles = (m // tm) * nb
    hrows = tm + (BF16_ROWS if use_halo else 0)
    wn = sum(width for width, _ in w_col_blocks)

    def cur(s):
        t = jnp.minimum(s, n_tiles - 1)
        return t // nb, t % nb

    def prev(s):
        t = jnp.maximum(s - 1, 0) if lagged else s
        return t // nb, t % nb

    a_kwargs = dict(pipeline_mode=pl.Buffered(1)) if single_buffer_a else {}
    in_specs = [pl.BlockSpec((tm, k), lambda s: (cur(s)[0], 0), **a_kwargs)]
    args = [a]
    if use_halo:
        in_specs.append(pl.BlockSpec((BF16_ROWS, k), lambda s: (jnp.maximum(cur(s)[0] * hb - 1, 0), 0)))
        args.append(a)
    for width, first in w_col_blocks:
        in_specs.append(pl.BlockSpec((k, width), lambda s, first=first: (0, first + cur(s)[1])))
        args.append(w)
    if row_ssq is not None:
        in_specs.append(pl.BlockSpec((tm, LANES), lambda s: (cur(s)[0], 0)))
        args.append(row_ssq[0])
        if use_halo:
            in_specs.append(pl.BlockSpec((BF16_ROWS, LANES), lambda s: (jnp.maximum(cur(s)[0] * hb - 1, 0), 0)))
            args.append(row_ssq[0])
    for arr in col_aux:
        in_specs.append(pl.BlockSpec((None,) + arr.shape[1:], lambda s: (prev(s)[1], 0, 0)))
        args.append(arr)
    for arr in row_aux:
        in_specs.append(pl.BlockSpec((tm, arr.shape[1]), lambda s: (prev(s)[0], 0)))
        args.append(arr)
    scratch = ([pltpu.VMEM((hrows, k), BF16)] if use_halo else [])
    scratch += [pltpu.VMEM((hrows, wn), F32)] * (2 if lagged else 1)
    kern = functools.partial(
        _pipelined_kernel, epilogue=epilogue, n_w=len(w_col_blocks), n_aux=len(col_aux) + len(row_aux),
        use_halo=use_halo, lagged=lagged, scale_d=row_ssq[1] if row_ssq is not None else 0, nb=nb,
        n_tiles=n_tiles,
        tiles_per_seq=(halo_seq_len // tm) if use_halo else 1)
    return pl.pallas_call(
        kern,
        grid=(n_tiles + (1 if lagged else 0),),
        in_specs=in_specs,
        out_specs=pl.BlockSpec((tm, out_cols), lambda s: prev(s)),
        out_shape=jax.ShapeDtypeStruct((m, nb * out_cols), out_dtype),
        scratch_shapes=scratch,
        compiler_params=_params("arbitrary"),
        name=name,
    )(*args)


def _rope_table_kernel(pos_ref, inv_ref, cos_ref, sin_ref):
    freqs = pos_ref[...].astype(F32) * inv_ref[...]
    lane = lax.broadcasted_iota(jnp.int32, freqs.shape, 1)
    sign = jnp.where(lane < DIFF_HEAD_DIM // 2, -1.0, 1.0)
    cos_ref[...] = jnp.cos(freqs)
    sin_ref[...] = jnp.sin(freqs) * sign


def rope_tables(positions):
    m = positions.size
    inv = 1.0 / (ROPE_THETA ** (jnp.arange(0, DIFF_HEAD_DIM, 2, dtype=F32) / DIFF_HEAD_DIM))
    inv_full = jnp.concatenate([inv, inv]).reshape(1, DIFF_HEAD_DIM)
    tm = _pick(m, 1024)
    shape = jax.ShapeDtypeStruct((m, DIFF_HEAD_DIM), F32)
    return pl.pallas_call(
        _rope_table_kernel,
        grid=(m // tm,),
        in_specs=[pl.BlockSpec((tm, 1), lambda i: (i, 0)),
                  pl.BlockSpec((1, DIFF_HEAD_DIM), lambda i: (0, 0))],
        out_specs=[pl.BlockSpec((tm, DIFF_HEAD_DIM), lambda i: (i, 0))] * 2,
        out_shape=[shape, shape],
        compiler_params=_params("arbitrary"),
        name="rope_tables",
    )(positions.reshape(m, 1), inv_full)


def _diff_attn_kernel(q_ref, k_ref, v_ref, lq1_ref, lk1_ref, lq2_ref, lk2_ref, sw_ref, o_ref,
                      m_ref, l_ref, acc_ref, *, lambda_init, tq):
    qi = pl.program_id(2)
    d = DIFF_HEAD_DIM
    q = q_ref[...]

    m_ref[...] = jnp.full(m_ref.shape, -jnp.inf, F32)
    l_ref[...] = jnp.zeros(l_ref.shape, F32)
    acc_ref[...] = jnp.zeros(acc_ref.shape, F32)

    def kv_block(kb, masked):
        start = pl.multiple_of(kb * tq, tq)
        k = k_ref[pl.ds(start, tq), :]
        v = v_ref[pl.ds(start, tq), :]
        scores = [lax.dot_general(q[:, c * d:(c + 1) * d], k[:, c * d:(c + 1) * d],
                                  (((1,), (1,)), ((), ())), preferred_element_type=F32) for c in range(2)]
        probs = []
        alphas = []
        for c in range(2):
            s = scores[c]
            if masked:
                row = lax.broadcasted_iota(jnp.int32, s.shape, 0)
                col = lax.broadcasted_iota(jnp.int32, s.shape, 1)
                s = jnp.where(col <= row, s, -jnp.inf)
            chunks = [s[:, t * LANES:(t + 1) * LANES] for t in range(tq // LANES)]
            cmax = functools.reduce(jnp.maximum, chunks)
            m_old = m_ref[c]
            m_new = jnp.maximum(m_old, jnp.max(cmax, axis=-1, keepdims=True))
            alpha = jnp.exp2(m_old - m_new)
            ps = [jnp.exp2(ch - m_new) for ch in chunks]
            l_ref[c] = alpha * l_ref[c] + functools.reduce(jnp.add, ps)
            m_ref[c] = m_new
            probs.append(jnp.concatenate(ps, axis=1).astype(v.dtype))
            alphas.append(alpha)
        pv = jnp.dot(jnp.concatenate(probs, axis=0), v, preferred_element_type=F32)
        for c in range(2):
            a2 = jnp.concatenate([alphas[c], alphas[c]], axis=1)
            acc_ref[c] = a2 * acc_ref[c] + pv[c * tq:(c + 1) * tq]

    def body(kb, carry):
        kv_block(kb, False)
        return carry

    lax.fori_loop(0, qi, body, 0)
    kv_block(qi, True)

    lam = (jnp.exp(jnp.sum(lq1_ref[...] * lk1_ref[...], axis=-1, keepdims=True))
           - jnp.exp(jnp.sum(lq2_ref[...] * lk2_ref[...], axis=-1, keepdims=True)) + lambda_init)
    l0 = jnp.sum(l_ref[0], axis=-1, keepdims=True)
    l1 = jnp.sum(l_ref[1], axis=-1, keepdims=True)
    o = acc_ref[0] / l0 - lam * (acc_ref[1] / l1)
    o = o * lax.rsqrt(jnp.mean(o * o, axis=-1, keepdims=True) + EPS)
    o_ref[...] = ((o * sw_ref[...]) * (1.0 - lambda_init)).astype(o_ref.dtype)


def _cast_rows(w, steps):
    k, n = w.shape
    for rows in range(BF16_ROWS, k + 1, BF16_ROWS):
        if k % rows == 0 and k // rows <= steps:
            return rows if rows * n * 4 <= CAST_BLOCK_BYTES else None
    return None


def _plan_casts(weights, steps, lin):
    rows = [None if w.dtype == BF16 else _cast_rows(w, steps) for w in weights]
    arrays, specs, spans = [], [], []
    at_end = False
    for w, r in zip(weights, rows):
        if not r:
            continue
        nb = w.shape[0] // r
        first = steps - nb if at_end else 0
        at_end = not at_end if nb < steps else at_end
        arrays.append(w)
        spans.append((first, nb))
        specs.append(pl.BlockSpec(
            (r, w.shape[1]),
            lambda *ids, first=first, nb=nb: (jnp.clip(lin(*ids) - first, 0, nb - 1), 0)))

    def finish(outputs):
        done = iter(outputs)
        return [next(done) if r else w.astype(BF16) for w, r in zip(weights, rows)]

    return arrays, specs, spans, finish


def _run_casts(step, spans, cast_in, cast_out):
    for (first, nb), src, dst in zip(spans, cast_in, cast_out):
        @pl.when((step >= first) & (step < first + nb))
        def _():
            dst[...] = src[...].astype(dst.dtype)


def diff_attention(qkv, lq1, lk1, lq2, lk2, subln_w, lambda_init, batch, seq_len, *, tq=512,
                   cast_weights=()):
    m, n = qkv.shape
    d_model = n // 3
    hw = 2 * DIFF_HEAD_DIM
    heads = d_model // hw
    tq = _pick(seq_len, tq)
    nq = seq_len // tq

    def lin(b, h, qi):
        return (b * heads + h) * nq + qi

    cast_arrays, cast_specs, spans, finish = _plan_casts(cast_weights, batch * heads * nq, lin)
    n_cast = len(cast_arrays)

    def kern(*refs):
        main_in, cast_in = refs[:8], refs[8:8 + n_cast]
        o_ref = refs[8 + n_cast]
        cast_out = refs[9 + n_cast:9 + 2 * n_cast]
        scratch = refs[9 + 2 * n_cast:]
        _run_casts(lin(pl.program_id(0), pl.program_id(1), pl.program_id(2)), spans, cast_in, cast_out)
        _diff_attn_kernel(*main_in, o_ref, *scratch, lambda_init=lambda_init, tq=tq)

    vec = lambda a: a.reshape(1, -1)
    lam_spec = pl.BlockSpec((1, DIFF_HEAD_DIM), lambda b, h, qi: (0, 0))
    outs = pl.pallas_call(
        kern,
        grid=(batch, heads, nq),
        in_specs=[
            pl.BlockSpec((tq, hw), lambda b, h, qi: (b * nq + qi, h)),
            pl.BlockSpec((seq_len, hw), lambda b, h, qi: (b, heads + h)),
            pl.BlockSpec((seq_len, hw), lambda b, h, qi: (b, 2 * heads + h)),
            lam_spec, lam_spec, lam_spec, lam_spec,
            pl.BlockSpec((1, hw), lambda b, h, qi: (0, 0)),
        ] + cast_specs,
        out_specs=[pl.BlockSpec((tq, hw), lambda b, h, qi: (b * nq + qi, h))] + cast_specs,
        out_shape=[jax.ShapeDtypeStruct((m, d_model), BF16)]
        + [jax.ShapeDtypeStruct(w.shape, BF16) for w in cast_arrays],
        scratch_shapes=[pltpu.VMEM((2, tq, LANES), F32), pltpu.VMEM((2, tq, LANES), F32),
                        pltpu.VMEM((2, tq, hw), F32)],
        compiler_params=_params("arbitrary", "arbitrary", "arbitrary"),
        name="diff_attention",
    )(qkv, qkv, qkv, vec(lq1), vec(lk1), vec(lq2), vec(lk2), vec(subln_w), *cast_arrays)
    return outs[0], finish(outs[1:])


def _dot_exact(a, b):
    return jnp.dot(a, b, preferred_element_type=F32, precision=lax.Precision.HIGHEST)


def _ssd_kernel(z_ref, x_ref, xh_ref, b_ref, bh_ref, c_ref, ch_ref, dt_ref,
                cwx_ref, cwb_ref, cwc_ref, cbx_ref, cbb_ref, cbc_ref,
                dtb_ref, alog_ref, dexp_ref, nw_ref, sel_ref, expand_ref, o_ref, state_ref, *, heads_per_group):
    g = pl.program_id(1)
    c = pl.program_id(2)
    t = CHUNK
    hpg = heads_per_group
    p = SSM_HEADDIM
    gw = hpg * p
    nh = dt_ref.shape[1]

    @pl.when(c == 0)
    def _():
        state_ref[...] = jnp.zeros(state_ref.shape, F32)

    seq_start = c == 0

    def conv_silu(cur_ref, halo_ref, w_ref, bias_ref):
        halo = halo_ref[...]
        halo = jnp.where(seq_start, jnp.zeros_like(halo), halo)
        ext = jnp.concatenate([halo, cur_ref[...]], axis=0)
        return _silu(_causal_conv_ext(ext, SUBLANES, w_ref[...], bias_ref[...]))

    xc = conv_silu(x_ref, xh_ref, cwx_ref, cbx_ref)
    bc = conv_silu(b_ref, bh_ref, cwb_ref, cbb_ref)
    cc = conv_silu(c_ref, ch_ref, cwc_ref, cbc_ref)

    dtr = dt_ref[...] + dtb_ref[...]
    dtv = jnp.maximum(dtr, 0.0) + jnp.log1p(jnp.exp(-jnp.abs(dtr)))
    a_all = dtv * (-jnp.exp(alog_ref[...]))

    row_t = lax.broadcasted_iota(jnp.int32, (t, t), 0)
    col_t = lax.broadcasted_iota(jnp.int32, (t, t), 1)
    causal = col_t <= row_t
    a_cs_all = _dot_exact(jnp.where(causal, 1.0, 0.0), a_all)
    a_last = a_cs_all[t - 1:t, :]

    a_cs = _dot_exact(a_cs_all, sel_ref[...])
    a_cs_t = a_cs.T

    stacked = jnp.concatenate([dtv, jnp.exp(a_cs_all), jnp.exp(a_last - a_cs_all)], axis=0)
    hi = stacked.astype(BF16)
    lo = (stacked - hi.astype(F32)).astype(BF16)
    expanded = jnp.dot(jnp.concatenate([hi, lo], axis=1), expand_ref[...],
                       preferred_element_type=F32)
    dt_e = expanded[0:t]
    exp_acs_e = expanded[t:2 * t]
    dec_e = expanded[2 * t:3 * t]

    xdt = xc * dt_e
    xdt_b = xdt.astype(BF16)
    bb = bc.astype(BF16)
    cb = cc.astype(BF16)
    cbm = lax.dot_general(cb, bb, (((1,), (1,)), ((), ())), preferred_element_type=F32)

    state = state_ref[...]
    y = jnp.dot(cb, state.astype(BF16), preferred_element_type=F32) * exp_acs_e

    lane = lax.broadcasted_iota(jnp.int32, (t, 2 * p), 1)
    pieces = []
    for pair in range(hpg // 2):
        x_pair = xdt_b[:, pair * 2 * p:(pair + 1) * 2 * p]
        outs = []
        for hh in (2 * pair, 2 * pair + 1):
            seg = a_cs[:, hh:hh + 1] - a_cs_t[hh:hh + 1, :]
            decay = jnp.exp(jnp.where(causal, seg, -jnp.inf))
            outs.append(jnp.dot((cbm * decay).astype(BF16), x_pair, preferred_element_type=F32))
        pieces.append(jnp.where(lane < p, outs[0], outs[1]))
    y = y + jnp.concatenate(pieces, axis=1) + dexp_ref[...] * xc

    new_contrib = jnp.dot(bc.T.astype(BF16), (xdt * dec_e).astype(BF16), preferred_element_type=F32)
    state_ref[...] = state * exp_acs_e[t - 1:t, :] + new_contrib

    gn = y * _silu(z_ref[...])
    gn = gn * lax.rsqrt(jnp.mean(gn * gn, axis=-1, keepdims=True) + EPS)
    o_ref[...] = (gn * nw_ref[...]).astype(o_ref.dtype)


def ssd_mixer(zx, bcm, dt, conv_w, conv_b, dt_bias, a_log, d_skip, norm_w, batch, seq_len, cast_weights=()):
    m = zx.shape[0]
    d_inner = zx.shape[1] // 2
    nh = dt.shape[1]
    hpg = nh // SSM_GROUPS
    gw = d_inner // SSM_GROUPS
    n = SSM_STATE
    t = CHUNK
    nc = seq_len // t
    hb = t // SUBLANES
    ng = SSM_GROUPS
    xb0 = d_inner // gw
    cwb0 = d_inner // n

    def row(b, g, c):
        return b * nc + c

    def halo(b, g, c):
        return jnp.maximum((b * nc + c) * hb - 1, 0)

    d_exp = jnp.repeat(d_skip.astype(F32), SSM_HEADDIM).reshape(1, d_inner)
    conv_b2 = conv_b.reshape(1, -1)
    head = jnp.arange(nh)
    sel = (head[None, :, None] == (jnp.arange(ng) * hpg)[:, None, None] + head[None, None, :]) \
        & (head[None, None, :] < hpg)
    chan_head = jnp.arange(gw) // SSM_HEADDIM
    expand = head[None, :, None] == (jnp.arange(ng) * hpg)[:, None, None] + chan_head[None, None, :]
    sel = sel.astype(F32)
    expand = jnp.concatenate([expand, expand], axis=1).astype(BF16)
    def lin(b, g, c):
        return (b * ng + g) * nc + c

    cast_arrays, cast_specs, spans, finish = _plan_casts(cast_weights, batch * ng * nc, lin)
    n_cast = len(cast_arrays)
    n_main = 20

    def kern(*refs):
        main_in, cast_in = refs[:n_main], refs[n_main:n_main + n_cast]
        o_ref = refs[n_main + n_cast]
        cast_out = refs[n_main + n_cast + 1:n_main + 2 * n_cast + 1]
        scratch = refs[n_main + 2 * n_cast + 1:]
        _run_casts(lin(pl.program_id(0), pl.program_id(1), pl.program_id(2)), spans, cast_in, cast_out)
        _ssd_kernel(*main_in, o_ref, *scratch, heads_per_group=hpg)

    outs = pl.pallas_call(
        kern,
        grid=(batch, ng, nc),
        in_specs=[
            pl.BlockSpec((t, gw), lambda b, g, c: (row(b, g, c), g)),
            pl.BlockSpec((t, gw), lambda b, g, c: (row(b, g, c), xb0 + g)),
            pl.BlockSpec((SUBLANES, gw), lambda b, g, c: (halo(b, g, c), xb0 + g)),
            pl.BlockSpec((t, n), lambda b, g, c: (row(b, g, c), g)),
            pl.BlockSpec((SUBLANES, n), lambda b, g, c: (halo(b, g, c), g)),
            pl.BlockSpec((t, n), lambda b, g, c: (row(b, g, c), ng + g)),
            pl.BlockSpec((SUBLANES, n), lambda b, g, c: (halo(b, g, c), ng + g)),
            pl.BlockSpec((t, nh), lambda b, g, c: (row(b, g, c), 0)),
            pl.BlockSpec((SSM_CONV, gw), lambda b, g, c: (0, g)),
            pl.BlockSpec((SSM_CONV, n), lambda b, g, c: (0, cwb0 + g)),
            pl.BlockSpec((SSM_CONV, n), lambda b, g, c: (0, cwb0 + ng + g)),
            pl.BlockSpec((1, gw), lambda b, g, c: (0, g)),
            pl.BlockSpec((1, n), lambda b, g, c: (0, cwb0 + g)),
            pl.BlockSpec((1, n), lambda b, g, c: (0, cwb0 + ng + g)),
            pl.BlockSpec((1, nh), lambda b, g, c: (0, 0)),
            pl.BlockSpec((1, nh), lambda b, g, c: (0, 0)),
            pl.BlockSpec((1, gw), lambda b, g, c: (0, g)),
            pl.BlockSpec((1, gw), lambda b, g, c: (0, g)),
            pl.BlockSpec((None, nh, nh), lambda b, g, c: (g, 0, 0)),
            pl.BlockSpec((None, 2 * nh, gw), lambda b, g, c: (g, 0, 0)),
        ] + cast_specs,
        out_specs=[pl.BlockSpec((t, gw), lambda b, g, c: (row(b, g, c), g))] + cast_specs,
        out_shape=[jax.ShapeDtypeStruct((m, d_inner), BF16)]
        + [jax.ShapeDtypeStruct(w.shape, BF16) for w in cast_arrays],
        scratch_shapes=[pltpu.VMEM((n, gw), F32)],
        compiler_params=_params("arbitrary", "arbitrary", "arbitrary"),
        name="ssd_mixer",
    )(zx, zx, zx, bcm, bcm, bcm, bcm, dt, conv_w, conv_w, conv_w, conv_b2, conv_b2, conv_b2,
      dt_bias.reshape(1, nh), a_log.reshape(1, nh), d_exp, norm_w.reshape(1, d_inner), sel, expand,
      *cast_arrays)
    return outs[0], finish(outs[1:])


def mamba_layer(xf, act, w_in, conv_w, conv_b, dt_bias, a_log, d_skip, norm_w, w_out, batch, seq_len,
                next_norm_w, cast_in_proj=(), cast_ssd=(), cast_out_proj=()):
    u, rs = act
    nh = dt_bias.shape[0]
    d_inner = norm_w.shape[0]
    bc_cols = 2 * SSM_GROUPS * SSM_STATE
    w_in = w_in.astype(BF16)
    zx, c_in = matmul(u, w_in, col0=0, n=2 * d_inner, tm=1024, tn=1024, row_ssq=rs, single_buffer_a=True,
                      cast_weights=list(cast_in_proj))
    bcm = matmul(u, w_in, col0=2 * d_inner, n=bc_cols, tm=1024, tn=1024, row_ssq=rs)
    dt = matmul(u, w_in, col0=2 * d_inner + bc_cols, n=nh, tm=1024, tn=nh, row_ssq=rs)
    yn, c_ssd = ssd_mixer(zx, bcm, dt, conv_w, conv_b, dt_bias, a_log, d_skip, norm_w, batch, seq_len,
                          cast_weights=(w_out,) + tuple(cast_ssd))
    xf, (xw, ssq), c_out = matmul(yn, c_ssd[0], tm=1024, tn=512, residual=xf, single_buffer_a=True,
                                  norm_w=next_norm_w, cast_weights=list(cast_out_proj))
    return xf, (xw, (ssq, xf.shape[1])), c_in, c_ssd[1:], c_out


def attention_layer(xf, act, cos, sin, w_qkv, lq1, lk1, lq2, lk2, subln_w, w_o, lambda_init, batch, seq_len,
                    next_norm_w, cast_weights=()):
    u, rs = act
    d_model = w_o.shape[0]
    tn = _pick(d_model, 1024)
    nb = w_qkv.shape[1] // tn
    qkv = pipelined_matmul(u, w_qkv.astype(BF16), [(tn, 0)],
                           epilogue=functools.partial(_epilogue_rope, rope_tiles=2 * d_model // tn),
                           row_aux=(cos, sin), out_cols=tn, out_dtype=BF16, nb=nb, row_ssq=rs,
                           single_buffer_a=False, name="qkv_rope")
    o, casted = diff_attention(qkv, lq1, lk1, lq2, lk2, subln_w, lambda_init, batch, seq_len,
                               cast_weights=cast_weights)
    xf, (xw, ssq) = matmul(o, w_o.astype(BF16), tm=1024, tn=512, residual=xf, norm_w=next_norm_w)
    return xf, (xw, (ssq, d_model)), casted


def ffn_layer(xf, act, w_up, conv_w, conv_b, w_down, seq_len, next_norm_w, cast_down_proj=()):
    u, rs = act
    d_ff = w_up.shape[1] // 2
    tn = _pick(d_ff, 256)
    nb = d_ff // tn
    cw = conv_w.astype(F32).reshape(FFN_CONV, 2, nb, tn).transpose(2, 0, 1, 3).reshape(nb, FFN_CONV, 2 * tn)
    cb = conv_b.astype(F32).reshape(2, nb, tn).transpose(1, 0, 2).reshape(nb, 1, 2 * tn)
    g = pipelined_matmul(u, w_up.astype(BF16), [(tn, 0), (tn, nb)], epilogue=_epilogue_conv_gate,
                         col_aux=(cw, cb), out_cols=tn, out_dtype=BF16, nb=nb, halo_seq_len=seq_len,
                         tm=2048, lagged=False, single_buffer_a=True, row_ssq=rs, name="ffn_up")
    outs = matmul(g, w_down.astype(BF16), tm=1024, tn=256, residual=xf, single_buffer_a=True,
                  norm_w=next_norm_w, cast_weights=list(cast_down_proj))
    if next_norm_w is None:
        return outs[0], None, outs[1]
    xf, (xw, ssq), c_down = outs
    return xf, (xw, (ssq, xf.shape[1])), c_down


def kernel(x, positions, l0_norm_mix, l0_m_w_in, l0_m_conv_w, l0_m_conv_b, l0_m_dt_bias, l0_m_a_log, l0_m_d, l0_m_norm, l0_m_w_out, l0_norm_ffn, l0_f_w_up, l0_f_conv_w, l0_f_conv_b, l0_f_w_down, l1_norm_mix, l1_a_w_qkv, l1_a_lq1, l1_a_lk1, l1_a_lq2, l1_a_lk2, l1_a_subln, l1_a_w_o, l1_norm_ffn, l1_f_w_up, l1_f_conv_w, l1_f_conv_b, l1_f_w_down, l2_norm_mix, l2_m_w_in, l2_m_conv_w, l2_m_conv_b, l2_m_dt_bias, l2_m_a_log, l2_m_d, l2_m_norm, l2_m_w_out, l2_norm_ffn, l2_f_w_up, l2_f_conv_w, l2_f_conv_b, l2_f_w_down, l3_norm_mix, l3_a_w_qkv, l3_a_lq1, l3_a_lk1, l3_a_lq2, l3_a_lk2, l3_a_subln, l3_a_w_o, l3_norm_ffn, l3_f_w_up, l3_f_conv_w, l3_f_conv_b, l3_f_w_down, final_norm):
    norm_mix = [l0_norm_mix, l1_norm_mix, l2_norm_mix, l3_norm_mix]
    norm_ffn = [l0_norm_ffn, l1_norm_ffn, l2_norm_ffn, l3_norm_ffn]
    mixer_params = [
        (l0_m_w_in, l0_m_conv_w, l0_m_conv_b, l0_m_dt_bias, l0_m_a_log, l0_m_d, l0_m_norm, l0_m_w_out),
        (l1_a_w_qkv, l1_a_lq1, l1_a_lk1, l1_a_lq2, l1_a_lk2, l1_a_subln, l1_a_w_o),
        (l2_m_w_in, l2_m_conv_w, l2_m_conv_b, l2_m_dt_bias, l2_m_a_log, l2_m_d, l2_m_norm, l2_m_w_out),
        (l3_a_w_qkv, l3_a_lq1, l3_a_lk1, l3_a_lq2, l3_a_lk2, l3_a_subln, l3_a_w_o),
    ]
    ffn_params = [
        (l0_f_w_up, l0_f_conv_w, l0_f_conv_b, l0_f_w_down),
        (l1_f_w_up, l1_f_conv_w, l1_f_conv_b, l1_f_w_down),
        (l2_f_w_up, l2_f_conv_w, l2_f_conv_b, l2_f_w_down),
        (l3_f_w_up, l3_f_conv_w, l3_f_conv_b, l3_f_w_down),
    ]
    batch, seq_len, d_model = x.shape
    xf = x.reshape(batch * seq_len, d_model)
    cos, sin = rope_tables(positions)
    n_layers = len(norm_mix)
    act = (rmsnorm(xf, norm_mix[0], BF16), None)
    for i in range(n_layers):
        w_up, cw, cb, w_down = ffn_params[i]
        nxt = list(mixer_params[i + 1]) if i + 1 < n_layers else None
        if i % N_MIXERS == 0:
            xf, act, (w_up,), (w_down,), c_out = mamba_layer(
                xf, act, *mixer_params[i], batch, seq_len, norm_ffn[i], cast_in_proj=[w_up],
                cast_ssd=[w_down], cast_out_proj=[nxt[0]] if nxt else [])
            if nxt:
                nxt[0] = c_out[0]
            cast_down = [nxt[-1], ffn_params[i + 1][0]] if nxt else []
        else:
            lambda_init = 0.8 - 0.6 * math.exp(-0.3 * i)
            xf, act, (w_up, w_down) = attention_layer(xf, act, cos, sin, *mixer_params[i], lambda_init, batch,
                                                      seq_len, norm_ffn[i], cast_weights=[w_up, w_down])
            cast_down = [nxt[0]] if nxt else []
        xf, act, c_down = ffn_layer(xf, act, w_up, cw, cb, w_down, seq_len,
                                    norm_mix[i + 1] if nxt else None, cast_down_proj=cast_down)
        if nxt:
            nxt[-1 if i % N_MIXERS == 0 else 0] = c_down[0]
            mixer_params[i + 1] = tuple(nxt)
            if len(c_down) > 1:
                ffn_params[i + 1] = (c_down[1],) + tuple(ffn_params[i + 1][1:])
    return rmsnorm(xf, final_norm, F32).reshape(batch, seq_len, d_model)
```

```python
import functools
import math

import jax
import jax.numpy as jnp
from jax import lax
from jax.experimental import pallas as pl
from jax.experimental.pallas import tpu as pltpu

F32 = jnp.float32
BF16 = jnp.bfloat16

EPS = 1e-5
N_MIXERS = 2

SSM_HEADDIM = 64
SSM_GROUPS = 8
SSM_STATE = 128
SSM_CONV = 4
CHUNK = 128

DIFF_HEAD_DIM = 128
ATTN_SCALE = DIFF_HEAD_DIM ** -0.5
Q_PRESCALE = ATTN_SCALE * math.log2(math.e)
ROPE_THETA = 10000.0

FFN_CONV = 3

V7X_VMEM_BYTES = 64 * 1024 * 1024
VMEM_LIMIT_BYTES = V7X_VMEM_BYTES - 8 * 1024 * 1024
SUBLANES = 8
BF16_ROWS = 16
LANES = 128
MXU_COLS = 256
CAST_BLOCK_BYTES = 4 * 1024 * 1024
EPILOGUE_VREGS = 32


def _params(*sem):
    return pltpu.CompilerParams(dimension_semantics=sem, vmem_limit_bytes=VMEM_LIMIT_BYTES)


def _pick(dim, pref):
    if dim <= pref:
        return dim
    t = pref
    while dim % t:
        t //= 2
    return t


def _silu(x):
    return x / (1.0 + jnp.exp(-x))


def _rmsnorm_kernel(x_ref, w_ref, o_ref):
    x = x_ref[...]
    y = x * lax.rsqrt(jnp.mean(x * x, axis=-1, keepdims=True) + EPS)
    o_ref[...] = (y * w_ref[...]).astype(o_ref.dtype)


def rmsnorm(x, w, out_dtype):
    m, d = x.shape
    tm = _pick(m, 256)
    return pl.pallas_call(
        _rmsnorm_kernel,
        grid=(m // tm,),
        in_specs=[pl.BlockSpec((tm, d), lambda i: (i, 0)),
                  pl.BlockSpec((1, d), lambda i: (0, 0))],
        out_specs=pl.BlockSpec((tm, d), lambda i: (i, 0)),
        out_shape=jax.ShapeDtypeStruct((m, d), out_dtype),
        compiler_params=_params("arbitrary"),
        name="rmsnorm",
    )(x, w.reshape(1, d))


def _row_rstd(ssq, d):
    return lax.rsqrt(jnp.sum(ssq, axis=-1, keepdims=True) / d + EPS)


def _mm_kernel(*refs, has_residual, has_norm, scale_d, n_cast):
    it = iter(refs)
    a_ref, w_ref = next(it), next(it)
    r_ref = next(it) if has_residual else None
    nw_ref = next(it) if has_norm else None
    ssq_in_ref = next(it) if scale_d else None
    cast_in = [next(it) for _ in range(n_cast)]
    o_ref = next(it)
    xw_ref, ssq_ref = (next(it), next(it)) if has_norm else (None, None)
    cast_out = [next(it) for _ in range(n_cast)]
    for src, dst in zip(cast_in, cast_out):
        dst[...] = src[...].astype(dst.dtype)
    acc = jnp.dot(a_ref[...], w_ref[...], preferred_element_type=F32)
    if scale_d:
        acc = acc * _row_rstd(ssq_in_ref[...], scale_d)
    if has_residual:
        acc = r_ref[...] + acc
    o_ref[...] = acc.astype(o_ref.dtype)
    if has_norm:
        xw_ref[...] = (acc * nw_ref[...]).astype(xw_ref.dtype)
        sq = acc * acc
        part = functools.reduce(jnp.add, [sq[:, t * LANES:(t + 1) * LANES] for t in range(sq.shape[1] // LANES)])

        @pl.when(pl.program_id(1) == 0)
        def _():
            ssq_ref[...] = jnp.zeros(ssq_ref.shape, F32)

        ssq_ref[...] += part


def matmul(a, w, *, col0=0, n=None, tm, tn, out_dtype=F32, residual=None, single_buffer_a=False,
           cast_weights=None, row_ssq=None, norm_w=None):
    want_casts = cast_weights is not None
    cast_weights = cast_weights or ()
    m, k = a.shape
    n = w.shape[1] - col0 if n is None else n
    tm = _pick(m, tm)
    tn = _pick(n, tn)
    assert col0 % tn == 0
    jb = col0 // tn
    nj = n // tn
    steps = (m // tm) * nj
    a_kwargs = dict(pipeline_mode=pl.Buffered(1)) if single_buffer_a else {}
    in_specs = [pl.BlockSpec((tm, k), lambda i, j: (i, 0), **a_kwargs),
                pl.BlockSpec((k, tn), lambda i, j: (0, j + jb))]
    args = [a, w]
    tile = pl.BlockSpec((tm, tn), lambda i, j: (i, j))
    rows = pl.BlockSpec((tm, LANES), lambda i, j: (i, 0))
    if residual is not None:
        in_specs.append(tile)
        args.append(residual)
    if norm_w is not None:
        in_specs.append(pl.BlockSpec((1, tn), lambda i, j: (0, j)))
        args.append(norm_w.reshape(1, n).astype(F32))
    if row_ssq is not None:
        in_specs.append(rows)
        args.append(row_ssq[0])
    in_kernel = [c.shape[0] % (steps * BF16_ROWS) == 0
                 and (c.shape[0] // steps) * c.shape[1] * 4 <= CAST_BLOCK_BYTES for c in cast_weights]
    cast_arrays = [c for c, ok in zip(cast_weights, in_kernel) if ok]
    cast_specs = [pl.BlockSpec((c.shape[0] // steps, c.shape[1]), lambda i, j: (i * nj + j, 0))
                  for c in cast_arrays]
    norm_specs = [tile, rows] if norm_w is not None else []
    norm_shapes = ([jax.ShapeDtypeStruct((m, n), BF16), jax.ShapeDtypeStruct((m, LANES), F32)]
                   if norm_w is not None else [])
    outs = pl.pallas_call(
        functools.partial(_mm_kernel, has_residual=residual is not None, has_norm=norm_w is not None,
                          scale_d=row_ssq[1] if row_ssq is not None else 0, n_cast=len(cast_arrays)),
        grid=(m // tm, nj),
        in_specs=in_specs + cast_specs,
        out_specs=[tile] + norm_specs + cast_specs,
        out_shape=[jax.ShapeDtypeStruct((m, n), out_dtype)] + norm_shapes
        + [jax.ShapeDtypeStruct(c.shape, BF16) for c in cast_arrays],
        compiler_params=_params("arbitrary", "arbitrary"),
        name="matmul",
    )(*args, *cast_arrays)
    result = [outs[0]]
    n_norm = len(norm_specs)
    if norm_w is not None:
        result.append((outs[1], outs[2]))
    if want_casts:
        done = iter(outs[1 + n_norm:])
        result.append([next(done) if ok else c.astype(BF16) for c, ok in zip(cast_weights, in_kernel)])
    return result[0] if len(result) == 1 else tuple(result)


def _shift_rows(ext, s, halo_rows):
    if s == 0:
        return ext[halo_rows:]
    return pltpu.roll(ext, s, axis=0)[halo_rows:]


def _causal_conv_ext(ext, halo_rows, w, b):
    k_width = w.shape[0]
    y = b + w[0:1] * _shift_rows(ext, k_width - 1, halo_rows)
    for k in range(1, k_width):
        y = y + w[k:k + 1] * _shift_rows(ext, k_width - 1 - k, halo_rows)
    return y


def _row_chunks(rows, cols):
    rc = max(BF16_ROWS, EPILOGUE_VREGS * SUBLANES * LANES // cols)
    rc = min(rc, rows)
    return [(r0, rc) for r0 in range(0, rows, rc)]


def _epilogue_conv_gate(h_ref, halo_rows, aux, o_ref, jp):
    cw, cb = aux[0][...], aux[1][...]
    tn = o_ref.shape[1]

    def chunk(r0, rc):
        ext = h_ref[pl.ds(halo_rows + r0 - SUBLANES, rc + SUBLANES), :]
        y = _causal_conv_ext(ext, SUBLANES, cw, cb)
        o_ref[r0:r0 + rc, :] = (_silu(y[:, :tn]) * y[:, tn:]).astype(o_ref.dtype)

    return [functools.partial(chunk, r0, rc) for r0, rc in _row_chunks(o_ref.shape[0], 2 * tn)]


def _epilogue_rope(h_ref, halo_rows, aux, o_ref, jp, *, rope_tiles):
    cos_ref, sin_ref = aux
    d = DIFF_HEAD_DIM
    is_rope = jp < rope_tiles
    scale = jnp.where(jp < rope_tiles // 2, Q_PRESCALE, 1.0)

    def chunk(r0, rc):
        cos = jnp.where(is_rope, cos_ref[r0:r0 + rc, :] * scale, 1.0)
        sin = jnp.where(is_rope, sin_ref[r0:r0 + rc, :] * scale, 0.0)
        for c in range(o_ref.shape[1] // d):
            t = h_ref[r0:r0 + rc, c * d:(c + 1) * d]
            rot = pltpu.roll(t, d // 2, axis=1)
            o_ref[r0:r0 + rc, c * d:(c + 1) * d] = (t * cos + rot * sin).astype(o_ref.dtype)

    return [functools.partial(chunk, r0, rc) for r0, rc in _row_chunks(o_ref.shape[0], 4 * d)]


def _pipelined_kernel(*refs, epilogue, n_w, n_aux, use_halo, lagged, scale_d, nb, n_tiles, tiles_per_seq):
    it = iter(refs)
    a_ref = next(it)
    ah_ref = next(it) if use_halo else None
    w_refs = [next(it) for _ in range(n_w)]
    ssq_ref = next(it) if scale_d else None
    ssqh_ref = next(it) if (scale_d and use_halo) else None
    aux = [next(it) for _ in range(n_aux)]
    o_ref = next(it)
    aext_ref = next(it) if use_halo else None
    h = [next(it) for _ in range(2 if lagged else 1)]
    halo_rows = BF16_ROWS if use_halo else 0

    s = pl.program_id(0)
    t = jnp.minimum(s, n_tiles - 1)
    i = t // nb
    j = t % nb

    if use_halo:
        @pl.when((j == 0) & (s < n_tiles))
        def _():
            seq_start = (i % tiles_per_seq) == 0
            halo = ah_ref[...]
            aext_ref[0:BF16_ROWS, :] = jnp.where(seq_start, jnp.zeros_like(halo), halo)
            aext_ref[BF16_ROWS:, :] = a_ref[...]

    def matmul_into(h_ref):
        lhs_ref = aext_ref if use_halo else a_ref
        rstd = None
        if scale_d:
            rstd = _row_rstd(ssq_ref[...], scale_d)
            if use_halo:
                rstd = jnp.concatenate([_row_rstd(ssqh_ref[...], scale_d), rstd], axis=0)
        col = 0
        for w_ref in w_refs:
            wn = w_ref.shape[1]
            step_cols = min(wn, MXU_COLS)
            for c0 in range(0, wn, step_cols):
                prod = jnp.dot(lhs_ref[...], w_ref[:, c0:c0 + step_cols], preferred_element_type=F32)
                h_ref[:, col + c0:col + c0 + step_cols] = prod if rstd is None else prod * rstd
            col += wn

    if not lagged:
        matmul_into(h[0])
        for chunk in epilogue(h[0], halo_rows, aux, o_ref, j):
            chunk()
        return

    jp = jnp.maximum(s - 1, 0) % nb
    slot = s % 2

    @pl.when(s == 0)
    def _():
        h[1][...] = jnp.zeros(h[1].shape, F32)

    def step(cur, prev):
        for chunk in epilogue(h[prev], halo_rows, aux, o_ref, jp):
            chunk()
        matmul_into(h[cur])

    @pl.when(slot == 0)
    def _():
        step(0, 1)

    @pl.when(slot == 1)
    def _():
        step(1, 0)


def pipelined_matmul(a, w, w_col_blocks, *, epilogue, col_aux=(), row_aux=(), out_cols, out_dtype,
                     nb, halo_seq_len=None, tm=1024, single_buffer_a=True, lagged=True, row_ssq=None,
                     name):
    m, k = a.shape
    use_halo = halo_seq_len is not None
    tm = _pick(halo_seq_len if use_halo else m, tm)
    hb = tm // BF16_ROWS
    n_tiles = (m // tm) * nb
    hrows = tm + (BF16_ROWS if use_halo else 0)
    wn = sum(width for width, _ in w_col_blocks)

    def cur(s):
        t = jnp.minimum(s, n_tiles - 1)
        return t // nb, t % nb

    def prev(s):
        t = jnp.maximum(s - 1, 0) if lagged else s
        return t // nb, t % nb

    a_kwargs = dict(pipeline_mode=pl.Buffered(1)) if single_buffer_a else {}
    in_specs = [pl.BlockSpec((tm, k), lambda s: (cur(s)[0], 0), **a_kwargs)]
    args = [a]
    if use_halo:
        in_specs.append(pl.BlockSpec((BF16_ROWS, k), lambda s: (jnp.maximum(cur(s)[0] * hb - 1, 0), 0)))
        args.append(a)
    for width, first in w_col_blocks:
        in_specs.append(pl.BlockSpec((k, width), lambda s, first=first: (0, first + cur(s)[1])))
        args.append(w)
    if row_ssq is not None:
        in_specs.append(pl.BlockSpec((tm, LANES), lambda s: (cur(s)[0], 0)))
        args.append(row_ssq[0])
        if use_halo:
            in_specs.append(pl.BlockSpec((BF16_ROWS, LANES), lambda s: (jnp.maximum(cur(s)[0] * hb - 1, 0), 0)))
            args.append(row_ssq[0])
    for arr in col_aux:
        in_specs.append(pl.BlockSpec((None,) + arr.shape[1:], lambda s: (prev(s)[1], 0, 0)))
        args.append(arr)
    for arr in row_aux:
        in_specs.append(pl.BlockSpec((tm, arr.shape[1]), lambda s: (prev(s)[0], 0)))
        args.append(arr)
    scratch = ([pltpu.VMEM((hrows, k), BF16)] if use_halo else [])
    scratch += [pltpu.VMEM((hrows, wn), F32)] * (2 if lagged else 1)
    kern = functools.partial(
        _pipelined_kernel, epilogue=epilogue, n_w=len(w_col_blocks), n_aux=len(col_aux) + len(row_aux),
        use_halo=use_halo, lagged=lagged, scale_d=row_ssq[1] if row_ssq is not None else 0, nb=nb,
        n_tiles=n_tiles,
        tiles_per_seq=(halo_seq_len // tm) if use_halo else 1)
    return pl.pallas_call(
        kern,
        grid=(n_tiles + (1 if lagged else 0),),
        in_specs=in_specs,
        out_specs=pl.BlockSpec((tm, out_cols), lambda s: prev(s)),
        out_shape=jax.ShapeDtypeStruct((m, nb * out_cols), out_dtype),
        scratch_shapes=scratch,
        compiler_params=_params("arbitrary"),
        name=name,
    )(*args)


def _rope_table_kernel(pos_ref, inv_ref, cos_ref, sin_ref):
    freqs = pos_ref[...].astype(F32) * inv_ref[...]
    lane = lax.broadcasted_iota(jnp.int32, freqs.shape, 1)
    sign = jnp.where(lane < DIFF_HEAD_DIM // 2, -1.0, 1.0)
    cos_ref[...] = jnp.cos(freqs)
    sin_ref[...] = jnp.sin(freqs) * sign


def rope_tables(positions):
    m = positions.size
    inv = 1.0 / (ROPE_THETA ** (jnp.arange(0, DIFF_HEAD_DIM, 2, dtype=F32) / DIFF_HEAD_DIM))
    inv_full = jnp.concatenate([inv, inv]).reshape(1, DIFF_HEAD_DIM)
    tm = _pick(m, 1024)
    shape = jax.ShapeDtypeStruct((m, DIFF_HEAD_DIM), F32)
    return pl.pallas_call(
        _rope_table_kernel,
        grid=(m // tm,),
        in_specs=[pl.BlockSpec((tm, 1), lambda i: (i, 0)),
                  pl.BlockSpec((1, DIFF_HEAD_DIM), lambda i: (0, 0))],
        out_specs=[pl.BlockSpec((tm, DIFF_HEAD_DIM), lambda i: (i, 0))] * 2,
        out_shape=[shape, shape],
        compiler_params=_params("arbitrary"),
        name="rope_tables",
    )(positions.reshape(m, 1), inv_full)


def _diff_attn_kernel(q_ref, k_ref, v_ref, lq1_ref, lk1_ref, lq2_ref, lk2_ref, sw_ref, o_ref,
                      m_ref, l_ref, acc_ref, *, lambda_init, tq):
    qi = pl.program_id(2)
    d = DIFF_HEAD_DIM
    q = q_ref[...]

    m_ref[...] = jnp.full(m_ref.shape, -jnp.inf, F32)
    l_ref[...] = jnp.zeros(l_ref.shape, F32)
    acc_ref[...] = jnp.zeros(acc_ref.shape, F32)

    def kv_block(kb, masked):
        start = pl.multiple_of(kb * tq, tq)
        k = k_ref[pl.ds(start, tq), :]
        v = v_ref[pl.ds(start, tq), :]
        scores = [lax.dot_general(q[:, c * d:(c + 1) * d], k[:, c * d:(c + 1) * d],
                                  (((1,), (1,)), ((), ())), preferred_element_type=F32) for c in range(2)]
        probs = []
        alphas = []
        for c in range(2):
            s = scores[c]
            if masked:
                row = lax.broadcasted_iota(jnp.int32, s.shape, 0)
                col = lax.broadcasted_iota(jnp.int32, s.shape, 1)
                s = jnp.where(col <= row, s, -jnp.inf)
            chunks = [s[:, t * LANES:(t + 1) * LANES] for t in range(tq // LANES)]
            cmax = functools.reduce(jnp.maximum, chunks)
            m_old = m_ref[c]
            m_new = jnp.maximum(m_old, jnp.max(cmax, axis=-1, keepdims=True))
            alpha = jnp.exp2(m_old - m_new)
            ps = [jnp.exp2(ch - m_new) for ch in chunks]
            l_ref[c] = alpha * l_ref[c] + functools.reduce(jnp.add, ps)
            m_ref[c] = m_new
            probs.append(jnp.concatenate(ps, axis=1).astype(v.dtype))
            alphas.append(alpha)
        pv = jnp.dot(jnp.concatenate(probs, axis=0), v, preferred_element_type=F32)
        for c in range(2):
            a2 = jnp.concatenate([alphas[c], alphas[c]], axis=1)
            acc_ref[c] = a2 * acc_ref[c] + pv[c * tq:(c + 1) * tq]

    def body(kb, carry):
        kv_block(kb, False)
        return carry

    lax.fori_loop(0, qi, body, 0)
    kv_block(qi, True)

    lam = (jnp.exp(jnp.sum(lq1_ref[...] * lk1_ref[...], axis=-1, keepdims=True))
           - jnp.exp(jnp.sum(lq2_ref[...] * lk2_ref[...], axis=-1, keepdims=True)) + lambda_init)
    l0 = jnp.sum(l_ref[0], axis=-1, keepdims=True)
    l1 = jnp.sum(l_ref[1], axis=-1, keepdims=True)
    o = acc_ref[0] / l0 - lam * (acc_ref[1] / l1)
    o = o * lax.rsqrt(jnp.mean(o * o, axis=-1, keepdims=True) + EPS)
    o_ref[...] = ((o * sw_ref[...]) * (1.0 - lambda_init)).astype(o_ref.dtype)


def _cast_rows(w, steps):
    k, n = w.shape
    for rows in range(BF16_ROWS, k + 1, BF16_ROWS):
        if k % rows == 0 and k // rows <= steps:
            return rows if rows * n * 4 <= CAST_BLOCK_BYTES else None
    return None


def _plan_casts(weights, steps, lin):
    rows = [None if w.dtype == BF16 else _cast_rows(w, steps) for w in weights]
    arrays, specs, spans = [], [], []
    at_end = False
    for w, r in zip(weights, rows):
        if not r:
            continue
        nb = w.shape[0] // r
        first = steps - nb if at_end else 0
        at_end = not at_end if nb < steps else at_end
        arrays.append(w)
        spans.append((first, nb))
        specs.append(pl.BlockSpec(
            (r, w.shape[1]),
            lambda *ids, first=first, nb=nb: (jnp.clip(lin(*ids) - first, 0, nb - 1), 0)))

    def finish(outputs):
        done = iter(outputs)
        return [next(done) if r else w.astype(BF16) for w, r in zip(weights, rows)]

    return arrays, specs, spans, finish


def _run_casts(step, spans, cast_in, cast_out):
    for (first, nb), src, dst in zip(spans, cast_in, cast_out):
        @pl.when((step >= first) & (step < first + nb))
        def _():
            dst[...] = src[...].astype(dst.dtype)


def diff_attention(qkv, lq1, lk1, lq2, lk2, subln_w, lambda_init, batch, seq_len, *, tq=1024,
                   cast_weights=()):
    m, n = qkv.shape
    d_model = n // 3
    hw = 2 * DIFF_HEAD_DIM
    heads = d_model // hw
    tq = _pick(seq_len, tq)
    nq = seq_len // tq

    def lin(b, h, qi):
        return (b * heads + h) * nq + qi

    cast_arrays, cast_specs, spans, finish = _plan_casts(cast_weights, batch * heads * nq, lin)
    n_cast = len(cast_arrays)

    def kern(*refs):
        main_in, cast_in = refs[:8], refs[8:8 + n_cast]
        o_ref = refs[8 + n_cast]
        cast_out = refs[9 + n_cast:9 + 2 * n_cast]
        scratch = refs[9 + 2 * n_cast:]
        _run_casts(lin(pl.program_id(0), pl.program_id(1), pl.program_id(2)), spans, cast_in, cast_out)
        _diff_attn_kernel(*main_in, o_ref, *scratch, lambda_init=lambda_init, tq=tq)

    vec = lambda a: a.reshape(1, -1)
    lam_spec = pl.BlockSpec((1, DIFF_HEAD_DIM), lambda b, h, qi: (0, 0))
    outs = pl.pallas_call(
        kern,
        grid=(batch, heads, nq),
        in_specs=[
            pl.BlockSpec((tq, hw), lambda b, h, qi: (b * nq + qi, h)),
            pl.BlockSpec((seq_len, hw), lambda b, h, qi: (b, heads + h)),
            pl.BlockSpec((seq_len, hw), lambda b, h, qi: (b, 2 * heads + h)),
            lam_spec, lam_spec, lam_spec, lam_spec,
            pl.BlockSpec((1, hw), lambda b, h, qi: (0, 0)),
        ] + cast_specs,
        out_specs=[pl.BlockSpec((tq, hw), lambda b, h, qi: (b * nq + qi, h))] + cast_specs,
        out_shape=[jax.ShapeDtypeStruct((m, d_model), BF16)]
        + [jax.ShapeDtypeStruct(w.shape, BF16) for w in cast_arrays],
        scratch_shapes=[pltpu.VMEM((2, tq, LANES), F32), pltpu.VMEM((2, tq, LANES), F32),
                        pltpu.VMEM((2, tq, hw), F32)],
        compiler_params=_params("arbitrary", "arbitrary", "arbitrary"),
        name="diff_attention",
    )(qkv, qkv, qkv, vec(lq1), vec(lk1), vec(lq2), vec(lk2), vec(subln_w), *cast_arrays)
    return outs[0], finish(outs[1:])


def _dot_exact(a, b):
    return jnp.dot(a, b, preferred_element_type=F32, precision=lax.Precision.HIGHEST)


def _ssd_kernel(z_ref, x_ref, xh_ref, b_ref, bh_ref, c_ref, ch_ref, dt_ref,
                cwx_ref, cwb_ref, cwc_ref, cbx_ref, cbb_ref, cbc_ref,
                dtb_ref, alog_ref, dexp_ref, nw_ref, sel_ref, expand_ref, o_ref, state_ref, *, heads_per_group):
    g = pl.program_id(1)
    c = pl.program_id(2)
    t = CHUNK
    hpg = heads_per_group
    p = SSM_HEADDIM
    gw = hpg * p
    nh = dt_ref.shape[1]

    @pl.when(c == 0)
    def _():
        state_ref[...] = jnp.zeros(state_ref.shape, F32)

    seq_start = c == 0

    def conv_silu(cur_ref, halo_ref, w_ref, bias_ref):
        halo = halo_ref[...]
        halo = jnp.where(seq_start, jnp.zeros_like(halo), halo)
        ext = jnp.concatenate([halo, cur_ref[...]], axis=0)
        return _silu(_causal_conv_ext(ext, SUBLANES, w_ref[...], bias_ref[...]))

    xc = conv_silu(x_ref, xh_ref, cwx_ref, cbx_ref)
    bc = conv_silu(b_ref, bh_ref, cwb_ref, cbb_ref)
    cc = conv_silu(c_ref, ch_ref, cwc_ref, cbc_ref)

    dtr = dt_ref[...] + dtb_ref[...]
    dtv = jnp.maximum(dtr, 0.0) + jnp.log1p(jnp.exp(-jnp.abs(dtr)))
    a_all = dtv * (-jnp.exp(alog_ref[...]))

    row_t = lax.broadcasted_iota(jnp.int32, (t, t), 0)
    col_t = lax.broadcasted_iota(jnp.int32, (t, t), 1)
    causal = col_t <= row_t
    a_cs_all = _dot_exact(jnp.where(causal, 1.0, 0.0), a_all)
    a_last = a_cs_all[t - 1:t, :]

    a_cs = _dot_exact(a_cs_all, sel_ref[...])
    a_cs_t = a_cs.T

    stacked = jnp.concatenate([dtv, jnp.exp(a_cs_all), jnp.exp(a_last - a_cs_all)], axis=0)
    hi = stacked.astype(BF16)
    lo = (stacked - hi.astype(F32)).astype(BF16)
    expanded = jnp.dot(jnp.concatenate([hi, lo], axis=1), expand_ref[...],
                       preferred_element_type=F32)
    dt_e = expanded[0:t]
    exp_acs_e = expanded[t:2 * t]
    dec_e = expanded[2 * t:3 * t]

    xdt = xc * dt_e
    xdt_b = xdt.astype(BF16)
    bb = bc.astype(BF16)
    cb = cc.astype(BF16)
    cbm = lax.dot_general(cb, bb, (((1,), (1,)), ((), ())), preferred_element_type=F32)

    state = state_ref[...]
    y = jnp.dot(cb, state.astype(BF16), preferred_element_type=F32) * exp_acs_e

    lane = lax.broadcasted_iota(jnp.int32, (t, 2 * p), 1)
    pieces = []
    for pair in range(hpg // 2):
        x_pair = xdt_b[:, pair * 2 * p:(pair + 1) * 2 * p]
        outs = []
        for hh in (2 * pair, 2 * pair + 1):
            seg = a_cs[:, hh:hh + 1] - a_cs_t[hh:hh + 1, :]
            decay = jnp.exp(jnp.where(causal, seg, -jnp.inf))
            outs.append(jnp.dot((cbm * decay).astype(BF16), x_pair, preferred_element_type=F32))
        pieces.append(jnp.where(lane < p, outs[0], outs[1]))
    y = y + jnp.concatenate(pieces, axis=1) + dexp_ref[...] * xc

    new_contrib = jnp.dot(bc.T.astype(BF16), (xdt * dec_e).astype(BF16), preferred_element_type=F32)
    state_ref[...] = state * exp_acs_e[t - 1:t, :] + new_contrib

    gn = y * _silu(z_ref[...])
    gn = gn * lax.rsqrt(jnp.mean(gn * gn, axis=-1, keepdims=True) + EPS)
    o_ref[...] = (gn * nw_ref[...]).astype(o_ref.dtype)


def ssd_mixer(zx, bcm, dt, conv_w, conv_b, dt_bias, a_log, d_skip, norm_w, batch, seq_len, cast_weights=()):
    m = zx.shape[0]
    d_inner = zx.shape[1] // 2
    nh = dt.shape[1]
    hpg = nh // SSM_GROUPS
    gw = d_inner // SSM_GROUPS
    n = SSM_STATE
    t = CHUNK
    nc = seq_len // t
    hb = t // SUBLANES
    ng = SSM_GROUPS
    xb0 = d_inner // gw
    cwb0 = d_inner // n

    def row(b, g, c):
        return b * nc + c

    def halo(b, g, c):
        return jnp.maximum((b * nc + c) * hb - 1, 0)

    d_exp = jnp.repeat(d_skip.astype(F32), SSM_HEADDIM).reshape(1, d_inner)
    conv_b2 = conv_b.reshape(1, -1)
    head = jnp.arange(nh)
    sel = (head[None, :, None] == (jnp.arange(ng) * hpg)[:, None, None] + head[None, None, :]) \
        & (head[None, None, :] < hpg)
    chan_head = jnp.arange(gw) // SSM_HEADDIM
    expand = head[None, :, None] == (jnp.arange(ng) * hpg)[:, None, None] + chan_head[None, None, :]
    sel = sel.astype(F32)
    expand = jnp.concatenate([expand, expand], axis=1).astype(BF16)
    def lin(b, g, c):
        return (b * ng + g) * nc + c

    cast_arrays, cast_specs, spans, finish = _plan_casts(cast_weights, batch * ng * nc, lin)
    n_cast = len(cast_arrays)
    n_main = 20

    def kern(*refs):
        main_in, cast_in = refs[:n_main], refs[n_main:n_main + n_cast]
        o_ref = refs[n_main + n_cast]
        cast_out = refs[n_main + n_cast + 1:n_main + 2 * n_cast + 1]
        scratch = refs[n_main + 2 * n_cast + 1:]
        _run_casts(lin(pl.program_id(0), pl.program_id(1), pl.program_id(2)), spans, cast_in, cast_out)
        _ssd_kernel(*main_in, o_ref, *scratch, heads_per_group=hpg)

    outs = pl.pallas_call(
        kern,
        grid=(batch, ng, nc),
        in_specs=[
            pl.BlockSpec((t, gw), lambda b, g, c: (row(b, g, c), g)),
            pl.BlockSpec((t, gw), lambda b, g, c: (row(b, g, c), xb0 + g)),
            pl.BlockSpec((SUBLANES, gw), lambda b, g, c: (halo(b, g, c), xb0 + g)),
            pl.BlockSpec((t, n), lambda b, g, c: (row(b, g, c), g)),
            pl.BlockSpec((SUBLANES, n), lambda b, g, c: (halo(b, g, c), g)),
            pl.BlockSpec((t, n), lambda b, g, c: (row(b, g, c), ng + g)),
            pl.BlockSpec((SUBLANES, n), lambda b, g, c: (halo(b, g, c), ng + g)),
            pl.BlockSpec((t, nh), lambda b, g, c: (row(b, g, c), 0)),
            pl.BlockSpec((SSM_CONV, gw), lambda b, g, c: (0, g)),
            pl.BlockSpec((SSM_CONV, n), lambda b, g, c: (0, cwb0 + g)),
            pl.BlockSpec((SSM_CONV, n), lambda b, g, c: (0, cwb0 + ng + g)),
            pl.BlockSpec((1, gw), lambda b, g, c: (0, g)),
            pl.BlockSpec((1, n), lambda b, g, c: (0, cwb0 + g)),
            pl.BlockSpec((1, n), lambda b, g, c: (0, cwb0 + ng + g)),
            pl.BlockSpec((1, nh), lambda b, g, c: (0, 0)),
            pl.BlockSpec((1, nh), lambda b, g, c: (0, 0)),
            pl.BlockSpec((1, gw), lambda b, g, c: (0, g)),
            pl.BlockSpec((1, gw), lambda b, g, c: (0, g)),
            pl.BlockSpec((None, nh, nh), lambda b, g, c: (g, 0, 0)),
            pl.BlockSpec((None, 2 * nh, gw), lambda b, g, c: (g, 0, 0)),
        ] + cast_specs,
        out_specs=[pl.BlockSpec((t, gw), lambda b, g, c: (row(b, g, c), g))] + cast_specs,
        out_shape=[jax.ShapeDtypeStruct((m, d_inner), BF16)]
        + [jax.ShapeDtypeStruct(w.shape, BF16) for w in cast_arrays],
        scratch_shapes=[pltpu.VMEM((n, gw), F32)],
        compiler_params=_params("arbitrary", "arbitrary", "arbitrary"),
        name="ssd_mixer",
    )(zx, zx, zx, bcm, bcm, bcm, bcm, dt, conv_w, conv_w, conv_w, conv_b2, conv_b2, conv_b2,
      dt_bias.reshape(1, nh), a_log.reshape(1, nh), d_exp, norm_w.reshape(1, d_inner), sel, expand,
      *cast_arrays)
    return outs[0], finish(outs[1:])


def mamba_layer(xf, act, w_in, conv_w, conv_b, dt_bias, a_log, d_skip, norm_w, w_out, batch, seq_len,
                next_norm_w, cast_in_proj=(), cast_ssd=(), cast_out_proj=()):
    u, rs = act
    nh = dt_bias.shape[0]
    d_inner = norm_w.shape[0]
    bc_cols = 2 * SSM_GROUPS * SSM_STATE
    w_in = w_in.astype(BF16)
    zx, c_in = matmul(u, w_in, col0=0, n=2 * d_inner, tm=1024, tn=1024, row_ssq=rs, single_buffer_a=True,
                      cast_weights=list(cast_in_proj))
    bcm = matmul(u, w_in, col0=2 * d_inner, n=bc_cols, tm=1024, tn=1024, row_ssq=rs)
    dt = matmul(u, w_in, col0=2 * d_inner + bc_cols, n=nh, tm=1024, tn=nh, row_ssq=rs)
    yn, c_ssd = ssd_mixer(zx, bcm, dt, conv_w, conv_b, dt_bias, a_log, d_skip, norm_w, batch, seq_len,
                          cast_weights=(w_out,) + tuple(cast_ssd))
    xf, (xw, ssq), c_out = matmul(yn, c_ssd[0], tm=1024, tn=512, residual=xf, single_buffer_a=True,
                                  norm_w=next_norm_w, cast_weights=list(cast_out_proj))
    return xf, (xw, (ssq, xf.shape[1])), c_in, c_ssd[1:], c_out


def attention_layer(xf, act, cos, sin, w_qkv, lq1, lk1, lq2, lk2, subln_w, w_o, lambda_init, batch, seq_len,
                    next_norm_w, cast_weights=()):
    u, rs = act
    d_model = w_o.shape[0]
    tn = _pick(d_model, 1024)
    nb = w_qkv.shape[1] // tn
    qkv = pipelined_matmul(u, w_qkv.astype(BF16), [(tn, 0)],
                           epilogue=functools.partial(_epilogue_rope, rope_tiles=2 * d_model // tn),
                           row_aux=(cos, sin), out_cols=tn, out_dtype=BF16, nb=nb, row_ssq=rs,
                           single_buffer_a=False, name="qkv_rope")
    o, casted = diff_attention(qkv, lq1, lk1, lq2, lk2, subln_w, lambda_init, batch, seq_len,
                               cast_weights=cast_weights)
    xf, (xw, ssq) = matmul(o, w_o.astype(BF16), tm=1024, tn=512, residual=xf, norm_w=next_norm_w)
    return xf, (xw, (ssq, d_model)), casted


def ffn_layer(xf, act, w_up, conv_w, conv_b, w_down, seq_len, next_norm_w, cast_down_proj=()):
    u, rs = act
    d_ff = w_up.shape[1] // 2
    tn = _pick(d_ff, 256)
    nb = d_ff // tn
    cw = conv_w.astype(F32).reshape(FFN_CONV, 2, nb, tn).transpose(2, 0, 1, 3).reshape(nb, FFN_CONV, 2 * tn)
    cb = conv_b.astype(F32).reshape(2, nb, tn).transpose(1, 0, 2).reshape(nb, 1, 2 * tn)
    g = pipelined_matmul(u, w_up.astype(BF16), [(tn, 0), (tn, nb)], epilogue=_epilogue_conv_gate,
                         col_aux=(cw, cb), out_cols=tn, out_dtype=BF16, nb=nb, halo_seq_len=seq_len,
                         tm=2048, lagged=False, single_buffer_a=True, row_ssq=rs, name="ffn_up")
    outs = matmul(g, w_down.astype(BF16), tm=1024, tn=256, residual=xf, single_buffer_a=True,
                  norm_w=next_norm_w, cast_weights=list(cast_down_proj))
    if next_norm_w is None:
        return outs[0], None, outs[1]
    xf, (xw, ssq), c_down = outs
    return xf, (xw, (ssq, xf.shape[1])), c_down


def kernel(x, positions, l0_norm_mix, l0_m_w_in, l0_m_conv_w, l0_m_conv_b, l0_m_dt_bias, l0_m_a_log, l0_m_d, l0_m_norm, l0_m_w_out, l0_norm_ffn, l0_f_w_up, l0_f_conv_w, l0_f_conv_b, l0_f_w_down, l1_norm_mix, l1_a_w_qkv, l1_a_lq1, l1_a_lk1, l1_a_lq2, l1_a_lk2, l1_a_subln, l1_a_w_o, l1_norm_ffn, l1_f_w_up, l1_f_conv_w, l1_f_conv_b, l1_f_w_down, l2_norm_mix, l2_m_w_in, l2_m_conv_w, l2_m_conv_b, l2_m_dt_bias, l2_m_a_log, l2_m_d, l2_m_norm, l2_m_w_out, l2_norm_ffn, l2_f_w_up, l2_f_conv_w, l2_f_conv_b, l2_f_w_down, l3_norm_mix, l3_a_w_qkv, l3_a_lq1, l3_a_lk1, l3_a_lq2, l3_a_lk2, l3_a_subln, l3_a_w_o, l3_norm_ffn, l3_f_w_up, l3_f_conv_w, l3_f_conv_b, l3_f_w_down, final_norm):
    norm_mix = [l0_norm_mix, l1_norm_mix, l2_norm_mix, l3_norm_mix]
    norm_ffn = [l0_norm_ffn, l1_norm_ffn, l2_norm_ffn, l3_norm_ffn]
    mixer_params = [
        (l0_m_w_in, l0_m_conv_w, l0_m_conv_b, l0_m_dt_bias, l0_m_a_log, l0_m_d, l0_m_norm, l0_m_w_out),
        (l1_a_w_qkv, l1_a_lq1, l1_a_lk1, l1_a_lq2, l1_a_lk2, l1_a_subln, l1_a_w_o),
        (l2_m_w_in, l2_m_conv_w, l2_m_conv_b, l2_m_dt_bias, l2_m_a_log, l2_m_d, l2_m_norm, l2_m_w_out),
        (l3_a_w_qkv, l3_a_lq1, l3_a_lk1, l3_a_lq2, l3_a_lk2, l3_a_subln, l3_a_w_o),
    ]
    ffn_params = [
        (l0_f_w_up, l0_f_conv_w, l0_f_conv_b, l0_f_w_down),
        (l1_f_w_up, l1_f_conv_w, l1_f_conv_b, l1_f_w_down),
        (l2_f_w_up, l2_f_conv_w, l2_f_conv_b, l2_f_w_down),
        (l3_f_w_up, l3_f_conv_w, l3_f_conv_b, l3_f_w_down),
    ]
    batch, seq_len, d_model = x.shape
    xf = x.reshape(batch * seq_len, d_model)
    cos, sin = rope_tables(positions)
    n_layers = len(norm_mix)
    act = (rmsnorm(xf, norm_mix[0], BF16), None)
    for i in range(n_layers):
        w_up, cw, cb, w_down = ffn_params[i]
        nxt = list(mixer_params[i + 1]) if i + 1 < n_layers else None
        if i % N_MIXERS == 0:
            xf, act, (w_up,), (w_down,), c_out = mamba_layer(
                xf, act, *mixer_params[i], batch, seq_len, norm_ffn[i], cast_in_proj=[w_up],
                cast_ssd=[w_down], cast_out_proj=[nxt[0]] if nxt else [])
            if nxt:
                nxt[0] = c_out[0]
            cast_down = [nxt[-1], ffn_params[i + 1][0]] if nxt else []
        else:
            lambda_init = 0.8 - 0.6 * math.exp(-0.3 * i)
            xf, act, (w_up, w_down) = attention_layer(xf, act, cos, sin, *mixer_params[i], lambda_init, batch,
                                                      seq_len, norm_ffn[i], cast_weights=[w_up, w_down])
            cast_down = [nxt[0]] if nxt else []
        xf, act, c_down = ffn_layer(xf, act, w_up, cw, cb, w_down, seq_len,
                                    norm_mix[i + 1] if nxt else None, cast_down_proj=cast_down)
        if nxt:
            nxt[-1 if i % N_MIXERS == 0 else 0] = c_down[0]
            mixer_params[i + 1] = tuple(nxt)
            if len(c_down) > 1:
                ffn_params[i + 1] = (c_down[1],) + tuple(ffn_params[i + 1][1:])
    return rmsnorm(xf, final_norm, F32).reshape(batch, seq_len, d_model)
```

```python
import functools
import math

import jax
import jax.numpy as jnp
from jax import lax
from jax.experimental import pallas as pl
from jax.experimental.pallas import tpu as pltpu

F32 = jnp.float32
BF16 = jnp.bfloat16

EPS = 1e-5
N_MIXERS = 2

SSM_HEADDIM = 64
SSM_GROUPS = 8
SSM_STATE = 128
SSM_CONV = 4
CHUNK = 128

DIFF_HEAD_DIM = 128
ATTN_SCALE = DIFF_HEAD_DIM ** -0.5
Q_PRESCALE = ATTN_SCALE * math.log2(math.e)
ROPE_THETA = 10000.0

FFN_CONV = 3

V7X_VMEM_BYTES = 64 * 1024 * 1024
VMEM_LIMIT_BYTES = V7X_VMEM_BYTES - 8 * 1024 * 1024
SUBLANES = 8
BF16_ROWS = 16
LANES = 128
MXU_COLS = 256
CAST_BLOCK_BYTES = 4 * 1024 * 1024
EPILOGUE_VREGS = 32


def _params(*sem):
    return pltpu.CompilerParams(dimension_semantics=sem, vmem_limit_bytes=VMEM_LIMIT_BYTES)


def _pick(dim, pref):
    if dim <= pref:
        return dim
    t = pref
    while dim % t:
        t //= 2
    return t


def _silu(x):
    return x / (1.0 + jnp.exp(-x))


def _rmsnorm_kernel(x_ref, w_ref, o_ref):
    x = x_ref[...]
    y = x * lax.rsqrt(jnp.mean(x * x, axis=-1, keepdims=True) + EPS)
    o_ref[...] = (y * w_ref[...]).astype(o_ref.dtype)


def rmsnorm(x, w, out_dtype):
    m, d = x.shape
    tm = _pick(m, 256)
    return pl.pallas_call(
        _rmsnorm_kernel,
        grid=(m // tm,),
        in_specs=[pl.BlockSpec((tm, d), lambda i: (i, 0)),
                  pl.BlockSpec((1, d), lambda i: (0, 0))],
        out_specs=pl.BlockSpec((tm, d), lambda i: (i, 0)),
        out_shape=jax.ShapeDtypeStruct((m, d), out_dtype),
        compiler_params=_params("arbitrary"),
        name="rmsnorm",
    )(x, w.reshape(1, d))


def _row_rstd(ssq, d):
    return lax.rsqrt(jnp.sum(ssq, axis=-1, keepdims=True) / d + EPS)


def _mm_kernel(*refs, has_residual, has_norm, scale_d, n_cast):
    it = iter(refs)
    a_ref, w_ref = next(it), next(it)
    r_ref = next(it) if has_residual else None
    nw_ref = next(it) if has_norm else None
    ssq_in_ref = next(it) if scale_d else None
    cast_in = [next(it) for _ in range(n_cast)]
    o_ref = next(it)
    xw_ref, ssq_ref = (next(it), next(it)) if has_norm else (None, None)
    cast_out = [next(it) for _ in range(n_cast)]
    for src, dst in zip(cast_in, cast_out):
        dst[...] = src[...].astype(dst.dtype)
    acc = jnp.dot(a_ref[...], w_ref[...], preferred_element_type=F32)
    if scale_d:
        acc = acc * _row_rstd(ssq_in_ref[...], scale_d)
    if has_residual:
        acc = r_ref[...] + acc
    o_ref[...] = acc.astype(o_ref.dtype)
    if has_norm:
        xw_ref[...] = (acc * nw_ref[...]).astype(xw_ref.dtype)
        sq = acc * acc
        part = functools.reduce(jnp.add, [sq[:, t * LANES:(t + 1) * LANES] for t in range(sq.shape[1] // LANES)])

        @pl.when(pl.program_id(1) == 0)
        def _():
            ssq_ref[...] = jnp.zeros(ssq_ref.shape, F32)

        ssq_ref[...] += part


def matmul(a, w, *, col0=0, n=None, tm, tn, out_dtype=F32, residual=None, single_buffer_a=False,
           cast_weights=None, row_ssq=None, norm_w=None):
    want_casts = cast_weights is not None
    cast_weights = cast_weights or ()
    m, k = a.shape
    n = w.shape[1] - col0 if n is None else n
    tm = _pick(m, tm)
    tn = _pick(n, tn)
    assert col0 % tn == 0
    jb = col0 // tn
    nj = n // tn
    steps = (m // tm) * nj
    a_kwargs = dict(pipeline_mode=pl.Buffered(1)) if single_buffer_a else {}
    in_specs = [pl.BlockSpec((tm, k), lambda i, j: (i, 0), **a_kwargs),
                pl.BlockSpec((k, tn), lambda i, j: (0, j + jb))]
    args = [a, w]
    tile = pl.BlockSpec((tm, tn), lambda i, j: (i, j))
    rows = pl.BlockSpec((tm, LANES), lambda i, j: (i, 0))
    if residual is not None:
        in_specs.append(tile)
        args.append(residual)
    if norm_w is not None:
        in_specs.append(pl.BlockSpec((1, tn), lambda i, j: (0, j)))
        args.append(norm_w.reshape(1, n).astype(F32))
    if row_ssq is not None:
        in_specs.append(rows)
        args.append(row_ssq[0])
    in_kernel = [c.shape[0] % (steps * BF16_ROWS) == 0
                 and (c.shape[0] // steps) * c.shape[1] * 4 <= CAST_BLOCK_BYTES for c in cast_weights]
    cast_arrays = [c for c, ok in zip(cast_weights, in_kernel) if ok]
    cast_specs = [pl.BlockSpec((c.shape[0] // steps, c.shape[1]), lambda i, j: (i * nj + j, 0))
                  for c in cast_arrays]
    norm_specs = [tile, rows] if norm_w is not None else []
    norm_shapes = ([jax.ShapeDtypeStruct((m, n), BF16), jax.ShapeDtypeStruct((m, LANES), F32)]
                   if norm_w is not None else [])
    outs = pl.pallas_call(
        functools.partial(_mm_kernel, has_residual=residual is not None, has_norm=norm_w is not None,
                          scale_d=row_ssq[1] if row_ssq is not None else 0, n_cast=len(cast_arrays)),
        grid=(m // tm, nj),
        in_specs=in_specs + cast_specs,
        out_specs=[tile] + norm_specs + cast_specs,
        out_shape=[jax.ShapeDtypeStruct((m, n), out_dtype)] + norm_shapes
        + [jax.ShapeDtypeStruct(c.shape, BF16) for c in cast_arrays],
        compiler_params=_params("arbitrary", "arbitrary"),
        name="matmul",
    )(*args, *cast_arrays)
    result = [outs[0]]
    n_norm = len(norm_specs)
    if norm_w is not None:
        result.append((outs[1], outs[2]))
    if want_casts:
        done = iter(outs[1 + n_norm:])
        result.append([next(done) if ok else c.astype(BF16) for c, ok in zip(cast_weights, in_kernel)])
    return result[0] if len(result) == 1 else tuple(result)


def _shift_rows(ext, s, halo_rows):
    if s == 0:
        return ext[halo_rows:]
    return pltpu.roll(ext, s, axis=0)[halo_rows:]


def _causal_conv_ext(ext, halo_rows, w, b):
    k_width = w.shape[0]
    y = b + w[0:1] * _shift_rows(ext, k_width - 1, halo_rows)
    for k in range(1, k_width):
        y = y + w[k:k + 1] * _shift_rows(ext, k_width - 1 - k, halo_rows)
    return y


def _row_chunks(rows, cols):
    rc = max(BF16_ROWS, EPILOGUE_VREGS * SUBLANES * LANES // cols)
    rc = min(rc, rows)
    return [(r0, rc) for r0 in range(0, rows, rc)]


def _epilogue_conv_gate(h_ref, halo_rows, aux, o_ref, jp):
    cw, cb = aux[0][...], aux[1][...]
    tn = o_ref.shape[1]

    def chunk(r0, rc):
        ext = h_ref[pl.ds(halo_rows + r0 - SUBLANES, rc + SUBLANES), :]
        y = _causal_conv_ext(ext, SUBLANES, cw, cb)
        o_ref[r0:r0 + rc, :] = (_silu(y[:, :tn]) * y[:, tn:]).astype(o_ref.dtype)

    return [functools.partial(chunk, r0, rc) for r0, rc in _row_chunks(o_ref.shape[0], 2 * tn)]


def _epilogue_rope(h_ref, halo_rows, aux, o_ref, jp, *, rope_tiles):
    cos_ref, sin_ref = aux
    d = DIFF_HEAD_DIM
    is_rope = jp < rope_tiles
    scale = jnp.where(jp < rope_tiles // 2, Q_PRESCALE, 1.0)

    def chunk(r0, rc):
        cos = jnp.where(is_rope, cos_ref[r0:r0 + rc, :] * scale, 1.0)
        sin = jnp.where(is_rope, sin_ref[r0:r0 + rc, :] * scale, 0.0)
        for c in range(o_ref.shape[1] // d):
            t = h_ref[r0:r0 + rc, c * d:(c + 1) * d]
            rot = pltpu.roll(t, d // 2, axis=1)
            o_ref[r0:r0 + rc, c * d:(c + 1) * d] = (t * cos + rot * sin).astype(o_ref.dtype)

    return [functools.partial(chunk, r0, rc) for r0, rc in _row_chunks(o_ref.shape[0], 4 * d)]


def _pipelined_kernel(*refs, epilogue, n_w, n_aux, use_halo, lagged, m_parts, scale_d, nb, n_tiles,
                      tiles_per_seq):
    it = iter(refs)
    a_ref = next(it)
    ah_ref = next(it) if use_halo else None
    w_refs = [next(it) for _ in range(n_w)]
    ssq_ref = next(it) if scale_d else None
    ssqh_ref = next(it) if (scale_d and use_halo) else None
    aux = [next(it) for _ in range(n_aux)]
    o_ref = next(it)
    aext_ref = next(it) if use_halo else None
    h = [next(it) for _ in range(2 if lagged else m_parts)]
    halo_rows = BF16_ROWS if use_halo else 0

    s = pl.program_id(0)
    t = jnp.minimum(s, n_tiles - 1)
    i = t // nb
    j = t % nb

    if use_halo:
        @pl.when((j == 0) & (s < n_tiles))
        def _():
            seq_start = (i % tiles_per_seq) == 0
            halo = ah_ref[...]
            aext_ref[0:BF16_ROWS, :] = jnp.where(seq_start, jnp.zeros_like(halo), halo)
            aext_ref[BF16_ROWS:, :] = a_ref[...]

    def matmul_into(h_ref, row0=0):
        lhs_ref = aext_ref if use_halo else a_ref
        nrows = h_ref.shape[0]
        rstd = None
        if scale_d:
            rstd = _row_rstd(ssq_ref[...], scale_d)
            if use_halo:
                rstd = jnp.concatenate([_row_rstd(ssqh_ref[...], scale_d), rstd], axis=0)
            rstd = rstd[row0:row0 + nrows]
        col = 0
        for w_ref in w_refs:
            wn = w_ref.shape[1]
            step_cols = min(wn, MXU_COLS)
            for c0 in range(0, wn, step_cols):
                prod = jnp.dot(lhs_ref[pl.ds(row0, nrows), :], w_ref[:, c0:c0 + step_cols],
                               preferred_element_type=F32)
                h_ref[:, col + c0:col + c0 + step_cols] = prod if rstd is None else prod * rstd
            col += wn

    if not lagged:
        part_rows = o_ref.shape[0] // len(h)
        for part, h_ref in enumerate(h):
            matmul_into(h_ref, part * part_rows)
            for chunk in epilogue(h_ref, halo_rows, aux, o_ref.at[pl.ds(part * part_rows, part_rows)], j):
                chunk()
        return

    jp = jnp.maximum(s - 1, 0) % nb
    slot = s % 2

    @pl.when(s == 0)
    def _():
        h[1][...] = jnp.zeros(h[1].shape, F32)

    def step(cur, prev):
        for chunk in epilogue(h[prev], halo_rows, aux, o_ref, jp):
            chunk()
        matmul_into(h[cur])

    @pl.when(slot == 0)
    def _():
        step(0, 1)

    @pl.when(slot == 1)
    def _():
        step(1, 0)


def pipelined_matmul(a, w, w_col_blocks, *, epilogue, col_aux=(), row_aux=(), out_cols, out_dtype,
                     nb, halo_seq_len=None, tm=1024, single_buffer_a=True, lagged=True, m_parts=1,
                     row_ssq=None, name):
    m, k = a.shape
    use_halo = halo_seq_len is not None
    tm = _pick(halo_seq_len if use_halo else m, tm)
    hb = tm // BF16_ROWS
    n_tiles = (m // tm) * nb
    hrows = tm + (BF16_ROWS if use_halo else 0)
    wn = sum(width for width, _ in w_col_blocks)

    def cur(s):
        t = jnp.minimum(s, n_tiles - 1)
        return t // nb, t % nb

    def prev(s):
        t = jnp.maximum(s - 1, 0) if lagged else s
        return t // nb, t % nb

    a_kwargs = dict(pipeline_mode=pl.Buffered(1)) if single_buffer_a else {}
    in_specs = [pl.BlockSpec((tm, k), lambda s: (cur(s)[0], 0), **a_kwargs)]
    args = [a]
    if use_halo:
        in_specs.append(pl.BlockSpec((BF16_ROWS, k), lambda s: (jnp.maximum(cur(s)[0] * hb - 1, 0), 0)))
        args.append(a)
    for width, first in w_col_blocks:
        in_specs.append(pl.BlockSpec((k, width), lambda s, first=first: (0, first + cur(s)[1])))
        args.append(w)
    if row_ssq is not None:
        in_specs.append(pl.BlockSpec((tm, LANES), lambda s: (cur(s)[0], 0)))
        args.append(row_ssq[0])
        if use_halo:
            in_specs.append(pl.BlockSpec((BF16_ROWS, LANES), lambda s: (jnp.maximum(cur(s)[0] * hb - 1, 0), 0)))
            args.append(row_ssq[0])
    for arr in col_aux:
        in_specs.append(pl.BlockSpec((None,) + arr.shape[1:], lambda s: (prev(s)[1], 0, 0)))
        args.append(arr)
    for arr in row_aux:
        in_specs.append(pl.BlockSpec((tm, arr.shape[1]), lambda s: (prev(s)[0], 0)))
        args.append(arr)
    scratch = ([pltpu.VMEM((hrows, k), BF16)] if use_halo else [])
    if lagged:
        scratch += [pltpu.VMEM((hrows, wn), F32)] * 2
    else:
        scratch += [pltpu.VMEM((tm // m_parts + hrows - tm, wn), F32)] * m_parts
    kern = functools.partial(
        _pipelined_kernel, epilogue=epilogue, n_w=len(w_col_blocks), n_aux=len(col_aux) + len(row_aux),
        use_halo=use_halo, lagged=lagged, m_parts=m_parts,
        scale_d=row_ssq[1] if row_ssq is not None else 0, nb=nb,
        n_tiles=n_tiles,
        tiles_per_seq=(halo_seq_len // tm) if use_halo else 1)
    return pl.pallas_call(
        kern,
        grid=(n_tiles + (1 if lagged else 0),),
        in_specs=in_specs,
        out_specs=pl.BlockSpec((tm, out_cols), lambda s: prev(s)),
        out_shape=jax.ShapeDtypeStruct((m, nb * out_cols), out_dtype),
        scratch_shapes=scratch,
        compiler_params=_params("arbitrary"),
        name=name,
    )(*args)


def _rope_table_kernel(pos_ref, inv_ref, cos_ref, sin_ref):
    freqs = pos_ref[...].astype(F32) * inv_ref[...]
    lane = lax.broadcasted_iota(jnp.int32, freqs.shape, 1)
    sign = jnp.where(lane < DIFF_HEAD_DIM // 2, -1.0, 1.0)
    cos_ref[...] = jnp.cos(freqs)
    sin_ref[...] = jnp.sin(freqs) * sign


def rope_tables(positions):
    m = positions.size
    inv = 1.0 / (ROPE_THETA ** (jnp.arange(0, DIFF_HEAD_DIM, 2, dtype=F32) / DIFF_HEAD_DIM))
    inv_full = jnp.concatenate([inv, inv]).reshape(1, DIFF_HEAD_DIM)
    tm = _pick(m, 1024)
    shape = jax.ShapeDtypeStruct((m, DIFF_HEAD_DIM), F32)
    return pl.pallas_call(
        _rope_table_kernel,
        grid=(m // tm,),
        in_specs=[pl.BlockSpec((tm, 1), lambda i: (i, 0)),
                  pl.BlockSpec((1, DIFF_HEAD_DIM), lambda i: (0, 0))],
        out_specs=[pl.BlockSpec((tm, DIFF_HEAD_DIM), lambda i: (i, 0))] * 2,
        out_shape=[shape, shape],
        compiler_params=_params("arbitrary"),
        name="rope_tables",
    )(positions.reshape(m, 1), inv_full)


def _diff_attn_kernel(q_ref, k_ref, v_ref, lq1_ref, lk1_ref, lq2_ref, lk2_ref, sw_ref, o_ref,
                      m_ref, l_ref, acc_ref, *, lambda_init, tq):
    qi = pl.program_id(2)
    d = DIFF_HEAD_DIM
    q = q_ref[...]

    m_ref[...] = jnp.full(m_ref.shape, -jnp.inf, F32)
    l_ref[...] = jnp.zeros(l_ref.shape, F32)
    acc_ref[...] = jnp.zeros(acc_ref.shape, F32)

    def kv_block(kb, masked):
        start = pl.multiple_of(kb * tq, tq)
        k = k_ref[pl.ds(start, tq), :]
        v = v_ref[pl.ds(start, tq), :]
        scores = [lax.dot_general(q[:, c * d:(c + 1) * d], k[:, c * d:(c + 1) * d],
                                  (((1,), (1,)), ((), ())), preferred_element_type=F32) for c in range(2)]
        probs = []
        alphas = []
        for c in range(2):
            s = scores[c]
            if masked:
                row = lax.broadcasted_iota(jnp.int32, s.shape, 0)
                col = lax.broadcasted_iota(jnp.int32, s.shape, 1)
                s = jnp.where(col <= row, s, -jnp.inf)
            chunks = [s[:, t * LANES:(t + 1) * LANES] for t in range(tq // LANES)]
            cmax = functools.reduce(jnp.maximum, chunks)
            m_old = m_ref[c]
            m_new = jnp.maximum(m_old, jnp.max(cmax, axis=-1, keepdims=True))
            alpha = jnp.exp2(m_old - m_new)
            ps = [jnp.exp2(ch - m_new) for ch in chunks]
            l_ref[c] = alpha * l_ref[c] + functools.reduce(jnp.add, ps)
            m_ref[c] = m_new
            probs.append(jnp.concatenate(ps, axis=1).astype(v.dtype))
            alphas.append(alpha)
        pv = jnp.dot(jnp.concatenate(probs, axis=0), v, preferred_element_type=F32)
        for c in range(2):
            a2 = jnp.concatenate([alphas[c], alphas[c]], axis=1)
            acc_ref[c] = a2 * acc_ref[c] + pv[c * tq:(c + 1) * tq]

    def body(kb, carry):
        kv_block(kb, False)
        return carry

    lax.fori_loop(0, qi, body, 0)
    kv_block(qi, True)

    lam = (jnp.exp(jnp.sum(lq1_ref[...] * lk1_ref[...], axis=-1, keepdims=True))
           - jnp.exp(jnp.sum(lq2_ref[...] * lk2_ref[...], axis=-1, keepdims=True)) + lambda_init)
    l0 = jnp.sum(l_ref[0], axis=-1, keepdims=True)
    l1 = jnp.sum(l_ref[1], axis=-1, keepdims=True)
    o = acc_ref[0] / l0 - lam * (acc_ref[1] / l1)
    o = o * lax.rsqrt(jnp.mean(o * o, axis=-1, keepdims=True) + EPS)
    o_ref[...] = ((o * sw_ref[...]) * (1.0 - lambda_init)).astype(o_ref.dtype)


def _cast_rows(w, steps):
    k, n = w.shape
    for rows in range(BF16_ROWS, k + 1, BF16_ROWS):
        if k % rows == 0 and k // rows <= steps:
            return rows if rows * n * 4 <= CAST_BLOCK_BYTES else None
    return None


def _plan_casts(weights, steps, lin):
    rows = [None if w.dtype == BF16 else _cast_rows(w, steps) for w in weights]
    arrays, specs, spans = [], [], []
    at_end = False
    for w, r in zip(weights, rows):
        if not r:
            continue
        nb = w.shape[0] // r
        first = steps - nb if at_end else 0
        at_end = not at_end if nb < steps else at_end
        arrays.append(w)
        spans.append((first, nb))
        specs.append(pl.BlockSpec(
            (r, w.shape[1]),
            lambda *ids, first=first, nb=nb: (jnp.clip(lin(*ids) - first, 0, nb - 1), 0)))

    def finish(outputs):
        done = iter(outputs)
        return [next(done) if r else w.astype(BF16) for w, r in zip(weights, rows)]

    return arrays, specs, spans, finish


def _run_casts(step, spans, cast_in, cast_out):
    for (first, nb), src, dst in zip(spans, cast_in, cast_out):
        @pl.when((step >= first) & (step < first + nb))
        def _():
            dst[...] = src[...].astype(dst.dtype)


def diff_attention(qkv, lq1, lk1, lq2, lk2, subln_w, lambda_init, batch, seq_len, *, tq=1024,
                   cast_weights=()):
    m, n = qkv.shape
    d_model = n // 3
    hw = 2 * DIFF_HEAD_DIM
    heads = d_model // hw
    tq = _pick(seq_len, tq)
    nq = seq_len // tq

    def lin(b, h, qi):
        return (b * heads + h) * nq + qi

    cast_arrays, cast_specs, spans, finish = _plan_casts(cast_weights, batch * heads * nq, lin)
    n_cast = len(cast_arrays)

    def kern(*refs):
        main_in, cast_in = refs[:8], refs[8:8 + n_cast]
        o_ref = refs[8 + n_cast]
        cast_out = refs[9 + n_cast:9 + 2 * n_cast]
        scratch = refs[9 + 2 * n_cast:]
        _run_casts(lin(pl.program_id(0), pl.program_id(1), pl.program_id(2)), spans, cast_in, cast_out)
        _diff_attn_kernel(*main_in, o_ref, *scratch, lambda_init=lambda_init, tq=tq)

    vec = lambda a: a.reshape(1, -1)
    lam_spec = pl.BlockSpec((1, DIFF_HEAD_DIM), lambda b, h, qi: (0, 0))
    outs = pl.pallas_call(
        kern,
        grid=(batch, heads, nq),
        in_specs=[
            pl.BlockSpec((tq, hw), lambda b, h, qi: (b * nq + qi, h)),
            pl.BlockSpec((seq_len, hw), lambda b, h, qi: (b, heads + h)),
            pl.BlockSpec((seq_len, hw), lambda b, h, qi: (b, 2 * heads + h)),
            lam_spec, lam_spec, lam_spec, lam_spec,
            pl.BlockSpec((1, hw), lambda b, h, qi: (0, 0)),
        ] + cast_specs,
        out_specs=[pl.BlockSpec((tq, hw), lambda b, h, qi: (b * nq + qi, h))] + cast_specs,
        out_shape=[jax.ShapeDtypeStruct((m, d_model), BF16)]
        + [jax.ShapeDtypeStruct(w.shape, BF16) for w in cast_arrays],
        scratch_shapes=[pltpu.VMEM((2, tq, LANES), F32), pltpu.VMEM((2, tq, LANES), F32),
                        pltpu.VMEM((2, tq, hw), F32)],
        compiler_params=_params("arbitrary", "arbitrary", "arbitrary"),
        name="diff_attention",
    )(qkv, qkv, qkv, vec(lq1), vec(lk1), vec(lq2), vec(lk2), vec(subln_w), *cast_arrays)
    return outs[0], finish(outs[1:])


def _dot_exact(a, b):
    return jnp.dot(a, b, preferred_element_type=F32, precision=lax.Precision.HIGHEST)


def _ssd_kernel(z_ref, x_ref, xh_ref, b_ref, bh_ref, c_ref, ch_ref, dt_ref,
                cwx_ref, cwb_ref, cwc_ref, cbx_ref, cbb_ref, cbc_ref,
                dtb_ref, alog_ref, dexp_ref, nw_ref, sel_ref, expand_ref, o_ref, state_ref, *, heads_per_group):
    g = pl.program_id(1)
    c = pl.program_id(2)
    t = CHUNK
    hpg = heads_per_group
    p = SSM_HEADDIM
    gw = hpg * p
    nh = dt_ref.shape[1]

    @pl.when(c == 0)
    def _():
        state_ref[...] = jnp.zeros(state_ref.shape, F32)

    seq_start = c == 0

    def conv_silu(cur_ref, halo_ref, w_ref, bias_ref):
        halo = halo_ref[...]
        halo = jnp.where(seq_start, jnp.zeros_like(halo), halo)
        ext = jnp.concatenate([halo, cur_ref[...]], axis=0)
        return _silu(_causal_conv_ext(ext, SUBLANES, w_ref[...], bias_ref[...]))

    xc = conv_silu(x_ref, xh_ref, cwx_ref, cbx_ref)
    bc = conv_silu(b_ref, bh_ref, cwb_ref, cbb_ref)
    cc = conv_silu(c_ref, ch_ref, cwc_ref, cbc_ref)

    dtr = dt_ref[...] + dtb_ref[...]
    dtv = jnp.maximum(dtr, 0.0) + jnp.log1p(jnp.exp(-jnp.abs(dtr)))
    a_all = dtv * (-jnp.exp(alog_ref[...]))

    row_t = lax.broadcasted_iota(jnp.int32, (t, t), 0)
    col_t = lax.broadcasted_iota(jnp.int32, (t, t), 1)
    causal = col_t <= row_t
    a_cs_all = _dot_exact(jnp.where(causal, 1.0, 0.0), a_all)
    a_last = a_cs_all[t - 1:t, :]

    a_cs = _dot_exact(a_cs_all, sel_ref[...])
    a_cs_t = a_cs.T

    stacked = jnp.concatenate([dtv, jnp.exp(a_cs_all), jnp.exp(a_last - a_cs_all)], axis=0)
    hi = stacked.astype(BF16)
    lo = (stacked - hi.astype(F32)).astype(BF16)
    expanded = jnp.dot(jnp.concatenate([hi, lo], axis=1), expand_ref[...],
                       preferred_element_type=F32)
    dt_e = expanded[0:t]
    exp_acs_e = expanded[t:2 * t]
    dec_e = expanded[2 * t:3 * t]

    xdt = xc * dt_e
    xdt_b = xdt.astype(BF16)
    bb = bc.astype(BF16)
    cb = cc.astype(BF16)
    cbm = lax.dot_general(cb, bb, (((1,), (1,)), ((), ())), preferred_element_type=F32)

    state = state_ref[...]
    y = jnp.dot(cb, state.astype(BF16), preferred_element_type=F32) * exp_acs_e

    lane = lax.broadcasted_iota(jnp.int32, (t, 2 * p), 1)
    pieces = []
    for pair in range(hpg // 2):
        x_pair = xdt_b[:, pair * 2 * p:(pair + 1) * 2 * p]
        outs = []
        for hh in (2 * pair, 2 * pair + 1):
            seg = a_cs[:, hh:hh + 1] - a_cs_t[hh:hh + 1, :]
            decay = jnp.exp(jnp.where(causal, seg, -jnp.inf))
            outs.append(jnp.dot((cbm * decay).astype(BF16), x_pair, preferred_element_type=F32))
        pieces.append(jnp.where(lane < p, outs[0], outs[1]))
    y = y + jnp.concatenate(pieces, axis=1) + dexp_ref[...] * xc

    new_contrib = jnp.dot(bc.T.astype(BF16), (xdt * dec_e).astype(BF16), preferred_element_type=F32)
    state_ref[...] = state * exp_acs_e[t - 1:t, :] + new_contrib

    gn = y * _silu(z_ref[...])
    gn = gn * lax.rsqrt(jnp.mean(gn * gn, axis=-1, keepdims=True) + EPS)
    o_ref[...] = (gn * nw_ref[...]).astype(o_ref.dtype)


def ssd_mixer(zx, bcm, dt, conv_w, conv_b, dt_bias, a_log, d_skip, norm_w, batch, seq_len, cast_weights=()):
    m = zx.shape[0]
    d_inner = zx.shape[1] // 2
    nh = dt.shape[1]
    hpg = nh // SSM_GROUPS
    gw = d_inner // SSM_GROUPS
    n = SSM_STATE
    t = CHUNK
    nc = seq_len // t
    hb = t // SUBLANES
    ng = SSM_GROUPS
    xb0 = d_inner // gw
    cwb0 = d_inner // n

    def row(b, g, c):
        return b * nc + c

    def halo(b, g, c):
        return jnp.maximum((b * nc + c) * hb - 1, 0)

    d_exp = jnp.repeat(d_skip.astype(F32), SSM_HEADDIM).reshape(1, d_inner)
    conv_b2 = conv_b.reshape(1, -1)
    head = jnp.arange(nh)
    sel = (head[None, :, None] == (jnp.arange(ng) * hpg)[:, None, None] + head[None, None, :]) \
        & (head[None, None, :] < hpg)
    chan_head = jnp.arange(gw) // SSM_HEADDIM
    expand = head[None, :, None] == (jnp.arange(ng) * hpg)[:, None, None] + chan_head[None, None, :]
    sel = sel.astype(F32)
    expand = jnp.concatenate([expand, expand], axis=1).astype(BF16)
    def lin(b, g, c):
        return (b * ng + g) * nc + c

    cast_arrays, cast_specs, spans, finish = _plan_casts(cast_weights, batch * ng * nc, lin)
    n_cast = len(cast_arrays)
    n_main = 20

    def kern(*refs):
        main_in, cast_in = refs[:n_main], refs[n_main:n_main + n_cast]
        o_ref = refs[n_main + n_cast]
        cast_out = refs[n_main + n_cast + 1:n_main + 2 * n_cast + 1]
        scratch = refs[n_main + 2 * n_cast + 1:]
        _run_casts(lin(pl.program_id(0), pl.program_id(1), pl.program_id(2)), spans, cast_in, cast_out)
        _ssd_kernel(*main_in, o_ref, *scratch, heads_per_group=hpg)

    outs = pl.pallas_call(
        kern,
        grid=(batch, ng, nc),
        in_specs=[
            pl.BlockSpec((t, gw), lambda b, g, c: (row(b, g, c), g)),
            pl.BlockSpec((t, gw), lambda b, g, c: (row(b, g, c), xb0 + g)),
            pl.BlockSpec((SUBLANES, gw), lambda b, g, c: (halo(b, g, c), xb0 + g)),
            pl.BlockSpec((t, n), lambda b, g, c: (row(b, g, c), g)),
            pl.BlockSpec((SUBLANES, n), lambda b, g, c: (halo(b, g, c), g)),
            pl.BlockSpec((t, n), lambda b, g, c: (row(b, g, c), ng + g)),
            pl.BlockSpec((SUBLANES, n), lambda b, g, c: (halo(b, g, c), ng + g)),
            pl.BlockSpec((t, nh), lambda b, g, c: (row(b, g, c), 0)),
            pl.BlockSpec((SSM_CONV, gw), lambda b, g, c: (0, g)),
            pl.BlockSpec((SSM_CONV, n), lambda b, g, c: (0, cwb0 + g)),
            pl.BlockSpec((SSM_CONV, n), lambda b, g, c: (0, cwb0 + ng + g)),
            pl.BlockSpec((1, gw), lambda b, g, c: (0, g)),
            pl.BlockSpec((1, n), lambda b, g, c: (0, cwb0 + g)),
            pl.BlockSpec((1, n), lambda b, g, c: (0, cwb0 + ng + g)),
            pl.BlockSpec((1, nh), lambda b, g, c: (0, 0)),
            pl.BlockSpec((1, nh), lambda b, g, c: (0, 0)),
            pl.BlockSpec((1, gw), lambda b, g, c: (0, g)),
            pl.BlockSpec((1, gw), lambda b, g, c: (0, g)),
            pl.BlockSpec((None, nh, nh), lambda b, g, c: (g, 0, 0)),
            pl.BlockSpec((None, 2 * nh, gw), lambda b, g, c: (g, 0, 0)),
        ] + cast_specs,
        out_specs=[pl.BlockSpec((t, gw), lambda b, g, c: (row(b, g, c), g))] + cast_specs,
        out_shape=[jax.ShapeDtypeStruct((m, d_inner), BF16)]
        + [jax.ShapeDtypeStruct(w.shape, BF16) for w in cast_arrays],
        scratch_shapes=[pltpu.VMEM((n, gw), F32)],
        compiler_params=_params("arbitrary", "arbitrary", "arbitrary"),
        name="ssd_mixer",
    )(zx, zx, zx, bcm, bcm, bcm, bcm, dt, conv_w, conv_w, conv_w, conv_b2, conv_b2, conv_b2,
      dt_bias.reshape(1, nh), a_log.reshape(1, nh), d_exp, norm_w.reshape(1, d_inner), sel, expand,
      *cast_arrays)
    return outs[0], finish(outs[1:])


def mamba_layer(xf, act, w_in, conv_w, conv_b, dt_bias, a_log, d_skip, norm_w, w_out, batch, seq_len,
                next_norm_w, cast_in_proj=(), cast_ssd=(), cast_out_proj=()):
    u, rs = act
    nh = dt_bias.shape[0]
    d_inner = norm_w.shape[0]
    bc_cols = 2 * SSM_GROUPS * SSM_STATE
    w_in = w_in.astype(BF16)
    zx, c_in = matmul(u, w_in, col0=0, n=2 * d_inner, tm=1024, tn=1024, row_ssq=rs, single_buffer_a=True,
                      cast_weights=list(cast_in_proj))
    bcm = matmul(u, w_in, col0=2 * d_inner, n=bc_cols, tm=1024, tn=1024, row_ssq=rs)
    dt = matmul(u, w_in, col0=2 * d_inner + bc_cols, n=nh, tm=1024, tn=nh, row_ssq=rs)
    yn, c_ssd = ssd_mixer(zx, bcm, dt, conv_w, conv_b, dt_bias, a_log, d_skip, norm_w, batch, seq_len,
                          cast_weights=(w_out,) + tuple(cast_ssd))
    xf, (xw, ssq), c_out = matmul(yn, c_ssd[0], tm=1024, tn=512, residual=xf, single_buffer_a=True,
                                  norm_w=next_norm_w, cast_weights=list(cast_out_proj))
    return xf, (xw, (ssq, xf.shape[1])), c_in, c_ssd[1:], c_out


def attention_layer(xf, act, cos, sin, w_qkv, lq1, lk1, lq2, lk2, subln_w, w_o, lambda_init, batch, seq_len,
                    next_norm_w, cast_weights=()):
    u, rs = act
    d_model = w_o.shape[0]
    tn = _pick(d_model, 1024)
    nb = w_qkv.shape[1] // tn
    qkv = pipelined_matmul(u, w_qkv.astype(BF16), [(tn, 0)],
                           epilogue=functools.partial(_epilogue_rope, rope_tiles=2 * d_model // tn),
                           row_aux=(cos, sin), out_cols=tn, out_dtype=BF16, nb=nb, row_ssq=rs,
                           single_buffer_a=False, name="qkv_rope")
    o, casted = diff_attention(qkv, lq1, lk1, lq2, lk2, subln_w, lambda_init, batch, seq_len,
                               cast_weights=cast_weights)
    xf, (xw, ssq) = matmul(o, w_o.astype(BF16), tm=1024, tn=512, residual=xf, norm_w=next_norm_w)
    return xf, (xw, (ssq, d_model)), casted


def ffn_layer(xf, act, w_up, conv_w, conv_b, w_down, seq_len, next_norm_w, cast_down_proj=()):
    u, rs = act
    d_ff = w_up.shape[1] // 2
    tn = _pick(d_ff, 256)
    nb = d_ff // tn
    cw = conv_w.astype(F32).reshape(FFN_CONV, 2, nb, tn).transpose(2, 0, 1, 3).reshape(nb, FFN_CONV, 2 * tn)
    cb = conv_b.astype(F32).reshape(2, nb, tn).transpose(1, 0, 2).reshape(nb, 1, 2 * tn)
    g = pipelined_matmul(u, w_up.astype(BF16), [(tn, 0), (tn, nb)], epilogue=_epilogue_conv_gate,
                         col_aux=(cw, cb), out_cols=tn, out_dtype=BF16, nb=nb, halo_seq_len=seq_len,
                         tm=2048, lagged=False, m_parts=2, single_buffer_a=True, row_ssq=rs, name="ffn_up")
    outs = matmul(g, w_down.astype(BF16), tm=1024, tn=256, residual=xf, single_buffer_a=True,
                  norm_w=next_norm_w, cast_weights=list(cast_down_proj))
    if next_norm_w is None:
        return outs[0], None, outs[1]
    xf, (xw, ssq), c_down = outs
    return xf, (xw, (ssq, xf.shape[1])), c_down


def kernel(x, positions, l0_norm_mix, l0_m_w_in, l0_m_conv_w, l0_m_conv_b, l0_m_dt_bias, l0_m_a_log, l0_m_d, l0_m_norm, l0_m_w_out, l0_norm_ffn, l0_f_w_up, l0_f_conv_w, l0_f_conv_b, l0_f_w_down, l1_norm_mix, l1_a_w_qkv, l1_a_lq1, l1_a_lk1, l1_a_lq2, l1_a_lk2, l1_a_subln, l1_a_w_o, l1_norm_ffn, l1_f_w_up, l1_f_conv_w, l1_f_conv_b, l1_f_w_down, l2_norm_mix, l2_m_w_in, l2_m_conv_w, l2_m_conv_b, l2_m_dt_bias, l2_m_a_log, l2_m_d, l2_m_norm, l2_m_w_out, l2_norm_ffn, l2_f_w_up, l2_f_conv_w, l2_f_conv_b, l2_f_w_down, l3_norm_mix, l3_a_w_qkv, l3_a_lq1, l3_a_lk1, l3_a_lq2, l3_a_lk2, l3_a_subln, l3_a_w_o, l3_norm_ffn, l3_f_w_up, l3_f_conv_w, l3_f_conv_b, l3_f_w_down, final_norm):
    norm_mix = [l0_norm_mix, l1_norm_mix, l2_norm_mix, l3_norm_mix]
    norm_ffn = [l0_norm_ffn, l1_norm_ffn, l2_norm_ffn, l3_norm_ffn]
    mixer_params = [
        (l0_m_w_in, l0_m_conv_w, l0_m_conv_b, l0_m_dt_bias, l0_m_a_log, l0_m_d, l0_m_norm, l0_m_w_out),
        (l1_a_w_qkv, l1_a_lq1, l1_a_lk1, l1_a_lq2, l1_a_lk2, l1_a_subln, l1_a_w_o),
        (l2_m_w_in, l2_m_conv_w, l2_m_conv_b, l2_m_dt_bias, l2_m_a_log, l2_m_d, l2_m_norm, l2_m_w_out),
        (l3_a_w_qkv, l3_a_lq1, l3_a_lk1, l3_a_lq2, l3_a_lk2, l3_a_subln, l3_a_w_o),
    ]
    ffn_params = [
        (l0_f_w_up, l0_f_conv_w, l0_f_conv_b, l0_f_w_down),
        (l1_f_w_up, l1_f_conv_w, l1_f_conv_b, l1_f_w_down),
        (l2_f_w_up, l2_f_conv_w, l2_f_conv_b, l2_f_w_down),
        (l3_f_w_up, l3_f_conv_w, l3_f_conv_b, l3_f_w_down),
    ]
    batch, seq_len, d_model = x.shape
    xf = x.reshape(batch * seq_len, d_model)
    cos, sin = rope_tables(positions)
    n_layers = len(norm_mix)
    act = (rmsnorm(xf, norm_mix[0], BF16), None)
    for i in range(n_layers):
        w_up, cw, cb, w_down = ffn_params[i]
        nxt = list(mixer_params[i + 1]) if i + 1 < n_layers else None
        if i % N_MIXERS == 0:
            xf, act, (w_up,), (w_down,), c_out = mamba_layer(
                xf, act, *mixer_params[i], batch, seq_len, norm_ffn[i], cast_in_proj=[w_up],
                cast_ssd=[w_down], cast_out_proj=[nxt[0]] if nxt else [])
            if nxt:
                nxt[0] = c_out[0]
            cast_down = [nxt[-1], ffn_params[i + 1][0]] if nxt else []
        else:
            lambda_init = 0.8 - 0.6 * math.exp(-0.3 * i)
            xf, act, (w_up, w_down) = attention_layer(xf, act, cos, sin, *mixer_params[i], lambda_init, batch,
                                                      seq_len, norm_ffn[i], cast_weights=[w_up, w_down])
            cast_down = [nxt[0]] if nxt else []
        xf, act, c_down = ffn_layer(xf, act, w_up, cw, cb, w_down, seq_len,
                                    norm_mix[i + 1] if nxt else None, cast_down_proj=cast_down)
        if nxt:
            nxt[-1 if i % N_MIXERS == 0 else 0] = c_down[0]
            mixer_params[i + 1] = tuple(nxt)
            if len(c_down) > 1:
                ffn_params[i + 1] = (c_down[1],) + tuple(ffn_params[i + 1][1:])
    return rmsnorm(xf, final_norm, F32).reshape(batch, seq_len, d_model)
```

```python
import functools
import math

import jax
import jax.numpy as jnp
from jax import lax
from jax.experimental import pallas as pl
from jax.experimental.pallas import tpu as pltpu

F32 = jnp.float32
BF16 = jnp.bfloat16

EPS = 1e-5
N_MIXERS = 2

SSM_HEADDIM = 64
SSM_GROUPS = 8
SSM_STATE = 128
SSM_CONV = 4
CHUNK = 128

DIFF_HEAD_DIM = 128
ATTN_SCALE = DIFF_HEAD_DIM ** -0.5
Q_PRESCALE = ATTN_SCALE * math.log2(math.e)
ROPE_THETA = 10000.0

FFN_CONV = 3

V7X_VMEM_BYTES = 64 * 1024 * 1024
VMEM_LIMIT_BYTES = V7X_VMEM_BYTES - 8 * 1024 * 1024
SUBLANES = 8
BF16_ROWS = 16
LANES = 128
MXU_COLS = 256
CAST_BLOCK_BYTES = 4 * 1024 * 1024
EPILOGUE_VREGS = 32


def _params(*sem):
    return pltpu.CompilerParams(dimension_semantics=sem, vmem_limit_bytes=VMEM_LIMIT_BYTES)


def _pick(dim, pref):
    if dim <= pref:
        return dim
    t = pref
    while dim % t:
        t //= 2
    return t


def _silu(x):
    return x / (1.0 + jnp.exp(-x))


def _rmsnorm_kernel(x_ref, w_ref, o_ref):
    x = x_ref[...]
    y = x * lax.rsqrt(jnp.mean(x * x, axis=-1, keepdims=True) + EPS)
    o_ref[...] = (y * w_ref[...]).astype(o_ref.dtype)


def rmsnorm(x, w, out_dtype):
    m, d = x.shape
    tm = _pick(m, 256)
    return pl.pallas_call(
        _rmsnorm_kernel,
        grid=(m // tm,),
        in_specs=[pl.BlockSpec((tm, d), lambda i: (i, 0)),
                  pl.BlockSpec((1, d), lambda i: (0, 0))],
        out_specs=pl.BlockSpec((tm, d), lambda i: (i, 0)),
        out_shape=jax.ShapeDtypeStruct((m, d), out_dtype),
        compiler_params=_params("arbitrary"),
        name="rmsnorm",
    )(x, w.reshape(1, d))


def _row_rstd(ssq, d):
    return lax.rsqrt(jnp.sum(ssq, axis=-1, keepdims=True) / d + EPS)


def _mm_kernel(*refs, has_residual, has_norm, scale_d, n_cast):
    it = iter(refs)
    a_ref, w_ref = next(it), next(it)
    r_ref = next(it) if has_residual else None
    nw_ref = next(it) if has_norm else None
    ssq_in_ref = next(it) if scale_d else None
    cast_in = [next(it) for _ in range(n_cast)]
    o_ref = next(it)
    xw_ref, ssq_ref = (next(it), next(it)) if has_norm else (None, None)
    cast_out = [next(it) for _ in range(n_cast)]
    for src, dst in zip(cast_in, cast_out):
        dst[...] = src[...].astype(dst.dtype)
    acc = jnp.dot(a_ref[...], w_ref[...], preferred_element_type=F32)
    if scale_d:
        acc = acc * _row_rstd(ssq_in_ref[...], scale_d)
    if has_residual:
        acc = r_ref[...] + acc
    o_ref[...] = acc.astype(o_ref.dtype)
    if has_norm:
        xw_ref[...] = (acc * nw_ref[...]).astype(xw_ref.dtype)
        sq = acc * acc
        part = functools.reduce(jnp.add, [sq[:, t * LANES:(t + 1) * LANES] for t in range(sq.shape[1] // LANES)])

        @pl.when(pl.program_id(1) == 0)
        def _():
            ssq_ref[...] = jnp.zeros(ssq_ref.shape, F32)

        ssq_ref[...] += part


def matmul(a, w, *, col0=0, n=None, tm, tn, out_dtype=F32, residual=None, single_buffer_a=False,
           cast_weights=None, row_ssq=None, norm_w=None):
    want_casts = cast_weights is not None
    cast_weights = cast_weights or ()
    m, k = a.shape
    n = w.shape[1] - col0 if n is None else n
    tm = _pick(m, tm)
    tn = _pick(n, tn)
    assert col0 % tn == 0
    jb = col0 // tn
    nj = n // tn
    steps = (m // tm) * nj
    a_kwargs = dict(pipeline_mode=pl.Buffered(1)) if single_buffer_a else {}
    in_specs = [pl.BlockSpec((tm, k), lambda i, j: (i, 0), **a_kwargs),
                pl.BlockSpec((k, tn), lambda i, j: (0, j + jb))]
    args = [a, w]
    tile = pl.BlockSpec((tm, tn), lambda i, j: (i, j))
    rows = pl.BlockSpec((tm, LANES), lambda i, j: (i, 0))
    if residual is not None:
        in_specs.append(tile)
        args.append(residual)
    if norm_w is not None:
        in_specs.append(pl.BlockSpec((1, tn), lambda i, j: (0, j)))
        args.append(norm_w.reshape(1, n).astype(F32))
    if row_ssq is not None:
        in_specs.append(rows)
        args.append(row_ssq[0])
    in_kernel = [c.shape[0] % (steps * BF16_ROWS) == 0
                 and (c.shape[0] // steps) * c.shape[1] * 4 <= CAST_BLOCK_BYTES for c in cast_weights]
    cast_arrays = [c for c, ok in zip(cast_weights, in_kernel) if ok]
    cast_specs = [pl.BlockSpec((c.shape[0] // steps, c.shape[1]), lambda i, j: (i * nj + j, 0))
                  for c in cast_arrays]
    norm_specs = [tile, rows] if norm_w is not None else []
    norm_shapes = ([jax.ShapeDtypeStruct((m, n), BF16), jax.ShapeDtypeStruct((m, LANES), F32)]
                   if norm_w is not None else [])
    outs = pl.pallas_call(
        functools.partial(_mm_kernel, has_residual=residual is not None, has_norm=norm_w is not None,
                          scale_d=row_ssq[1] if row_ssq is not None else 0, n_cast=len(cast_arrays)),
        grid=(m // tm, nj),
        in_specs=in_specs + cast_specs,
        out_specs=[tile] + norm_specs + cast_specs,
        out_shape=[jax.ShapeDtypeStruct((m, n), out_dtype)] + norm_shapes
        + [jax.ShapeDtypeStruct(c.shape, BF16) for c in cast_arrays],
        compiler_params=_params("arbitrary", "arbitrary"),
        name="matmul",
    )(*args, *cast_arrays)
    result = [outs[0]]
    n_norm = len(norm_specs)
    if norm_w is not None:
        result.append((outs[1], outs[2]))
    if want_casts:
        done = iter(outs[1 + n_norm:])
        result.append([next(done) if ok else c.astype(BF16) for c, ok in zip(cast_weights, in_kernel)])
    return result[0] if len(result) == 1 else tuple(result)


def _shift_rows(ext, s, halo_rows):
    if s == 0:
        return ext[halo_rows:]
    return pltpu.roll(ext, s, axis=0)[halo_rows:]


def _causal_conv_ext(ext, halo_rows, w, b):
    k_width = w.shape[0]
    y = b + w[0:1] * _shift_rows(ext, k_width - 1, halo_rows)
    for k in range(1, k_width):
        y = y + w[k:k + 1] * _shift_rows(ext, k_width - 1 - k, halo_rows)
    return y


def _row_chunks(rows, cols):
    rc = max(BF16_ROWS, EPILOGUE_VREGS * SUBLANES * LANES // cols)
    rc = min(rc, rows)
    return [(r0, rc) for r0 in range(0, rows, rc)]


def _epilogue_conv_gate(h_ref, halo_rows, aux, o_ref, jp):
    cw, cb = aux[0][...], aux[1][...]
    tn = o_ref.shape[1]

    def chunk(r0, rc):
        ext = h_ref[pl.ds(halo_rows + r0 - SUBLANES, rc + SUBLANES), :]
        y = _causal_conv_ext(ext, SUBLANES, cw, cb)
        o_ref[r0:r0 + rc, :] = (_silu(y[:, :tn]) * y[:, tn:]).astype(o_ref.dtype)

    return [functools.partial(chunk, r0, rc) for r0, rc in _row_chunks(o_ref.shape[0], 2 * tn)]


def _epilogue_rope(h_ref, halo_rows, aux, o_ref, jp, *, rope_tiles):
    cos_ref, sin_ref = aux
    d = DIFF_HEAD_DIM
    is_rope = jp < rope_tiles
    scale = jnp.where(jp < rope_tiles // 2, Q_PRESCALE, 1.0)

    def chunk(r0, rc):
        cos = jnp.where(is_rope, cos_ref[r0:r0 + rc, :] * scale, 1.0)
        sin = jnp.where(is_rope, sin_ref[r0:r0 + rc, :] * scale, 0.0)
        for c in range(o_ref.shape[1] // d):
            t = h_ref[r0:r0 + rc, c * d:(c + 1) * d]
            rot = pltpu.roll(t, d // 2, axis=1)
            o_ref[r0:r0 + rc, c * d:(c + 1) * d] = (t * cos + rot * sin).astype(o_ref.dtype)

    return [functools.partial(chunk, r0, rc) for r0, rc in _row_chunks(o_ref.shape[0], 4 * d)]


def _pipelined_kernel(*refs, epilogue, n_w, n_aux, use_halo, lagged, m_parts, scale_d, nb, n_tiles,
                      tiles_per_seq):
    it = iter(refs)
    a_ref = next(it)
    ah_ref = next(it) if use_halo else None
    w_refs = [next(it) for _ in range(n_w)]
    ssq_ref = next(it) if scale_d else None
    ssqh_ref = next(it) if (scale_d and use_halo) else None
    aux = [next(it) for _ in range(n_aux)]
    o_ref = next(it)
    aext_ref = next(it) if use_halo else None
    h = [next(it) for _ in range(2 if lagged else m_parts)]
    halo_rows = BF16_ROWS if use_halo else 0

    s = pl.program_id(0)
    t = jnp.minimum(s, n_tiles - 1)
    i = t // nb
    j = t % nb

    if use_halo:
        @pl.when((j == 0) & (s < n_tiles))
        def _():
            seq_start = (i % tiles_per_seq) == 0
            halo = ah_ref[...]
            aext_ref[0:BF16_ROWS, :] = jnp.where(seq_start, jnp.zeros_like(halo), halo)
            aext_ref[BF16_ROWS:, :] = a_ref[...]

    def matmul_into(h_ref, row0=0):
        lhs_ref = aext_ref if use_halo else a_ref
        nrows = h_ref.shape[0]
        rstd = None
        if scale_d:
            rstd = _row_rstd(ssq_ref[...], scale_d)
            if use_halo:
                rstd = jnp.concatenate([_row_rstd(ssqh_ref[...], scale_d), rstd], axis=0)
            rstd = rstd[row0:row0 + nrows]
        col = 0
        for w_ref in w_refs:
            wn = w_ref.shape[1]
            step_cols = min(wn, MXU_COLS)
            for c0 in range(0, wn, step_cols):
                prod = jnp.dot(lhs_ref[pl.ds(row0, nrows), :], w_ref[:, c0:c0 + step_cols],
                               preferred_element_type=F32)
                h_ref[:, col + c0:col + c0 + step_cols] = prod if rstd is None else prod * rstd
            col += wn

    if not lagged:
        part_rows = o_ref.shape[0] // len(h)
        for part, h_ref in enumerate(h):
            matmul_into(h_ref, part * part_rows)
            for chunk in epilogue(h_ref, halo_rows, aux, o_ref.at[pl.ds(part * part_rows, part_rows)], j):
                chunk()
        return

    jp = jnp.maximum(s - 1, 0) % nb
    slot = s % 2

    @pl.when(s == 0)
    def _():
        h[1][...] = jnp.zeros(h[1].shape, F32)

    def step(cur, prev):
        for chunk in epilogue(h[prev], halo_rows, aux, o_ref, jp):
            chunk()
        matmul_into(h[cur])

    @pl.when(slot == 0)
    def _():
        step(0, 1)

    @pl.when(slot == 1)
    def _():
        step(1, 0)


def pipelined_matmul(a, w, w_col_blocks, *, epilogue, col_aux=(), row_aux=(), out_cols, out_dtype,
                     nb, halo_seq_len=None, tm=1024, single_buffer_a=True, lagged=True, m_parts=1,
                     row_ssq=None, name):
    m, k = a.shape
    use_halo = halo_seq_len is not None
    tm = _pick(halo_seq_len if use_halo else m, tm)
    hb = tm // BF16_ROWS
    n_tiles = (m // tm) * nb
    hrows = tm + (BF16_ROWS if use_halo else 0)
    wn = sum(width for width, _ in w_col_blocks)

    def cur(s):
        t = jnp.minimum(s, n_tiles - 1)
        return t // nb, t % nb

    def prev(s):
        t = jnp.maximum(s - 1, 0) if lagged else s
        return t // nb, t % nb

    a_kwargs = dict(pipeline_mode=pl.Buffered(1)) if single_buffer_a else {}
    in_specs = [pl.BlockSpec((tm, k), lambda s: (cur(s)[0], 0), **a_kwargs)]
    args = [a]
    if use_halo:
        in_specs.append(pl.BlockSpec((BF16_ROWS, k), lambda s: (jnp.maximum(cur(s)[0] * hb - 1, 0), 0)))
        args.append(a)
    for width, first in w_col_blocks:
        in_specs.append(pl.BlockSpec((k, width), lambda s, first=first: (0, first + cur(s)[1])))
        args.append(w)
    if row_ssq is not None:
        in_specs.append(pl.BlockSpec((tm, LANES), lambda s: (cur(s)[0], 0)))
        args.append(row_ssq[0])
        if use_halo:
            in_specs.append(pl.BlockSpec((BF16_ROWS, LANES), lambda s: (jnp.maximum(cur(s)[0] * hb - 1, 0), 0)))
            args.append(row_ssq[0])
    for arr in col_aux:
        in_specs.append(pl.BlockSpec((None,) + arr.shape[1:], lambda s: (prev(s)[1], 0, 0)))
        args.append(arr)
    for arr in row_aux:
        in_specs.append(pl.BlockSpec((tm, arr.shape[1]), lambda s: (prev(s)[0], 0)))
        args.append(arr)
    scratch = ([pltpu.VMEM((hrows, k), BF16)] if use_halo else [])
    if lagged:
        scratch += [pltpu.VMEM((hrows, wn), F32)] * 2
    else:
        scratch += [pltpu.VMEM((tm // m_parts + hrows - tm, wn), F32)] * m_parts
    kern = functools.partial(
        _pipelined_kernel, epilogue=epilogue, n_w=len(w_col_blocks), n_aux=len(col_aux) + len(row_aux),
        use_halo=use_halo, lagged=lagged, m_parts=m_parts,
        scale_d=row_ssq[1] if row_ssq is not None else 0, nb=nb,
        n_tiles=n_tiles,
        tiles_per_seq=(halo_seq_len // tm) if use_halo else 1)
    return pl.pallas_call(
        kern,
        grid=(n_tiles + (1 if lagged else 0),),
        in_specs=in_specs,
        out_specs=pl.BlockSpec((tm, out_cols), lambda s: prev(s)),
        out_shape=jax.ShapeDtypeStruct((m, nb * out_cols), out_dtype),
        scratch_shapes=scratch,
        compiler_params=_params("arbitrary"),
        name=name,
    )(*args)


def _rope_table_kernel(pos_ref, inv_ref, cos_ref, sin_ref):
    freqs = pos_ref[...].astype(F32) * inv_ref[...]
    lane = lax.broadcasted_iota(jnp.int32, freqs.shape, 1)
    sign = jnp.where(lane < DIFF_HEAD_DIM // 2, -1.0, 1.0)
    cos_ref[...] = jnp.cos(freqs)
    sin_ref[...] = jnp.sin(freqs) * sign


def rope_tables(positions):
    m = positions.size
    inv = 1.0 / (ROPE_THETA ** (jnp.arange(0, DIFF_HEAD_DIM, 2, dtype=F32) / DIFF_HEAD_DIM))
    inv_full = jnp.concatenate([inv, inv]).reshape(1, DIFF_HEAD_DIM)
    tm = _pick(m, 1024)
    shape = jax.ShapeDtypeStruct((m, DIFF_HEAD_DIM), F32)
    return pl.pallas_call(
        _rope_table_kernel,
        grid=(m // tm,),
        in_specs=[pl.BlockSpec((tm, 1), lambda i: (i, 0)),
                  pl.BlockSpec((1, DIFF_HEAD_DIM), lambda i: (0, 0))],
        out_specs=[pl.BlockSpec((tm, DIFF_HEAD_DIM), lambda i: (i, 0))] * 2,
        out_shape=[shape, shape],
        compiler_params=_params("arbitrary"),
        name="rope_tables",
    )(positions.reshape(m, 1), inv_full)


def _diff_attn_kernel(q_ref, k_ref, v_ref, lq1_ref, lk1_ref, lq2_ref, lk2_ref, sw_ref, o_ref,
                      m_ref, l_ref, acc_ref, *, lambda_init, tq):
    qi = pl.program_id(2)
    d = DIFF_HEAD_DIM
    q = q_ref[...]

    m_ref[...] = jnp.full(m_ref.shape, -jnp.inf, F32)
    l_ref[...] = jnp.zeros(l_ref.shape, F32)
    acc_ref[...] = jnp.zeros(acc_ref.shape, F32)

    def kv_block(kb, masked):
        start = pl.multiple_of(kb * tq, tq)
        k = k_ref[pl.ds(start, tq), :]
        v = v_ref[pl.ds(start, tq), :]
        scores = [lax.dot_general(q[:, c * d:(c + 1) * d], k[:, c * d:(c + 1) * d],
                                  (((1,), (1,)), ((), ())), preferred_element_type=F32) for c in range(2)]
        probs = []
        alphas = []
        for c in range(2):
            s = scores[c]
            if masked:
                row = lax.broadcasted_iota(jnp.int32, s.shape, 0)
                col = lax.broadcasted_iota(jnp.int32, s.shape, 1)
                s = jnp.where(col <= row, s, -jnp.inf)
            chunks = [s[:, t * LANES:(t + 1) * LANES] for t in range(tq // LANES)]
            cmax = functools.reduce(jnp.maximum, chunks)
            m_old = m_ref[c]
            m_new = jnp.maximum(m_old, jnp.max(cmax, axis=-1, keepdims=True))
            alpha = jnp.exp2(m_old - m_new)
            ps = [jnp.exp2(ch - m_new) for ch in chunks]
            l_ref[c] = alpha * l_ref[c] + functools.reduce(jnp.add, ps)
            m_ref[c] = m_new
            probs.append(jnp.concatenate(ps, axis=1).astype(v.dtype))
            alphas.append(alpha)
        pv = jnp.dot(jnp.concatenate(probs, axis=0), v, preferred_element_type=F32)
        for c in range(2):
            a2 = jnp.concatenate([alphas[c], alphas[c]], axis=1)
            acc_ref[c] = a2 * acc_ref[c] + pv[c * tq:(c + 1) * tq]

    def body(kb, carry):
        kv_block(kb, False)
        return carry

    lax.fori_loop(0, qi, body, 0)
    kv_block(qi, True)

    lam = (jnp.exp(jnp.sum(lq1_ref[...] * lk1_ref[...], axis=-1, keepdims=True))
           - jnp.exp(jnp.sum(lq2_ref[...] * lk2_ref[...], axis=-1, keepdims=True)) + lambda_init)
    l0 = jnp.sum(l_ref[0], axis=-1, keepdims=True)
    l1 = jnp.sum(l_ref[1], axis=-1, keepdims=True)
    o = acc_ref[0] / l0 - lam * (acc_ref[1] / l1)
    o = o * lax.rsqrt(jnp.mean(o * o, axis=-1, keepdims=True) + EPS)
    o_ref[...] = ((o * sw_ref[...]) * (1.0 - lambda_init)).astype(o_ref.dtype)


def _cast_rows(w, steps):
    k, n = w.shape
    for rows in range(BF16_ROWS, k + 1, BF16_ROWS):
        if k % rows == 0 and k // rows <= steps:
            return rows if rows * n * 4 <= CAST_BLOCK_BYTES else None
    return None


def _plan_casts(weights, steps, lin):
    rows = [None if w.dtype == BF16 else _cast_rows(w, steps) for w in weights]
    arrays, specs, spans = [], [], []
    at_end = False
    for w, r in zip(weights, rows):
        if not r:
            continue
        nb = w.shape[0] // r
        first = steps - nb if at_end else 0
        at_end = not at_end if nb < steps else at_end
        arrays.append(w)
        spans.append((first, nb))
        specs.append(pl.BlockSpec(
            (r, w.shape[1]),
            lambda *ids, first=first, nb=nb: (jnp.clip(lin(*ids) - first, 0, nb - 1), 0)))

    def finish(outputs):
        done = iter(outputs)
        return [next(done) if r else w.astype(BF16) for w, r in zip(weights, rows)]

    return arrays, specs, spans, finish


def _run_casts(step, spans, cast_in, cast_out):
    for (first, nb), src, dst in zip(spans, cast_in, cast_out):
        @pl.when((step >= first) & (step < first + nb))
        def _():
            dst[...] = src[...].astype(dst.dtype)


def diff_attention(qkv, lq1, lk1, lq2, lk2, subln_w, lambda_init, batch, seq_len, *, tq=1024,
                   cast_weights=()):
    m, n = qkv.shape
    d_model = n // 3
    hw = 2 * DIFF_HEAD_DIM
    heads = d_model // hw
    tq = _pick(seq_len, tq)
    nq = seq_len // tq

    def lin(b, h, qi):
        return (b * heads + h) * nq + qi

    cast_arrays, cast_specs, spans, finish = _plan_casts(cast_weights, batch * heads * nq, lin)
    n_cast = len(cast_arrays)

    def kern(*refs):
        main_in, cast_in = refs[:8], refs[8:8 + n_cast]
        o_ref = refs[8 + n_cast]
        cast_out = refs[9 + n_cast:9 + 2 * n_cast]
        scratch = refs[9 + 2 * n_cast:]
        _run_casts(lin(pl.program_id(0), pl.program_id(1), pl.program_id(2)), spans, cast_in, cast_out)
        _diff_attn_kernel(*main_in, o_ref, *scratch, lambda_init=lambda_init, tq=tq)

    vec = lambda a: a.reshape(1, -1)
    lam_spec = pl.BlockSpec((1, DIFF_HEAD_DIM), lambda b, h, qi: (0, 0))
    outs = pl.pallas_call(
        kern,
        grid=(batch, heads, nq),
        in_specs=[
            pl.BlockSpec((tq, hw), lambda b, h, qi: (b * nq + qi, h)),
            pl.BlockSpec((seq_len, hw), lambda b, h, qi: (b, heads + h)),
            pl.BlockSpec((seq_len, hw), lambda b, h, qi: (b, 2 * heads + h)),
            lam_spec, lam_spec, lam_spec, lam_spec,
            pl.BlockSpec((1, hw), lambda b, h, qi: (0, 0)),
        ] + cast_specs,
        out_specs=[pl.BlockSpec((tq, hw), lambda b, h, qi: (b * nq + qi, h))] + cast_specs,
        out_shape=[jax.ShapeDtypeStruct((m, d_model), BF16)]
        + [jax.ShapeDtypeStruct(w.shape, BF16) for w in cast_arrays],
        scratch_shapes=[pltpu.VMEM((2, tq, LANES), F32), pltpu.VMEM((2, tq, LANES), F32),
                        pltpu.VMEM((2, tq, hw), F32)],
        compiler_params=_params("arbitrary", "arbitrary", "arbitrary"),
        name="diff_attention",
    )(qkv, qkv, qkv, vec(lq1), vec(lk1), vec(lq2), vec(lk2), vec(subln_w), *cast_arrays)
    return outs[0], finish(outs[1:])


def _dot_exact(a, b):
    return jnp.dot(a, b, preferred_element_type=F32, precision=lax.Precision.HIGHEST)


def _ssd_kernel(z_ref, x_ref, xh_ref, b_ref, bh_ref, c_ref, ch_ref, dt_ref,
                cwx_ref, cwb_ref, cwc_ref, cbx_ref, cbb_ref, cbc_ref,
                dtb_ref, alog_ref, dexp_ref, nw_ref, sel_ref, expand_ref, o_ref, state_ref, *, heads_per_group):
    g = pl.program_id(1)
    c = pl.program_id(2)
    t = CHUNK
    hpg = heads_per_group
    p = SSM_HEADDIM
    gw = hpg * p
    nh = dt_ref.shape[1]

    @pl.when(c == 0)
    def _():
        state_ref[...] = jnp.zeros(state_ref.shape, F32)

    seq_start = c == 0

    def conv_silu(cur_ref, halo_ref, w_ref, bias_ref):
        halo = halo_ref[...]
        halo = jnp.where(seq_start, jnp.zeros_like(halo), halo)
        ext = jnp.concatenate([halo, cur_ref[...]], axis=0)
        return _silu(_causal_conv_ext(ext, SUBLANES, w_ref[...], bias_ref[...]))

    xc = conv_silu(x_ref, xh_ref, cwx_ref, cbx_ref)
    bc = conv_silu(b_ref, bh_ref, cwb_ref, cbb_ref)
    cc = conv_silu(c_ref, ch_ref, cwc_ref, cbc_ref)

    dtr = dt_ref[...] + dtb_ref[...]
    dtv = jnp.maximum(dtr, 0.0) + jnp.log1p(jnp.exp(-jnp.abs(dtr)))
    a_all = dtv * (-jnp.exp(alog_ref[...]))

    row_t = lax.broadcasted_iota(jnp.int32, (t, t), 0)
    col_t = lax.broadcasted_iota(jnp.int32, (t, t), 1)
    causal = col_t <= row_t
    a_cs_all = _dot_exact(jnp.where(causal, 1.0, 0.0), a_all)
    a_last = a_cs_all[t - 1:t, :]

    a_cs = _dot_exact(a_cs_all, sel_ref[...])
    a_cs_t = a_cs.T

    stacked = jnp.concatenate([dtv, jnp.exp(a_cs_all), jnp.exp(a_last - a_cs_all)], axis=0)
    hi = stacked.astype(BF16)
    lo = (stacked - hi.astype(F32)).astype(BF16)
    expanded = jnp.dot(jnp.concatenate([hi, lo], axis=1), expand_ref[...],
                       preferred_element_type=F32)
    dt_e = expanded[0:t]
    exp_acs_e = expanded[t:2 * t]
    dec_e = expanded[2 * t:3 * t]

    xdt = xc * dt_e
    xdt_b = xdt.astype(BF16)
    bb = bc.astype(BF16)
    cb = cc.astype(BF16)
    cbm = lax.dot_general(cb, bb, (((1,), (1,)), ((), ())), preferred_element_type=F32)

    state = state_ref[...]
    y = jnp.dot(cb, state.astype(BF16), preferred_element_type=F32) * exp_acs_e

    lane = lax.broadcasted_iota(jnp.int32, (t, 2 * p), 1)
    pieces = []
    for pair in range(hpg // 2):
        x_pair = xdt_b[:, pair * 2 * p:(pair + 1) * 2 * p]
        outs = []
        for hh in (2 * pair, 2 * pair + 1):
            seg = a_cs[:, hh:hh + 1] - a_cs_t[hh:hh + 1, :]
            decay = jnp.exp(jnp.where(causal, seg, -jnp.inf))
            outs.append(jnp.dot((cbm * decay).astype(BF16), x_pair, preferred_element_type=F32))
        pieces.append(jnp.where(lane < p, outs[0], outs[1]))
    y = y + jnp.concatenate(pieces, axis=1) + dexp_ref[...] * xc

    new_contrib = jnp.dot(bc.T.astype(BF16), (xdt * dec_e).astype(BF16), preferred_element_type=F32)
    state_ref[...] = state * exp_acs_e[t - 1:t, :] + new_contrib

    gn = y * _silu(z_ref[...])
    gn = gn * lax.rsqrt(jnp.mean(gn * gn, axis=-1, keepdims=True) + EPS)
    o_ref[...] = (gn * nw_ref[...]).astype(o_ref.dtype)


def ssd_mixer(zx, bcm, dt, conv_w, conv_b, dt_bias, a_log, d_skip, norm_w, batch, seq_len, cast_weights=()):
    m = zx.shape[0]
    d_inner = zx.shape[1] // 2
    nh = dt.shape[1]
    hpg = nh // SSM_GROUPS
    gw = d_inner // SSM_GROUPS
    n = SSM_STATE
    t = CHUNK
    nc = seq_len // t
    hb = t // SUBLANES
    ng = SSM_GROUPS
    xb0 = d_inner // gw
    cwb0 = d_inner // n

    def row(b, g, c):
        return b * nc + c

    def halo(b, g, c):
        return jnp.maximum((b * nc + c) * hb - 1, 0)

    d_exp = jnp.repeat(d_skip.astype(F32), SSM_HEADDIM).reshape(1, d_inner)
    conv_b2 = conv_b.reshape(1, -1)
    head = jnp.arange(nh)
    sel = (head[None, :, None] == (jnp.arange(ng) * hpg)[:, None, None] + head[None, None, :]) \
        & (head[None, None, :] < hpg)
    chan_head = jnp.arange(gw) // SSM_HEADDIM
    expand = head[None, :, None] == (jnp.arange(ng) * hpg)[:, None, None] + chan_head[None, None, :]
    sel = sel.astype(F32)
    expand = jnp.concatenate([expand, expand], axis=1).astype(BF16)
    def lin(b, g, c):
        return (b * ng + g) * nc + c

    cast_arrays, cast_specs, spans, finish = _plan_casts(cast_weights, batch * ng * nc, lin)
    n_cast = len(cast_arrays)
    n_main = 20

    def kern(*refs):
        main_in, cast_in = refs[:n_main], refs[n_main:n_main + n_cast]
        o_ref = refs[n_main + n_cast]
        cast_out = refs[n_main + n_cast + 1:n_main + 2 * n_cast + 1]
        scratch = refs[n_main + 2 * n_cast + 1:]
        _run_casts(lin(pl.program_id(0), pl.program_id(1), pl.program_id(2)), spans, cast_in, cast_out)
        _ssd_kernel(*main_in, o_ref, *scratch, heads_per_group=hpg)

    outs = pl.pallas_call(
        kern,
        grid=(batch, ng, nc),
        in_specs=[
            pl.BlockSpec((t, gw), lambda b, g, c: (row(b, g, c), g)),
            pl.BlockSpec((t, gw), lambda b, g, c: (row(b, g, c), xb0 + g)),
            pl.BlockSpec((SUBLANES, gw), lambda b, g, c: (halo(b, g, c), xb0 + g)),
            pl.BlockSpec((t, n), lambda b, g, c: (row(b, g, c), g)),
            pl.BlockSpec((SUBLANES, n), lambda b, g, c: (halo(b, g, c), g)),
            pl.BlockSpec((t, n), lambda b, g, c: (row(b, g, c), ng + g)),
            pl.BlockSpec((SUBLANES, n), lambda b, g, c: (halo(b, g, c), ng + g)),
            pl.BlockSpec((t, nh), lambda b, g, c: (row(b, g, c), 0)),
            pl.BlockSpec((SSM_CONV, gw), lambda b, g, c: (0, g)),
            pl.BlockSpec((SSM_CONV, n), lambda b, g, c: (0, cwb0 + g)),
            pl.BlockSpec((SSM_CONV, n), lambda b, g, c: (0, cwb0 + ng + g)),
            pl.BlockSpec((1, gw), lambda b, g, c: (0, g)),
            pl.BlockSpec((1, n), lambda b, g, c: (0, cwb0 + g)),
            pl.BlockSpec((1, n), lambda b, g, c: (0, cwb0 + ng + g)),
            pl.BlockSpec((1, nh), lambda b, g, c: (0, 0)),
            pl.BlockSpec((1, nh), lambda b, g, c: (0, 0)),
            pl.BlockSpec((1, gw), lambda b, g, c: (0, g)),
            pl.BlockSpec((1, gw), lambda b, g, c: (0, g)),
            pl.BlockSpec((None, nh, nh), lambda b, g, c: (g, 0, 0)),
            pl.BlockSpec((None, 2 * nh, gw), lambda b, g, c: (g, 0, 0)),
        ] + cast_specs,
        out_specs=[pl.BlockSpec((t, gw), lambda b, g, c: (row(b, g, c), g))] + cast_specs,
        out_shape=[jax.ShapeDtypeStruct((m, d_inner), BF16)]
        + [jax.ShapeDtypeStruct(w.shape, BF16) for w in cast_arrays],
        scratch_shapes=[pltpu.VMEM((n, gw), F32)],
        compiler_params=_params("arbitrary", "arbitrary", "arbitrary"),
        name="ssd_mixer",
    )(zx, zx, zx, bcm, bcm, bcm, bcm, dt, conv_w, conv_w, conv_w, conv_b2, conv_b2, conv_b2,
      dt_bias.reshape(1, nh), a_log.reshape(1, nh), d_exp, norm_w.reshape(1, d_inner), sel, expand,
      *cast_arrays)
    return outs[0], finish(outs[1:])


def mamba_layer(xf, act, w_in, conv_w, conv_b, dt_bias, a_log, d_skip, norm_w, w_out, batch, seq_len,
                next_norm_w, cast_in_proj=(), cast_ssd=(), cast_out_proj=()):
    u, rs = act
    nh = dt_bias.shape[0]
    d_inner = norm_w.shape[0]
    bc_cols = 2 * SSM_GROUPS * SSM_STATE
    w_in = w_in.astype(BF16)
    zx, c_in = matmul(u, w_in, col0=0, n=2 * d_inner, tm=1024, tn=1024, row_ssq=rs, single_buffer_a=True,
                      cast_weights=list(cast_in_proj))
    bcm = matmul(u, w_in, col0=2 * d_inner, n=bc_cols, tm=1024, tn=1024, row_ssq=rs)
    dt = matmul(u, w_in, col0=2 * d_inner + bc_cols, n=nh, tm=1024, tn=nh, row_ssq=rs)
    yn, c_ssd = ssd_mixer(zx, bcm, dt, conv_w, conv_b, dt_bias, a_log, d_skip, norm_w, batch, seq_len,
                          cast_weights=(w_out,) + tuple(cast_ssd))
    xf, (xw, ssq), c_out = matmul(yn, c_ssd[0], tm=1024, tn=512, residual=xf, single_buffer_a=True,
                                  norm_w=next_norm_w, cast_weights=list(cast_out_proj))
    return xf, (xw, (ssq, xf.shape[1])), c_in, c_ssd[1:], c_out


def attention_layer(xf, act, cos, sin, w_qkv, lq1, lk1, lq2, lk2, subln_w, w_o, lambda_init, batch, seq_len,
                    next_norm_w, cast_weights=()):
    u, rs = act
    d_model = w_o.shape[0]
    tn = _pick(d_model, 1024)
    nb = w_qkv.shape[1] // tn
    qkv = pipelined_matmul(u, w_qkv.astype(BF16), [(tn, 0)],
                           epilogue=functools.partial(_epilogue_rope, rope_tiles=2 * d_model // tn),
                           row_aux=(cos, sin), out_cols=tn, out_dtype=BF16, nb=nb, row_ssq=rs,
                           single_buffer_a=False, name="qkv_rope")
    o, casted = diff_attention(qkv, lq1, lk1, lq2, lk2, subln_w, lambda_init, batch, seq_len,
                               cast_weights=cast_weights)
    xf, (xw, ssq) = matmul(o, w_o.astype(BF16), tm=1024, tn=512, residual=xf, norm_w=next_norm_w)
    return xf, (xw, (ssq, d_model)), casted


def ffn_layer(xf, act, w_up, conv_w, conv_b, w_down, seq_len, next_norm_w, cast_down_proj=()):
    u, rs = act
    d_ff = w_up.shape[1] // 2
    tn = _pick(d_ff, 256)
    nb = d_ff // tn
    cw = conv_w.astype(F32).reshape(FFN_CONV, 2, nb, tn).transpose(2, 0, 1, 3).reshape(nb, FFN_CONV, 2 * tn)
    cb = conv_b.astype(F32).reshape(2, nb, tn).transpose(1, 0, 2).reshape(nb, 1, 2 * tn)
    g = pipelined_matmul(u, w_up.astype(BF16), [(tn, 0), (tn, nb)], epilogue=_epilogue_conv_gate,
                         col_aux=(cw, cb), out_cols=tn, out_dtype=BF16, nb=nb, halo_seq_len=seq_len,
                         tm=2048, lagged=False, m_parts=4, single_buffer_a=True, row_ssq=rs, name="ffn_up")
    outs = matmul(g, w_down.astype(BF16), tm=1024, tn=256, residual=xf, single_buffer_a=True,
                  norm_w=next_norm_w, cast_weights=list(cast_down_proj))
    if next_norm_w is None:
        return outs[0], None, outs[1]
    xf, (xw, ssq), c_down = outs
    return xf, (xw, (ssq, xf.shape[1])), c_down


def kernel(x, positions, l0_norm_mix, l0_m_w_in, l0_m_conv_w, l0_m_conv_b, l0_m_dt_bias, l0_m_a_log, l0_m_d, l0_m_norm, l0_m_w_out, l0_norm_ffn, l0_f_w_up, l0_f_conv_w, l0_f_conv_b, l0_f_w_down, l1_norm_mix, l1_a_w_qkv, l1_a_lq1, l1_a_lk1, l1_a_lq2, l1_a_lk2, l1_a_subln, l1_a_w_o, l1_norm_ffn, l1_f_w_up, l1_f_conv_w, l1_f_conv_b, l1_f_w_down, l2_norm_mix, l2_m_w_in, l2_m_conv_w, l2_m_conv_b, l2_m_dt_bias, l2_m_a_log, l2_m_d, l2_m_norm, l2_m_w_out, l2_norm_ffn, l2_f_w_up, l2_f_conv_w, l2_f_conv_b, l2_f_w_down, l3_norm_mix, l3_a_w_qkv, l3_a_lq1, l3_a_lk1, l3_a_lq2, l3_a_lk2, l3_a_subln, l3_a_w_o, l3_norm_ffn, l3_f_w_up, l3_f_conv_w, l3_f_conv_b, l3_f_w_down, final_norm):
    norm_mix = [l0_norm_mix, l1_norm_mix, l2_norm_mix, l3_norm_mix]
    norm_ffn = [l0_norm_ffn, l1_norm_ffn, l2_norm_ffn, l3_norm_ffn]
    mixer_params = [
        (l0_m_w_in, l0_m_conv_w, l0_m_conv_b, l0_m_dt_bias, l0_m_a_log, l0_m_d, l0_m_norm, l0_m_w_out),
        (l1_a_w_qkv, l1_a_lq1, l1_a_lk1, l1_a_lq2, l1_a_lk2, l1_a_subln, l1_a_w_o),
        (l2_m_w_in, l2_m_conv_w, l2_m_conv_b, l2_m_dt_bias, l2_m_a_log, l2_m_d, l2_m_norm, l2_m_w_out),
        (l3_a_w_qkv, l3_a_lq1, l3_a_lk1, l3_a_lq2, l3_a_lk2, l3_a_subln, l3_a_w_o),
    ]
    ffn_params = [
        (l0_f_w_up, l0_f_conv_w, l0_f_conv_b, l0_f_w_down),
        (l1_f_w_up, l1_f_conv_w, l1_f_conv_b, l1_f_w_down),
        (l2_f_w_up, l2_f_conv_w, l2_f_conv_b, l2_f_w_down),
        (l3_f_w_up, l3_f_conv_w, l3_f_conv_b, l3_f_w_down),
    ]
    batch, seq_len, d_model = x.shape
    xf = x.reshape(batch * seq_len, d_model)
    cos, sin = rope_tables(positions)
    n_layers = len(norm_mix)
    act = (rmsnorm(xf, norm_mix[0], BF16), None)
    for i in range(n_layers):
        w_up, cw, cb, w_down = ffn_params[i]
        nxt = list(mixer_params[i + 1]) if i + 1 < n_layers else None
        if i % N_MIXERS == 0:
            xf, act, (w_up,), (w_down,), c_out = mamba_layer(
                xf, act, *mixer_params[i], batch, seq_len, norm_ffn[i], cast_in_proj=[w_up],
                cast_ssd=[w_down], cast_out_proj=[nxt[0]] if nxt else [])
            if nxt:
                nxt[0] = c_out[0]
            cast_down = [nxt[-1], ffn_params[i + 1][0]] if nxt else []
        else:
            lambda_init = 0.8 - 0.6 * math.exp(-0.3 * i)
            xf, act, (w_up, w_down) = attention_layer(xf, act, cos, sin, *mixer_params[i], lambda_init, batch,
                                                      seq_len, norm_ffn[i], cast_weights=[w_up, w_down])
            cast_down = [nxt[0]] if nxt else []
        xf, act, c_down = ffn_layer(xf, act, w_up, cw, cb, w_down, seq_len,
                                    norm_mix[i + 1] if nxt else None, cast_down_proj=cast_down)
        if nxt:
            nxt[-1 if i % N_MIXERS == 0 else 0] = c_down[0]
            mixer_params[i + 1] = tuple(nxt)
            if len(c_down) > 1:
                ffn_params[i + 1] = (c_down[1],) + tuple(ffn_params[i + 1][1:])
    return rmsnorm(xf, final_norm, F32).reshape(batch, seq_len, d_model)
```
